```python
import math
import jax, jax.numpy as jnp
from jax import lax
import numpy as np

D_MODEL = 1024
BATCH = 8
SEQ = 2048
DEPTH = 1

MEM_LEN = 256
D_MIX = D_MODEL
D_SSD = D_MIX // 2
SSD_HEAD_DIM = 64
SSD_HEADS = D_SSD // SSD_HEAD_DIM
SSD_GROUPS = 2
SSD_STATE = 128
CONV_WIDTH = 4
CHUNK = 128
D_XBC = D_SSD + 2 * SSD_GROUPS * SSD_STATE
D_DIFF = D_MIX - D_SSD
DIFF_HEAD_DIM = 64
DIFF_HEADS = D_DIFF // (2 * DIFF_HEAD_DIM)
Q_BLOCK = 128
ROPE_THETA = 10000.0
D_IN = D_SSD + D_XBC + SSD_HEADS + 3 * D_DIFF
MEM_HEADS = 4
MEM_HEAD_DIM = D_MODEL // MEM_HEADS
D_FF = 4 * D_MODEL
NORM_EPS = 1e-6

kernel_name = "hybrid_ssd_diffattn_memxattn_sqrelu"


def rms_norm(x, g, eps=NORM_EPS):
    xf = x.astype(jnp.float32)
    y = xf * lax.rsqrt(jnp.mean(xf * xf, axis=-1, keepdims=True) + eps)
    return (y * g.astype(jnp.float32)).astype(x.dtype)


def rope_tables(positions, dim):
    inv = ROPE_THETA ** (-jnp.arange(0, dim, 2, dtype=jnp.float32) / dim)
    ang = positions.astype(jnp.float32)[..., None] * inv
    ang = jnp.concatenate([ang, ang], axis=-1)
    return jnp.cos(ang), jnp.sin(ang)


def apply_rope(x, cos, sin):
    extra = x.ndim - 3
    shp = cos.shape[:2] + (1,) * extra + cos.shape[-1:]
    cos, sin = cos.reshape(shp), sin.reshape(shp)
    x1, x2 = jnp.split(x, 2, axis=-1)
    rot = jnp.concatenate([-x2, x1], axis=-1)
    return (x * cos + rot * sin).astype(x.dtype)


def ssd_chunked_scan(xs, dt, a, bm, cm):
    Bsz, L, H, P = xs.shape
    G, N = SSD_GROUPS, SSD_STATE
    R = H // G
    nc = L // CHUNK
    f32 = jnp.float32
    xdt = (xs.astype(f32) * dt[..., None]).reshape(Bsz, nc, CHUNK, G, R, P)
    da = (dt * a).reshape(Bsz, nc, CHUNK, G, R)
    bc = bm.astype(f32).reshape(Bsz, nc, CHUNK, G, N)
    cc = cm.astype(f32).reshape(Bsz, nc, CHUNK, G, N)
    cum = jnp.cumsum(da, axis=2)
    causal = jnp.tril(jnp.ones((CHUNK, CHUNK), dtype=bool))
    seg = cum[:, :, :, None] - cum[:, :, None, :]
    decay = jnp.exp(jnp.where(causal[None, None, :, :, None, None], seg, -jnp.inf))
    cb = jnp.einsum('bclgn,bcsgn->bclsg', cc, bc)
    y_diag = jnp.einsum('bclsg,bclsgr,bcsgrp->bclgrp', cb, decay, xdt)
    to_end = jnp.exp(cum[:, :, -1:] - cum)
    chunk_states = jnp.einsum('bclgn,bclgr,bclgrp->bcgrpn', bc, to_end, xdt)
    chunk_decay = jnp.exp(cum[:, :, -1])

    def carry_state(h, inp):
        s_c, d_c = inp
        return h * d_c[..., None, None] + s_c, h

    h0 = jnp.zeros((Bsz, G, R, P, N), f32)
    _, h_in = lax.scan(carry_state, h0,
                       (jnp.moveaxis(chunk_states, 1, 0), jnp.moveaxis(chunk_decay, 1, 0)))
    h_in = jnp.moveaxis(h_in, 0, 1)
    y_off = jnp.einsum('bclgn,bcgrpn,bclgr->bclgrp', cc, h_in, jnp.exp(cum))
    return (y_diag + y_off).reshape(Bsz, L, H, P).astype(xs.dtype)


def ssd_mixer(z, xbc, dt_raw, conv_w, conv_b, dt_bias, a_log, d_skip, norm_w):
    Bsz, L, _ = z.shape
    xbc = lax.conv_general_dilated(
        xbc, conv_w, window_strides=(1,), padding=[(CONV_WIDTH - 1, 0)],
        dimension_numbers=('NWC', 'WIO', 'NWC'), feature_group_count=D_XBC)
    xbc = jax.nn.silu(xbc + conv_b)
    xs, bm, cm = jnp.split(xbc, [D_SSD, D_SSD + SSD_GROUPS * SSD_STATE], axis=-1)
    xs = xs.reshape(Bsz, L, SSD_HEADS, SSD_HEAD_DIM)
    bm = bm.reshape(Bsz, L, SSD_GROUPS, SSD_STATE)
    cm = cm.reshape(Bsz, L, SSD_GROUPS, SSD_STATE)
    dt = jax.nn.softplus((dt_raw + dt_bias).astype(jnp.float32))
    a = -jnp.exp(a_log.astype(jnp.float32))
    y = ssd_chunked_scan(xs, dt, a, bm, cm)
    y = (y + d_skip[:, None] * xs).reshape(Bsz, L, D_SSD)
    return rms_norm(y * jax.nn.silu(z), norm_w)


def diff_attention(q, k, v, cos, sin, lam, subln_w, lambda_init):
    Bsz, L, _ = q.shape
    q = apply_rope(q.reshape(Bsz, L, DIFF_HEADS, 2, DIFF_HEAD_DIM), cos, sin)
    k = apply_rope(k.reshape(Bsz, L, DIFF_HEADS, 2, DIFF_HEAD_DIM), cos, sin)
    q = q.transpose(0, 2, 3, 1, 4)
    k = k.transpose(0, 2, 3, 1, 4)
    v = v.reshape(Bsz, L, DIFF_HEADS, 2 * DIFF_HEAD_DIM).transpose(0, 2, 1, 3)
    scale = DIFF_HEAD_DIM ** -0.5
    key_pos = jnp.arange(L)

    def block(i):
        start = i * Q_BLOCK
        qb = lax.dynamic_slice_in_dim(q, start, Q_BLOCK, axis=3)
        s = jnp.einsum('bhcqd,bhckd->bhcqk', qb, k,
                       preferred_element_type=jnp.float32) * scale
        causal = (start + jnp.arange(Q_BLOCK))[:, None] >= key_pos[None, :]
        s = jnp.where(causal, s, -jnp.inf)
        p = jax.nn.softmax(s, axis=-1)
        attn = p[:, :, 0] - lam * p[:, :, 1]
        return jnp.einsum('bhqk,bhkv->bhqv', attn.astype(v.dtype), v)

    o = lax.map(block, jnp.arange(L // Q_BLOCK))
    o = o.transpose(1, 0, 3, 2, 4).reshape(Bsz, L, DIFF_HEADS, 2 * DIFF_HEAD_DIM)
    o = rms_norm(o, subln_w) * (1.0 - lambda_init)
    return o.reshape(Bsz, L, D_DIFF)


def cross_attention(h, mem_h, wq, wk, wv, wo):
    Bsz, L, _ = h.shape
    M = mem_h.shape[1]
    q = (h @ wq).reshape(Bsz, L, MEM_HEADS, MEM_HEAD_DIM)
    k = (mem_h @ wk).reshape(Bsz, M, MEM_HEADS, MEM_HEAD_DIM)
    v = (mem_h @ wv).reshape(Bsz, M, MEM_HEADS, MEM_HEAD_DIM)
    s = jnp.einsum('bqhd,bkhd->bhqk', q, k,
                   preferred_element_type=jnp.float32) * (MEM_HEAD_DIM ** -0.5)
    p = jax.nn.softmax(s, axis=-1)
    o = jnp.einsum('bhqk,bkhd->bqhd', p.astype(v.dtype), v).reshape(Bsz, L, D_MODEL)
    return o @ wo


def setup_inputs(seed: int = 0) -> dict:
    key = jax.random.key(seed)
    ks = jax.random.split(key, 32)
    f32 = jnp.float32

    def nrm(k, shape, scale):
        return jax.random.normal(k, shape, f32) * scale

    def gain(k, dim):
        return 1.0 + nrm(k, (DEPTH, dim), 0.02)

    x = nrm(ks[0], (BATCH, SEQ, D_MODEL), 1.0)
    mem = nrm(ks[1], (BATCH, MEM_LEN, D_MODEL), 1.0)
    offset = jax.random.randint(ks[2], (BATCH, 1), 0, 1024, dtype=jnp.int32)
    positions = offset + jnp.arange(SEQ, dtype=jnp.int32)[None, :]
    dt0 = jnp.exp(jax.random.uniform(ks[3], (DEPTH, SSD_HEADS), f32,
                                     math.log(1e-3), math.log(1e-1)))
    dt_bias = dt0 + jnp.log(-jnp.expm1(-dt0))
    a_log = jnp.log(jax.random.uniform(ks[4], (DEPTH, SSD_HEADS), f32, 1.0, 16.0))
    return {
        "x": x,
        "mem": mem,
        "positions": positions,
        "norm_mix_pre": gain(ks[5], D_MODEL),
        "norm_mix_post": gain(ks[6], D_MODEL),
        "norm_mem_q": gain(ks[7], D_MODEL),
        "norm_mem_kv": gain(ks[8], D_MODEL),
        "norm_mem_post": gain(ks[9], D_MODEL),
        "norm_mlp_pre": gain(ks[10], D_MODEL),
        "norm_mlp_post": gain(ks[11], D_MODEL),
        "w_in": nrm(ks[12], (DEPTH, D_MODEL, D_IN), D_MODEL ** -0.5),
        "conv_w": nrm(ks[13], (DEPTH, CONV_WIDTH, 1, D_XBC), CONV_WIDTH ** -0.5),
        "conv_b": nrm(ks[14], (DEPTH, D_XBC), 0.02),
        "dt_bias": dt_bias,
        "a_log": a_log,
        "d_skip": 1.0 + nrm(ks[15], (DEPTH, SSD_HEADS), 0.1),
        "ssd_norm_w": gain(ks[16], D_SSD),
        "lambda_q1": nrm(ks[17], (DEPTH, DIFF_HEAD_DIM), 0.1),
        "lambda_k1": nrm(ks[18], (DEPTH, DIFF_HEAD_DIM), 0.1),
        "lambda_q2": nrm(ks[19], (DEPTH, DIFF_HEAD_DIM), 0.1),
        "lambda_k2": nrm(ks[20], (DEPTH, DIFF_HEAD_DIM), 0.1),
        "subln_w": gain(ks[21], 2 * DIFF_HEAD_DIM),
        "w_out": nrm(ks[22], (DEPTH, D_MIX, D_MODEL), D_MIX ** -0.5),
        "w_mq": nrm(ks[23], (DEPTH, D_MODEL, D_MODEL), D_MODEL ** -0.5),
        "w_mk": nrm(ks[24], (DEPTH, D_MODEL, D_MODEL), D_MODEL ** -0.5),
        "w_mv": nrm(ks[25], (DEPTH, D_MODEL, D_MODEL), D_MODEL ** -0.5),
        "w_mo": nrm(ks[26], (DEPTH, D_MODEL, D_MODEL), D_MODEL ** -0.5),
        "w_up": nrm(ks[27], (DEPTH, D_MODEL, D_FF), D_MODEL ** -0.5),
        "w_down": nrm(ks[28], (DEPTH, D_FF, D_MODEL), D_FF ** -0.5),
    }


def reference(x, mem, positions, norm_mix_pre, norm_mix_post, norm_mem_q, norm_mem_kv,
              norm_mem_post, norm_mlp_pre, norm_mlp_post, w_in, conv_w, conv_b, dt_bias,
              a_log, d_skip, ssd_norm_w, lambda_q1, lambda_k1, lambda_q2, lambda_k2,
              subln_w, w_out, w_mq, w_mk, w_mv, w_mo, w_up, w_down):
    cos, sin = rope_tables(positions, DIFF_HEAD_DIM)
    split_at = [D_SSD, D_SSD + D_XBC, D_SSD + D_XBC + SSD_HEADS,
                D_SSD + D_XBC + SSD_HEADS + D_DIFF,
                D_SSD + D_XBC + SSD_HEADS + 2 * D_DIFF]
    for l in range(DEPTH):
        lambda_init = 0.8 - 0.6 * math.exp(-0.3 * l)
        h = rms_norm(x, norm_mix_pre[l])
        z, xbc, dt_raw, q, k, v = jnp.split(h @ w_in[l], split_at, axis=-1)
        y_ssd = ssd_mixer(z, xbc, dt_raw, conv_w[l], conv_b[l], dt_bias[l], a_log[l],
                          d_skip[l], ssd_norm_w[l])
        lam = (jnp.exp(jnp.sum(lambda_q1[l].astype(jnp.float32) * lambda_k1[l]))
               - jnp.exp(jnp.sum(lambda_q2[l].astype(jnp.float32) * lambda_k2[l]))
               + lambda_init)
        y_diff = diff_attention(q, k, v, cos, sin, lam, subln_w[l], lambda_init)
        mixed = jnp.concatenate([y_ssd, y_diff], axis=-1) @ w_out[l]
        x = x + rms_norm(mixed, norm_mix_post[l])
        h = rms_norm(x, norm_mem_q[l])
        mem_h = rms_norm(mem, norm_mem_kv[l])
        c = cross_attention(h, mem_h, w_mq[l], w_mk[l], w_mv[l], w_mo[l])
        x = x + rms_norm(c, norm_mem_post[l])
        h = rms_norm(x, norm_mlp_pre[l])
        m = jnp.square(jax.nn.relu(h @ w_up[l])) @ w_down[l]
        x = x + rms_norm(m, norm_mlp_post[l])
    return x
```

```python
import functools
import math

import jax
import jax.numpy as jnp
from jax import lax
from jax.experimental import pallas as pl
from jax.experimental.pallas import tpu as pltpu

F32 = jnp.float32
BF16 = jnp.bfloat16

D_MODEL = 1024
D_SSD = 512
SSD_HEAD_DIM = 64
SSD_HEADS = D_SSD // SSD_HEAD_DIM
SSD_GROUPS = 2
SSD_STATE = 128
CONV_WIDTH = 4
CHUNK = 128
D_XBC = D_SSD + 2 * SSD_GROUPS * SSD_STATE
D_DIFF = D_MODEL - D_SSD
DIFF_HEAD_DIM = 64
DIFF_HEADS = D_DIFF // (2 * DIFF_HEAD_DIM)
ROPE_THETA = 10000.0
MEM_HEADS = 4
MEM_HEAD_DIM = D_MODEL // MEM_HEADS
D_FF = 4 * D_MODEL
NORM_EPS = 1e-6
LAMBDA_INIT = 0.8 - 0.6 * math.exp(-0.3 * 0)

LANES = 128
CONV_HALO = 8
VMEM_LIMIT = 56 * 1024 * 1024


def _dot(a, b):
    return jnp.dot(a, b, preferred_element_type=F32)


def _dot_nt(a, b):
    return lax.dot_general(a, b, (((1,), (1,)), ((), ())), preferred_element_type=F32)


def _rms(x, g):
    ms = jnp.mean(x * x, axis=-1, keepdims=True)
    return x * lax.rsqrt(ms + NORM_EPS) * g


def _const_spec(shape):
    nd = len(shape)
    return pl.BlockSpec(shape, lambda *_: (0,) * nd)


def _params(*sem):
    return pltpu.CompilerParams(dimension_semantics=sem, vmem_limit_bytes=VMEM_LIMIT)


def _inproj_kernel(x_ref, g_ref, cos_ref, sin_ref, wz_ref, wxbc_ref, wdt_ref, wq_ref, wk_ref, wv_ref,
                   z_ref, xbc_ref, dt_ref, q_ref, k_ref, v_ref):
    h = _rms(x_ref[...], g_ref[...]).astype(BF16)
    z_ref[...] = _dot(h, wz_ref[...]).astype(BF16)
    xbc_ref[...] = _dot(h, wxbc_ref[...])
    dt_ref[...] = _dot(h, wdt_ref[...])
    v_ref[...] = _dot(h, wv_ref[...]).astype(BF16)
    cos = cos_ref[...]
    sin = sin_ref[...]
    lane = lax.broadcasted_iota(jnp.int32, cos.shape, 1)
    first_half = (lane % DIFF_HEAD_DIM) < DIFF_HEAD_DIM // 2
    for w_ref, o_ref, scale in ((wq_ref, q_ref, DIFF_HEAD_DIM ** -0.5), (wk_ref, k_ref, 1.0)):
        a = _dot(h, w_ref[...])
        for c in range(D_DIFF // LANES):
            ac = a[:, c * LANES:(c + 1) * LANES]
            rot = jnp.where(first_half,
                            pltpu.roll(ac, LANES - DIFF_HEAD_DIM // 2, 1),
                            pltpu.roll(ac, DIFF_HEAD_DIM // 2, 1))
            o_ref[:, c * LANES:(c + 1) * LANES] = ((ac * cos + rot * sin) * scale).astype(BF16)


def _inproj(x2, g, cos, sin, wz, wxbc, wdt, wq, wk, wv, tm):
    T = x2.shape[0]
    row = lambda n: pl.BlockSpec((tm, n), lambda i: (i, 0))
    return pl.pallas_call(
        _inproj_kernel,
        grid=(T // tm,),
        in_specs=[row(D_MODEL), _const_spec((1, D_MODEL)), row(LANES), row(LANES),
                  _const_spec(wz.shape), _const_spec(wxbc.shape), _const_spec(wdt.shape),
                  _const_spec(wq.shape), _const_spec(wk.shape), _const_spec(wv.shape)],
        out_specs=[row(D_SSD), row(D_XBC), row(LANES), row(D_DIFF), row(D_DIFF), row(D_DIFF)],
        out_shape=[jax.ShapeDtypeStruct((T, D_SSD), BF16),
                   jax.ShapeDtypeStruct((T, D_XBC), F32),
                   jax.ShapeDtypeStruct((T, LANES), F32),
                   jax.ShapeDtypeStruct((T, D_DIFF), BF16),
                   jax.ShapeDtypeStruct((T, D_DIFF), BF16),
                   jax.ShapeDtypeStruct((T, D_DIFF), BF16)],
        compiler_params=_params("parallel"),
        name="inproj",
    )(x2, g, cos, sin, wz, wxbc, wdt, wq, wk, wv)


def _pair_expand(cols, j, lane):
    return jnp.where(lane < SSD_HEAD_DIM, cols[:, 2 * j:2 * j + 1], cols[:, 2 * j + 1:2 * j + 2])


def _ssd_kernel(xbc_ref, dt_ref, z_ref, cw_ref, cb_ref, dtb_ref, alog_ref, dskip_ref, nw_ref,
                y_ref, xpad_ref, state_ref):
    c = pl.program_id(1)

    @pl.when(c == 0)
    def _():
        xpad_ref[0:CONV_HALO, :] = jnp.zeros((CONV_HALO, D_XBC), F32)
        state_ref[...] = jnp.zeros_like(state_ref)

    xpad_ref[CONV_HALO:CONV_HALO + CHUNK, :] = xbc_ref[...]
    acc = cb_ref[...] + cw_ref[CONV_WIDTH - 1:CONV_WIDTH, :] * xpad_ref[CONV_HALO:CONV_HALO + CHUNK, :]
    for j in range(CONV_WIDTH - 1):
        off = CONV_HALO - (CONV_WIDTH - 1) + j
        acc = acc + cw_ref[j:j + 1, :] * xpad_ref[off:off + CHUNK, :]
    xpad_ref[0:CONV_HALO, :] = xpad_ref[CHUNK:CHUNK + CONV_HALO, :]
    xc = acc * jax.nn.sigmoid(acc)
    xs = xc[:, :D_SSD]
    gn = SSD_GROUPS * SSD_STATE
    bm = xc[:, D_SSD:D_SSD + gn]
    cm = xc[:, D_SSD + gn:]

    dt = jax.nn.softplus(dt_ref[...] + dtb_ref[...])
    da = dt * (-jnp.exp(alog_ref[...]))
    row = lax.broadcasted_iota(jnp.int32, (CHUNK, LANES), 0)
    lane = lax.broadcasted_iota(jnp.int32, (CHUNK, LANES), 1)
    cum = da
    sh = 1
    while sh < CHUNK:
        cum = cum + jnp.where(row >= sh, pltpu.roll(cum, sh, 0), 0.0)
        sh *= 2
    cum_t = cum.T
    cum_last = cum[CHUNK - 1:CHUNK, :]
    ecum = jnp.exp(cum)
    to_end = jnp.exp(cum_last - cum)
    causal = row >= lane

    npairs = SSD_HEADS // 2
    dt_x = [_pair_expand(dt, j, lane) for j in range(npairs)]
    ecum_x = [_pair_expand(ecum, j, lane) for j in range(npairs)]
    toend_x = [_pair_expand(to_end, j, lane) for j in range(npairs)]
    xdt = [xs[:, j * LANES:(j + 1) * LANES] * dt_x[j] for j in range(npairs)]

    heads_per_group = SSD_HEADS // SSD_GROUPS
    pairs_per_group = heads_per_group // 2
    y_pairs = []
    for g in range(SSD_GROUPS):
        bg = bm[:, g * SSD_STATE:(g + 1) * SSD_STATE]
        cg = cm[:, g * SSD_STATE:(g + 1) * SSD_STATE].astype(BF16)
        bg16 = bg.astype(BF16)
        cbm = _dot_nt(cg, bg16)
        st = state_ref[g]
        y_off = _dot(cg, st.astype(BF16))
        upd = []
        for jp in range(pairs_per_group):
            j = g * pairs_per_group + jp
            yp = jnp.zeros((CHUNK, LANES), F32)
            for half in range(2):
                hd = 2 * j + half
                seg = cum[:, hd:hd + 1] - cum_t[hd:hd + 1, :]
                w = (cbm * jnp.where(causal, jnp.exp(seg), 0.0)).astype(BF16)
                in_half = (lane >= SSD_HEAD_DIM) if half else (lane < SSD_HEAD_DIM)
                yp = yp + _dot(w, jnp.where(in_half, xdt[j], 0.0).astype(BF16))
            y_pairs.append(yp + y_off[:, jp * LANES:(jp + 1) * LANES] * ecum_x[j])
            upd.append((xdt[j] * toend_x[j]).astype(BF16))
        upd = jnp.concatenate(upd, axis=1)
        cdec = jnp.concatenate([ecum_x[g * pairs_per_group + jp][CHUNK - 1:CHUNK, :]
                                for jp in range(pairs_per_group)], axis=1)
        state_ref[g] = st * cdec + _dot(bg.T.astype(BF16), upd)

    y = jnp.concatenate(y_pairs, axis=1) + dskip_ref[...] * xs
    zf = z_ref[...].astype(F32)
    y = y * (zf * jax.nn.sigmoid(zf))
    y_ref[...] = _rms(y, nw_ref[...]).astype(BF16)


def _ssd(xbc, dt, z, cw, cb, dtb, alog, dskip, nw, B, S):
    nc = S // CHUNK
    row = lambda n: pl.BlockSpec((CHUNK, n), lambda b, c: (b * nc + c, 0))
    return pl.pallas_call(
        _ssd_kernel,
        grid=(B, nc),
        in_specs=[row(D_XBC), row(LANES), row(D_SSD),
                  _const_spec(cw.shape), _const_spec(cb.shape), _const_spec(dtb.shape),
                  _const_spec(alog.shape), _const_spec(dskip.shape), _const_spec(nw.shape)],
        out_specs=row(D_SSD),
        out_shape=jax.ShapeDtypeStruct((B * S, D_SSD), BF16),
        scratch_shapes=[pltpu.VMEM((CONV_HALO + CHUNK, D_XBC), F32),
                        pltpu.VMEM((SSD_GROUPS, SSD_STATE, D_SSD // SSD_GROUPS), F32)],
        compiler_params=_params("parallel", "arbitrary"),
        name="ssd",
    )(xbc, dt, z, cw, cb, dtb, alog, dskip, nw)


def _diffattn_kernel(lam_ref, subw_ref, q_ref, k_ref, v_ref, o_ref, *, tq):
    i = pl.program_id(2)
    q = q_ref[...]
    lane = lax.broadcasted_iota(jnp.int32, q.shape, 1)
    zero = jnp.zeros_like(q)
    qc = (jnp.where(lane < DIFF_HEAD_DIM, q, zero), jnp.where(lane >= DIFF_HEAD_DIM, q, zero))
    r = lax.broadcasted_iota(jnp.int32, (tq, tq), 0)
    cidx = lax.broadcasted_iota(jnp.int32, (tq, tq), 1)

    def block(j, carry, masked):
        start = pl.multiple_of(j * tq, tq)
        kb = k_ref[pl.ds(start, tq), :]
        vb = v_ref[pl.ds(start, tq), :]
        out = []
        for comp in range(2):
            m, l, a = carry[3 * comp:3 * comp + 3]
            s = _dot_nt(qc[comp], kb)
            if masked:
                s = jnp.where(r >= cidx, s, -jnp.inf)
            mn = jnp.maximum(m, jnp.max(s, axis=-1, keepdims=True))
            p = jnp.exp(s - mn)
            alpha = jnp.exp(m - mn)
            l = alpha * l + jnp.sum(p, axis=-1, keepdims=True)
            a = alpha * a + _dot(p.astype(BF16), vb)
            out += [mn, l, a]
        return tuple(out)

    init = (jnp.full((tq, 1), -jnp.inf, F32), jnp.zeros((tq, 1), F32), jnp.zeros((tq, LANES), F32)) * 2
    carry = lax.fori_loop(0, i, lambda j, cr: block(j, cr, False), init)
    _, l0, a0, _, l1, a1 = block(i, carry, True)

    lv = lam_ref[...]
    lam = (jnp.exp(jnp.sum(lv[0:1] * lv[1:2], axis=-1, keepdims=True))
           - jnp.exp(jnp.sum(lv[2:3] * lv[3:4], axis=-1, keepdims=True)) + LAMBDA_INIT)
    o = a0 / l0 - lam * (a1 / l1)
    o_ref[...] = (_rms(o, subw_ref[...]) * (1.0 - LAMBDA_INIT)).astype(BF16)


def _diffattn(lamv, subw, q, k, v, B, S, tq):
    nq = S // tq
    return pl.pallas_call(
        functools.partial(_diffattn_kernel, tq=tq),
        grid=(B, DIFF_HEADS, nq),
        in_specs=[_const_spec(lamv.shape), _const_spec(subw.shape),
                  pl.BlockSpec((tq, LANES), lambda b, h, i: (b * nq + i, h)),
                  pl.BlockSpec((S, LANES), lambda b, h, i: (b, h)),
                  pl.BlockSpec((S, LANES), lambda b, h, i: (b, h))],
        out_specs=pl.BlockSpec((tq, LANES), lambda b, h, i: (b * nq + i, h)),
        out_shape=jax.ShapeDtypeStruct((B * S, D_DIFF), BF16),
        compiler_params=_params("parallel", "parallel", "arbitrary"),
        name="diffattn",
    )(lamv, subw, q, k, v)


def _outproj_kernel(ys_ref, yd_ref, x_ref, wo_s_ref, wo_d_ref, gpost_ref, gq_ref, wmq_ref, x1_ref, qm_ref):
    mixed = _dot(ys_ref[...], wo_s_ref[...]) + _dot(yd_ref[...], wo_d_ref[...])
    x1 = x_ref[...] + _rms(mixed, gpost_ref[...])
    x1_ref[...] = x1
    hq = _rms(x1, gq_ref[...]).astype(BF16)
    qm_ref[...] = (_dot(hq, wmq_ref[...]) * (MEM_HEAD_DIM ** -0.5)).astype(BF16)


def _outproj(ys, yd, x2, wo_s, wo_d, gpost, gq, wmq, tm):
    T = x2.shape[0]
    row = lambda n: pl.BlockSpec((tm, n), lambda i: (i, 0))
    return pl.pallas_call(
        _outproj_kernel,
        grid=(T // tm,),
        in_specs=[row(D_SSD), row(D_DIFF), row(D_MODEL), _const_spec(wo_s.shape), _const_spec(wo_d.shape),
                  _const_spec(gpost.shape), _const_spec(gq.shape), _const_spec(wmq.shape)],
        out_specs=[row(D_MODEL), row(D_MODEL)],
        out_shape=[jax.ShapeDtypeStruct((T, D_MODEL), F32), jax.ShapeDtypeStruct((T, D_MODEL), BF16)],
        compiler_params=_params("parallel"),
        name="outproj",
    )(ys, yd, x2, wo_s, wo_d, gpost, gq, wmq)


def _memkv_kernel(mem_ref, g_ref, wk_ref, wv_ref, k_ref, v_ref):
    h = _rms(mem_ref[...], g_ref[...]).astype(BF16)
    k_ref[...] = _dot(h, wk_ref[...]).astype(BF16)
    v_ref[...] = _dot(h, wv_ref[...]).astype(BF16)


def _memkv(mem2, g, wk, wv, M):
    R = mem2.shape[0]
    row = pl.BlockSpec((M, D_MODEL), lambda i: (i, 0))
    return pl.pallas_call(
        _memkv_kernel,
        grid=(R // M,),
        in_specs=[row, _const_spec(g.shape), _const_spec(wk.shape), _const_spec(wv.shape)],
        out_specs=[row, row],
        out_shape=[jax.ShapeDtypeStruct((R, D_MODEL), BF16)] * 2,
        compiler_params=_params("parallel"),
        name="memkv",
    )(mem2, g, wk, wv)


def _xattn_kernel(qm_ref, km_ref, vm_ref, x1_ref, wmo_ref, gpost_ref, x2_ref):
    outs = []
    for hd in range(MEM_HEADS):
        sl = slice(hd * MEM_HEAD_DIM, (hd + 1) * MEM_HEAD_DIM)
        s = _dot_nt(qm_ref[:, sl], km_ref[:, sl])
        e = jnp.exp(s - jnp.max(s, axis=-1, keepdims=True))
        p = e / jnp.sum(e, axis=-1, keepdims=True)
        outs.append(_dot(p.astype(BF16), vm_ref[:, sl]).astype(BF16))
    o = jnp.concatenate(outs, axis=1)
    x2_ref[...] = x1_ref[...] + _rms(_dot(o, wmo_ref[...]), gpost_ref[...])


def _xattn(qm, km, vm, x1, wmo, gpost, S, M, tm):
    T = x1.shape[0]
    per_b = S // tm
    row = lambda n: pl.BlockSpec((tm, n), lambda i: (i, 0))
    mem = pl.BlockSpec((M, D_MODEL), lambda i: (i // per_b, 0))
    return pl.pallas_call(
        _xattn_kernel,
        grid=(T // tm,),
        in_specs=[row(D_MODEL), mem, mem, row(D_MODEL), _const_spec(wmo.shape), _const_spec(gpost.shape)],
        out_specs=row(D_MODEL),
        out_shape=jax.ShapeDtypeStruct((T, D_MODEL), F32),
        compiler_params=_params("parallel"),
        name="xattn",
    )(qm, km, vm, x1, wmo, gpost)


def _mlp_kernel(x_ref, gpre_ref, wup_ref, wdn_ref, gpost_ref, o_ref, *, tf):
    x = x_ref[...]
    h = _rms(x, gpre_ref[...]).astype(BF16)
    acc = jnp.zeros(x.shape, F32)
    for c in range(D_FF // tf):
        u = jnp.maximum(_dot(h, wup_ref[:, c * tf:(c + 1) * tf]), 0.0)
        acc = acc + _dot((u * u).astype(BF16), wdn_ref[c * tf:(c + 1) * tf, :])
    o_ref[...] = x + _rms(acc, gpost_ref[...])


def _mlp(x2, gpre, wup, wdn, gpost, tm, tf):
    T = x2.shape[0]
    row = pl.BlockSpec((tm, D_MODEL), lambda i: (i, 0))
    return pl.pallas_call(
        functools.partial(_mlp_kernel, tf=tf),
        grid=(T // tm,),
        in_specs=[row, _const_spec(gpre.shape), _const_spec(wup.shape), _const_spec(wdn.shape),
                  _const_spec(gpost.shape)],
        out_specs=row,
        out_shape=jax.ShapeDtypeStruct((T, D_MODEL), F32),
        compiler_params=_params("parallel"),
        name="mlp",
    )(x2, gpre, wup, wdn, gpost)


def _rope_tables(positions):
    half = DIFF_HEAD_DIM // 2
    inv = ROPE_THETA ** (-jnp.arange(0, DIFF_HEAD_DIM, 2, dtype=F32) / DIFF_HEAD_DIM)
    ang = positions.astype(F32).reshape(-1, 1) * inv
    ang = jnp.concatenate([ang] * (LANES // half), axis=-1)
    sign = jnp.where((jnp.arange(LANES) % DIFF_HEAD_DIM) < half, -1.0, 1.0).astype(F32)
    return jnp.cos(ang), jnp.sin(ang) * sign


def kernel(x, mem, positions, norm_mix_pre, norm_mix_post, norm_mem_q, norm_mem_kv, norm_mem_post,
           norm_mlp_pre, norm_mlp_post, w_in, conv_w, conv_b, dt_bias, a_log, d_skip, ssd_norm_w,
           lambda_q1, lambda_k1, lambda_q2, lambda_k2, subln_w, w_out, w_mq, w_mk, w_mv, w_mo, w_up, w_down):
    B, S, _ = x.shape
    M = mem.shape[1]
    T = B * S
    assert norm_mix_pre.shape[0] == 1, "single-layer trunk"
    x2 = x.reshape(T, D_MODEL)
    cos, sin = _rope_tables(positions)

    w = w_in[0]
    o0 = D_SSD
    o1 = o0 + D_XBC
    o2 = o1 + SSD_HEADS
    o3 = o2 + D_DIFF
    o4 = o3 + D_DIFF
    wz = w[:, :o0].astype(BF16)
    wxbc = w[:, o0:o1].astype(BF16)
    wdt = jnp.pad(w[:, o1:o2], ((0, 0), (0, LANES - SSD_HEADS))).astype(BF16)
    wq = w[:, o2:o3].astype(BF16)
    wk = w[:, o3:o4].astype(BF16)
    wv = w[:, o4:].astype(BF16)

    z, xbc, dt, q, k, v = _inproj(x2, norm_mix_pre, cos, sin, wz, wxbc, wdt, wq, wk, wv, tm=512)

    pad_heads = lambda p: jnp.pad(p, ((0, 0), (0, LANES - SSD_HEADS)))
    y_ssd = _ssd(xbc, dt, z, conv_w[0, :, 0, :], conv_b, pad_heads(dt_bias), pad_heads(a_log),
                 jnp.repeat(d_skip, SSD_HEAD_DIM, axis=1), ssd_norm_w, B, S)

    lamv = jnp.concatenate([lambda_q1, lambda_k1, lambda_q2, lambda_k2], axis=0)
    y_diff = _diffattn(lamv, subln_w, q, k, v, B, S, tq=256)

    wo = w_out[0].astype(BF16)
    x1, qm = _outproj(y_ssd, y_diff, x2, wo[:D_SSD], wo[D_SSD:], norm_mix_post, norm_mem_q,
                      w_mq[0].astype(BF16), tm=512)
    km, vm = _memkv(mem.reshape(B * M, D_MODEL), norm_mem_kv, w_mk[0].astype(BF16), w_mv[0].astype(BF16), M)
    x2b = _xattn(qm, km, vm, x1, w_mo[0].astype(BF16), norm_mem_post, S, M, tm=512)
    out = _mlp(x2b, norm_mlp_pre, w_up[0].astype(BF16), w_down[0].astype(BF16), norm_mlp_post, tm=512, tf=1024)
    return out.reshape(B, S, D_MODEL)
```

```python
import functools
import math

import jax
import jax.numpy as jnp
from jax import lax
from jax.experimental import pallas as pl
from jax.experimental.pallas import tpu as pltpu

F32 = jnp.float32
BF16 = jnp.bfloat16

D_MODEL = 1024
D_SSD = 512
SSD_HEAD_DIM = 64
SSD_HEADS = D_SSD // SSD_HEAD_DIM
SSD_GROUPS = 2
SSD_STATE = 128
CONV_WIDTH = 4
CHUNK = 128
D_XBC = D_SSD + 2 * SSD_GROUPS * SSD_STATE
D_DIFF = D_MODEL - D_SSD
DIFF_HEAD_DIM = 64
DIFF_HEADS = D_DIFF // (2 * DIFF_HEAD_DIM)
ROPE_THETA = 10000.0
MEM_HEADS = 4
MEM_HEAD_DIM = D_MODEL // MEM_HEADS
D_FF = 4 * D_MODEL
NORM_EPS = 1e-6
LAMBDA_INIT = 0.8 - 0.6 * math.exp(-0.3 * 0)

LOG2E = math.log2(math.e)
LANES = 128
ONES_ROWS = 16
CONV_HALO = 8
VMEM_LIMIT = 56 * 1024 * 1024


def _dot(a, b):
    return jnp.dot(a, b, preferred_element_type=F32)


def _dot_nt(a, b):
    return lax.dot_general(a, b, (((1,), (1,)), ((), ())), preferred_element_type=F32)


def _rms(x, g):
    ms = jnp.mean(x * x, axis=-1, keepdims=True)
    return x * lax.rsqrt(ms + NORM_EPS) * g


def _const_spec(shape):
    nd = len(shape)
    return pl.BlockSpec(shape, lambda *_: (0,) * nd)


def _params(*sem):
    return pltpu.CompilerParams(dimension_semantics=sem, vmem_limit_bytes=VMEM_LIMIT)


def _inproj_kernel(x_ref, g_ref, cos_ref, sin_ref, wz_ref, wxbc_ref, wdt_ref, wq_ref, wk_ref, wvt_ref,
                   z_ref, xbc_ref, dt_ref, q_ref, k_ref, vt_ref):
    h = _rms(x_ref[...], g_ref[...]).astype(BF16)
    z_ref[...] = _dot(h, wz_ref[...]).astype(BF16)
    xbc_ref[...] = _dot(h, wxbc_ref[...])
    dt_ref[...] = _dot(h, wdt_ref[...])
    vt_ref[...] = _dot_nt(wvt_ref[...], h).astype(BF16)
    cos = cos_ref[...]
    sin = sin_ref[...]
    lane = lax.broadcasted_iota(jnp.int32, cos.shape, 1)
    first_half = (lane % DIFF_HEAD_DIM) < DIFF_HEAD_DIM // 2
    comp0 = lane < DIFF_HEAD_DIM

    def rope(a, c):
        ac = a[:, c * LANES:(c + 1) * LANES]
        rot = jnp.where(first_half,
                        pltpu.roll(ac, LANES - DIFF_HEAD_DIM // 2, 1),
                        pltpu.roll(ac, DIFF_HEAD_DIM // 2, 1))
        return ac * cos + rot * sin

    a = _dot(h, wk_ref[...])
    for c in range(DIFF_HEADS):
        k_ref[:, c * LANES:(c + 1) * LANES] = rope(a, c).astype(BF16)
    a = _dot(h, wq_ref[...])
    for c in range(DIFF_HEADS):
        qh = rope(a, c) * (DIFF_HEAD_DIM ** -0.5 * LOG2E)
        q_ref[:, (2 * c) * LANES:(2 * c + 1) * LANES] = jnp.where(comp0, qh, 0.0).astype(BF16)
        q_ref[:, (2 * c + 1) * LANES:(2 * c + 2) * LANES] = jnp.where(comp0, 0.0, qh).astype(BF16)


def _inproj(x2, g, cos, sin, wz, wxbc, wdt, wq, wk, wvt, tm):
    T = x2.shape[0]
    row = lambda n: pl.BlockSpec((tm, n), lambda i: (i, 0))
    return pl.pallas_call(
        _inproj_kernel,
        grid=(T // tm,),
        in_specs=[row(D_MODEL), _const_spec((1, D_MODEL)), row(LANES), row(LANES),
                  _const_spec(wz.shape), _const_spec(wxbc.shape), _const_spec(wdt.shape),
                  _const_spec(wq.shape), _const_spec(wk.shape), _const_spec(wvt.shape)],
        out_specs=[row(D_SSD), row(D_XBC), row(LANES), row(2 * D_DIFF), row(D_DIFF),
                   pl.BlockSpec((D_DIFF, tm), lambda i: (0, i))],
        out_shape=[jax.ShapeDtypeStruct((T, D_SSD), BF16),
                   jax.ShapeDtypeStruct((T, D_XBC), F32),
                   jax.ShapeDtypeStruct((T, LANES), F32),
                   jax.ShapeDtypeStruct((T, 2 * D_DIFF), BF16),
                   jax.ShapeDtypeStruct((T, D_DIFF), BF16),
                   jax.ShapeDtypeStruct((D_DIFF, T), BF16)],
        compiler_params=_params("parallel"),
        name="inproj",
    )(x2, g, cos, sin, wz, wxbc, wdt, wq, wk, wvt)


def _pair_expand(cols, j, lane):
    return jnp.where(lane < SSD_HEAD_DIM, cols[:, 2 * j:2 * j + 1], cols[:, 2 * j + 1:2 * j + 2])


def _ssd_kernel(xbc_ref, dt_ref, z_ref, cw_ref, cb_ref, dtb_ref, alog_ref, dskip_ref, nw_ref,
                y_ref, xpad_ref, state_ref):
    c = pl.program_id(1)

    @pl.when(c == 0)
    def _():
        xpad_ref[0:CONV_HALO, :] = jnp.zeros((CONV_HALO, D_XBC), F32)
        state_ref[...] = jnp.zeros_like(state_ref)

    xpad_ref[CONV_HALO:CONV_HALO + CHUNK, :] = xbc_ref[...]
    acc = cb_ref[...] + cw_ref[CONV_WIDTH - 1:CONV_WIDTH, :] * xpad_ref[CONV_HALO:CONV_HALO + CHUNK, :]
    for j in range(CONV_WIDTH - 1):
        off = CONV_HALO - (CONV_WIDTH - 1) + j
        acc = acc + cw_ref[j:j + 1, :] * xpad_ref[off:off + CHUNK, :]
    xpad_ref[0:CONV_HALO, :] = xpad_ref[CHUNK:CHUNK + CONV_HALO, :]
    xc = acc * jax.nn.sigmoid(acc)
    xs = xc[:, :D_SSD]
    gn = SSD_GROUPS * SSD_STATE
    bm = xc[:, D_SSD:D_SSD + gn]
    cm = xc[:, D_SSD + gn:]

    dt = jax.nn.softplus(dt_ref[...] + dtb_ref[...])
    da = dt * (-jnp.exp(alog_ref[...]))
    row = lax.broadcasted_iota(jnp.int32, (CHUNK, LANES), 0)
    lane = lax.broadcasted_iota(jnp.int32, (CHUNK, LANES), 1)
    cum = da
    sh = 1
    while sh < CHUNK:
        cum = cum + jnp.where(row >= sh, pltpu.roll(cum, sh, 0), 0.0)
        sh *= 2
    cum_t = cum.T
    cum_last = cum[CHUNK - 1:CHUNK, :]
    ecum = jnp.exp(cum)
    to_end = jnp.exp(cum_last - cum)
    causal = row >= lane

    npairs = SSD_HEADS // 2
    dt_x = [_pair_expand(dt, j, lane) for j in range(npairs)]
    ecum_x = [_pair_expand(ecum, j, lane) for j in range(npairs)]
    toend_x = [_pair_expand(to_end, j, lane) for j in range(npairs)]
    xdt = [xs[:, j * LANES:(j + 1) * LANES] * dt_x[j] for j in range(npairs)]

    heads_per_group = SSD_HEADS // SSD_GROUPS
    pairs_per_group = heads_per_group // 2
    y_pairs = []
    for g in range(SSD_GROUPS):
        bg = bm[:, g * SSD_STATE:(g + 1) * SSD_STATE]
        cg = cm[:, g * SSD_STATE:(g + 1) * SSD_STATE].astype(BF16)
        bg16 = bg.astype(BF16)
        cbm = _dot_nt(cg, bg16)
        st = state_ref[g]
        y_off = _dot(cg, st.astype(BF16))
        upd = []
        for jp in range(pairs_per_group):
            j = g * pairs_per_group + jp
            yp = jnp.zeros((CHUNK, LANES), F32)
            for half in range(2):
                hd = 2 * j + half
                seg = cum[:, hd:hd + 1] - cum_t[hd:hd + 1, :]
                w = (cbm * jnp.where(causal, jnp.exp(seg), 0.0)).astype(BF16)
                in_half = (lane >= SSD_HEAD_DIM) if half else (lane < SSD_HEAD_DIM)
                yp = yp + _dot(w, jnp.where(in_half, xdt[j], 0.0).astype(BF16))
            y_pairs.append(yp + y_off[:, jp * LANES:(jp + 1) * LANES] * ecum_x[j])
            upd.append((xdt[j] * toend_x[j]).astype(BF16))
        upd = jnp.concatenate(upd, axis=1)
        cdec = jnp.concatenate([ecum_x[g * pairs_per_group + jp][CHUNK - 1:CHUNK, :]
                                for jp in range(pairs_per_group)], axis=1)
        state_ref[g] = st * cdec + _dot(bg.T.astype(BF16), upd)

    y = jnp.concatenate(y_pairs, axis=1) + dskip_ref[...] * xs
    zf = z_ref[...].astype(F32)
    y = y * (zf * jax.nn.sigmoid(zf))
    y_ref[...] = _rms(y, nw_ref[...]).astype(BF16)


def _ssd(xbc, dt, z, cw, cb, dtb, alog, dskip, nw, B, S):
    nc = S // CHUNK
    row = lambda n: pl.BlockSpec((CHUNK, n), lambda b, c: (b * nc + c, 0))
    return pl.pallas_call(
        _ssd_kernel,
        grid=(B, nc),
        in_specs=[row(D_XBC), row(LANES), row(D_SSD),
                  _const_spec(cw.shape), _const_spec(cb.shape), _const_spec(dtb.shape),
                  _const_spec(alog.shape), _const_spec(dskip.shape), _const_spec(nw.shape)],
        out_specs=row(D_SSD),
        out_shape=jax.ShapeDtypeStruct((B * S, D_SSD), BF16),
        scratch_shapes=[pltpu.VMEM((CONV_HALO + CHUNK, D_XBC), F32),
                        pltpu.VMEM((SSD_GROUPS, SSD_STATE, D_SSD // SSD_GROUPS), F32)],
        compiler_params=_params("parallel", "arbitrary"),
        name="ssd",
    )(xbc, dt, z, cw, cb, dtb, alog, dskip, nw)


def _diffattn_kernel(lam_ref, subw_ref, q_ref, k_ref, vt_ref, o_ref, sa_scr, sb_scr, acc_scr, m_scr, *, tq, tk):
    i = pl.program_id(2)
    ones = jnp.ones((ONES_ROWS, tk), BF16)
    krow = lax.broadcasted_iota(jnp.int32, (tk, tq), 0)
    qcol = lax.broadcasted_iota(jnp.int32, (tk, tq), 1)

    def scores(kb_idx, s_ref):
        kb = k_ref[pl.ds(pl.multiple_of(kb_idx * tk, tk), tk), :]
        for c in range(2):
            s_ref[c] = _dot_nt(kb, q_ref[:, c * LANES:(c + 1) * LANES])

    def softmax_pv(kb_idx, s_ref, diag=None):
        vt = vt_ref[:, pl.ds(pl.multiple_of(kb_idx * tk, tk), tk)]
        lhs = jnp.concatenate([vt, ones], axis=0)
        for c in range(2):
            s = s_ref[c]
            if diag is not None:
                s = jnp.where(krow + diag * tk <= qcol, s, -jnp.inf)
            m_old = m_scr[c]
            m_new = jnp.maximum(m_old, jnp.max(s, axis=0, keepdims=True))
            alpha = jnp.exp2(m_old - m_new)
            p = jnp.exp2(s - m_new).astype(BF16)
            acc_scr[c] = alpha * acc_scr[c] + _dot(lhs, p)
            m_scr[c] = m_new

    m_scr[...] = jnp.full(m_scr.shape, -jnp.inf, F32)
    acc_scr[...] = jnp.zeros(acc_scr.shape, F32)
    scores(0, sa_scr)

    def body(t, carry):
        scores(2 * t + 1, sb_scr)
        softmax_pv(2 * t, sa_scr)
        scores(2 * t + 2, sa_scr)
        softmax_pv(2 * t + 1, sb_scr)
        return carry

    lax.fori_loop(0, i, body, 0)
    scores(2 * i + 1, sb_scr)
    softmax_pv(2 * i, sa_scr, diag=0)
    softmax_pv(2 * i + 1, sb_scr, diag=1)

    lv = lam_ref[...]
    lam = (jnp.exp(jnp.sum(lv[0:1] * lv[1:2], axis=-1, keepdims=True))
           - jnp.exp(jnp.sum(lv[2:3] * lv[3:4], axis=-1, keepdims=True)) + LAMBDA_INIT)
    hd = 2 * DIFF_HEAD_DIM
    a0 = acc_scr[0]
    a1 = acc_scr[1]
    ot = a0[:hd] / a0[hd:hd + 1] - lam * (a1[:hd] / a1[hd:hd + 1])
    ms = jnp.mean(ot * ot, axis=0, keepdims=True)
    ot = ot * lax.rsqrt(ms + NORM_EPS) * subw_ref[...] * (1.0 - LAMBDA_INIT)
    o_ref[...] = ot.T.astype(BF16)


def _diffattn(lamv, subw_col, q, k, vt, B, S, tq):
    nq = S // tq
    tk = tq // 2
    hd = 2 * DIFF_HEAD_DIM
    return pl.pallas_call(
        functools.partial(_diffattn_kernel, tq=tq, tk=tk),
        grid=(B, DIFF_HEADS, nq),
        in_specs=[_const_spec(lamv.shape), _const_spec(subw_col.shape),
                  pl.BlockSpec((tq, 2 * hd), lambda b, h, i: (b * nq + i, h)),
                  pl.BlockSpec((S, hd), lambda b, h, i: (b, h)),
                  pl.BlockSpec((hd, S), lambda b, h, i: (h, b))],
        out_specs=pl.BlockSpec((tq, hd), lambda b, h, i: (b * nq + i, h)),
        out_shape=jax.ShapeDtypeStruct((B * S, D_DIFF), BF16),
        scratch_shapes=[pltpu.VMEM((2, tk, tq), F32),
                        pltpu.VMEM((2, tk, tq), F32),
                        pltpu.VMEM((2, hd + ONES_ROWS, tq), F32),
                        pltpu.VMEM((2, 1, tq), F32)],
        compiler_params=_params("parallel", "parallel", "arbitrary"),
        name="diffattn",
    )(lamv, subw_col, q, k, vt)


def _outproj_kernel(ys_ref, yd_ref, x_ref, wo_s_ref, wo_d_ref, gpost_ref, gq_ref, wmq_ref, x1_ref, qm_ref):
    mixed = _dot(ys_ref[...], wo_s_ref[...]) + _dot(yd_ref[...], wo_d_ref[...])
    x1 = x_ref[...] + _rms(mixed, gpost_ref[...])
    x1_ref[...] = x1
    hq = _rms(x1, gq_ref[...]).astype(BF16)
    qm_ref[...] = (_dot(hq, wmq_ref[...]) * (MEM_HEAD_DIM ** -0.5)).astype(BF16)


def _outproj(ys, yd, x2, wo_s, wo_d, gpost, gq, wmq, tm):
    T = x2.shape[0]
    row = lambda n: pl.BlockSpec((tm, n), lambda i: (i, 0))
    return pl.pallas_call(
        _outproj_kernel,
        grid=(T // tm,),
        in_specs=[row(D_SSD), row(D_DIFF), row(D_MODEL), _const_spec(wo_s.shape), _const_spec(wo_d.shape),
                  _const_spec(gpost.shape), _const_spec(gq.shape), _const_spec(wmq.shape)],
        out_specs=[row(D_MODEL), row(D_MODEL)],
        out_shape=[jax.ShapeDtypeStruct((T, D_MODEL), F32), jax.ShapeDtypeStruct((T, D_MODEL), BF16)],
        compiler_params=_params("parallel"),
        name="outproj",
    )(ys, yd, x2, wo_s, wo_d, gpost, gq, wmq)


def _memkv_kernel(mem_ref, g_ref, wk_ref, wv_ref, k_ref, v_ref):
    h = _rms(mem_ref[...], g_ref[...]).astype(BF16)
    k_ref[...] = _dot(h, wk_ref[...]).astype(BF16)
    v_ref[...] = _dot(h, wv_ref[...]).astype(BF16)


def _memkv(mem2, g, wk, wv, M):
    R = mem2.shape[0]
    row = pl.BlockSpec((M, D_MODEL), lambda i: (i, 0))
    return pl.pallas_call(
        _memkv_kernel,
        grid=(R // M,),
        in_specs=[row, _const_spec(g.shape), _const_spec(wk.shape), _const_spec(wv.shape)],
        out_specs=[row, row],
        out_shape=[jax.ShapeDtypeStruct((R, D_MODEL), BF16)] * 2,
        compiler_params=_params("parallel"),
        name="memkv",
    )(mem2, g, wk, wv)


def _xattn_kernel(qm_ref, km_ref, vm_ref, x1_ref, wmo_ref, gpost_ref, x2_ref):
    outs = []
    for hd in range(MEM_HEADS):
        sl = slice(hd * MEM_HEAD_DIM, (hd + 1) * MEM_HEAD_DIM)
        s = _dot_nt(qm_ref[:, sl], km_ref[:, sl])
        e = jnp.exp(s - jnp.max(s, axis=-1, keepdims=True))
        p = e / jnp.sum(e, axis=-1, keepdims=True)
        outs.append(_dot(p.astype(BF16), vm_ref[:, sl]).astype(BF16))
    o = jnp.concatenate(outs, axis=1)
    x2_ref[...] = x1_ref[...] + _rms(_dot(o, wmo_ref[...]), gpost_ref[...])


def _xattn(qm, km, vm, x1, wmo, gpost, S, M, tm):
    T = x1.shape[0]
    per_b = S // tm
    row = lambda n: pl.BlockSpec((tm, n), lambda i: (i, 0))
    mem = pl.BlockSpec((M, D_MODEL), lambda i: (i // per_b, 0))
    return pl.pallas_call(
        _xattn_kernel,
        grid=(T // tm,),
        in_specs=[row(D_MODEL), mem, mem, row(D_MODEL), _const_spec(wmo.shape), _const_spec(gpost.shape)],
        out_specs=row(D_MODEL),
        out_shape=jax.ShapeDtypeStruct((T, D_MODEL), F32),
        compiler_params=_params("parallel"),
        name="xattn",
    )(qm, km, vm, x1, wmo, gpost)


def _mlp_kernel(x_ref, gpre_ref, wup_ref, wdn_ref, gpost_ref, o_ref, *, tf):
    x = x_ref[...]
    h = _rms(x, gpre_ref[...]).astype(BF16)
    acc = jnp.zeros(x.shape, F32)
    for c in range(D_FF // tf):
        u = jnp.maximum(_dot(h, wup_ref[:, c * tf:(c + 1) * tf]), 0.0)
        acc = acc + _dot((u * u).astype(BF16), wdn_ref[c * tf:(c + 1) * tf, :])
    o_ref[...] = x + _rms(acc, gpost_ref[...])


def _mlp(x2, gpre, wup, wdn, gpost, tm, tf):
    T = x2.shape[0]
    row = pl.BlockSpec((tm, D_MODEL), lambda i: (i, 0))
    return pl.pallas_call(
        functools.partial(_mlp_kernel, tf=tf),
        grid=(T // tm,),
        in_specs=[row, _const_spec(gpre.shape), _const_spec(wup.shape), _const_spec(wdn.shape),
                  _const_spec(gpost.shape)],
        out_specs=row,
        out_shape=jax.ShapeDtypeStruct((T, D_MODEL), F32),
        compiler_params=_params("parallel"),
        name="mlp",
    )(x2, gpre, wup, wdn, gpost)


def _rope_tables(positions):
    half = DIFF_HEAD_DIM // 2
    inv = ROPE_THETA ** (-jnp.arange(0, DIFF_HEAD_DIM, 2, dtype=F32) / DIFF_HEAD_DIM)
    ang = positions.astype(F32).reshape(-1, 1) * inv
    ang = jnp.concatenate([ang] * (LANES // half), axis=-1)
    sign = jnp.where((jnp.arange(LANES) % DIFF_HEAD_DIM) < half, -1.0, 1.0).astype(F32)
    return jnp.cos(ang), jnp.sin(ang) * sign


def kernel(x, mem, positions, norm_mix_pre, norm_mix_post, norm_mem_q, norm_mem_kv, norm_mem_post,
           norm_mlp_pre, norm_mlp_post, w_in, conv_w, conv_b, dt_bias, a_log, d_skip, ssd_norm_w,
           lambda_q1, lambda_k1, lambda_q2, lambda_k2, subln_w, w_out, w_mq, w_mk, w_mv, w_mo, w_up, w_down):
    B, S, _ = x.shape
    M = mem.shape[1]
    T = B * S
    assert norm_mix_pre.shape[0] == 1, "single-layer trunk"
    x2 = x.reshape(T, D_MODEL)
    cos, sin = _rope_tables(positions)

    w = w_in[0]
    o0 = D_SSD
    o1 = o0 + D_XBC
    o2 = o1 + SSD_HEADS
    o3 = o2 + D_DIFF
    o4 = o3 + D_DIFF
    wz = w[:, :o0].astype(BF16)
    wxbc = w[:, o0:o1].astype(BF16)
    wdt = jnp.pad(w[:, o1:o2], ((0, 0), (0, LANES - SSD_HEADS))).astype(BF16)
    wq = w[:, o2:o3].astype(BF16)
    wk = w[:, o3:o4].astype(BF16)
    wvt = w[:, o4:].T.astype(BF16)

    z, xbc, dt, q, k, vt = _inproj(x2, norm_mix_pre, cos, sin, wz, wxbc, wdt, wq, wk, wvt, tm=512)

    pad_heads = lambda p: jnp.pad(p, ((0, 0), (0, LANES - SSD_HEADS)))
    y_ssd = _ssd(xbc, dt, z, conv_w[0, :, 0, :], conv_b, pad_heads(dt_bias), pad_heads(a_log),
                 jnp.repeat(d_skip, SSD_HEAD_DIM, axis=1), ssd_norm_w, B, S)

    lamv = jnp.concatenate([lambda_q1, lambda_k1, lambda_q2, lambda_k2], axis=0)
    y_diff = _diffattn(lamv, subln_w.reshape(2 * DIFF_HEAD_DIM, 1), q, k, vt, B, S, tq=512)

    wo = w_out[0].astype(BF16)
    x1, qm = _outproj(y_ssd, y_diff, x2, wo[:D_SSD], wo[D_SSD:], norm_mix_post, norm_mem_q,
                      w_mq[0].astype(BF16), tm=512)
    km, vm = _memkv(mem.reshape(B * M, D_MODEL), norm_mem_kv, w_mk[0].astype(BF16), w_mv[0].astype(BF16), M)
    x2b = _xattn(qm, km, vm, x1, w_mo[0].astype(BF16), norm_mem_post, S, M, tm=512)
    out = _mlp(x2b, norm_mlp_pre, w_up[0].astype(BF16), w_down[0].astype(BF16), norm_mlp_post, tm=512, tf=1024)
    return out.reshape(B, S, D_MODEL)
```

```python
import functools
import math

import jax
import jax.numpy as jnp
from jax import lax
from jax.experimental import pallas as pl
from jax.experimental.pallas import tpu as pltpu

F32 = jnp.float32
BF16 = jnp.bfloat16

D_MODEL = 1024
D_SSD = 512
SSD_HEAD_DIM = 64
SSD_HEADS = D_SSD // SSD_HEAD_DIM
SSD_GROUPS = 2
SSD_STATE = 128
CONV_WIDTH = 4
CHUNK = 128
D_XBC = D_SSD + 2 * SSD_GROUPS * SSD_STATE
D_DIFF = D_MODEL - D_SSD
DIFF_HEAD_DIM = 64
DIFF_HEADS = D_DIFF // (2 * DIFF_HEAD_DIM)
ROPE_THETA = 10000.0
MEM_HEADS = 4
MEM_HEAD_DIM = D_MODEL // MEM_HEADS
D_FF = 4 * D_MODEL
NORM_EPS = 1e-6
LAMBDA_INIT = 0.8 - 0.6 * math.exp(-0.3 * 0)

LOG2E = math.log2(math.e)
LANES = 128
ONES_ROWS = 16
CONV_HALO = 8
VMEM_LIMIT = 56 * 1024 * 1024


def _dot(a, b):
    return jnp.dot(a, b, preferred_element_type=F32)


def _dot_nt(a, b):
    return lax.dot_general(a, b, (((1,), (1,)), ((), ())), preferred_element_type=F32)


def _rms(x, g):
    ms = jnp.mean(x * x, axis=-1, keepdims=True)
    return x * lax.rsqrt(ms + NORM_EPS) * g


def _const_spec(shape):
    nd = len(shape)
    return pl.BlockSpec(shape, lambda *_: (0,) * nd)


def _params(*sem):
    return pltpu.CompilerParams(dimension_semantics=sem, vmem_limit_bytes=VMEM_LIMIT)


def _inproj_kernel(x_ref, g_ref, cos_ref, sin_ref, wz_ref, wxbc_ref, wdt_ref, wq_ref, wk_ref, wvt_ref,
                   z_ref, xbc_ref, dt_ref, q_ref, k_ref, vt_ref):
    h = _rms(x_ref[...], g_ref[...]).astype(BF16)
    z_ref[...] = _dot(h, wz_ref[...]).astype(BF16)
    xbc_ref[...] = _dot(h, wxbc_ref[...])
    dt_ref[...] = _dot(h, wdt_ref[...])
    vt_ref[...] = _dot_nt(wvt_ref[...], h).astype(BF16)
    cos = cos_ref[...]
    sin = sin_ref[...]
    lane = lax.broadcasted_iota(jnp.int32, cos.shape, 1)
    first_half = (lane % DIFF_HEAD_DIM) < DIFF_HEAD_DIM // 2
    comp0 = lane < DIFF_HEAD_DIM

    def rope(a, c):
        ac = a[:, c * LANES:(c + 1) * LANES]
        rot = jnp.where(first_half,
                        pltpu.roll(ac, LANES - DIFF_HEAD_DIM // 2, 1),
                        pltpu.roll(ac, DIFF_HEAD_DIM // 2, 1))
        return ac * cos + rot * sin

    a = _dot(h, wk_ref[...])
    for c in range(DIFF_HEADS):
        k_ref[:, c * LANES:(c + 1) * LANES] = rope(a, c).astype(BF16)
    a = _dot(h, wq_ref[...])
    for c in range(DIFF_HEADS):
        qh = rope(a, c) * (DIFF_HEAD_DIM ** -0.5 * LOG2E)
        q_ref[:, (2 * c) * LANES:(2 * c + 1) * LANES] = jnp.where(comp0, qh, 0.0).astype(BF16)
        q_ref[:, (2 * c + 1) * LANES:(2 * c + 2) * LANES] = jnp.where(comp0, 0.0, qh).astype(BF16)


def _inproj(x2, g, cos, sin, wz, wxbc, wdt, wq, wk, wvt, tm):
    T = x2.shape[0]
    row = lambda n: pl.BlockSpec((tm, n), lambda i: (i, 0))
    return pl.pallas_call(
        _inproj_kernel,
        grid=(T // tm,),
        in_specs=[row(D_MODEL), _const_spec((1, D_MODEL)), row(LANES), row(LANES),
                  _const_spec(wz.shape), _const_spec(wxbc.shape), _const_spec(wdt.shape),
                  _const_spec(wq.shape), _const_spec(wk.shape), _const_spec(wvt.shape)],
        out_specs=[row(D_SSD), row(D_XBC), row(LANES), row(2 * D_DIFF), row(D_DIFF),
                   pl.BlockSpec((D_DIFF, tm), lambda i: (0, i))],
        out_shape=[jax.ShapeDtypeStruct((T, D_SSD), BF16),
                   jax.ShapeDtypeStruct((T, D_XBC), F32),
                   jax.ShapeDtypeStruct((T, LANES), F32),
                   jax.ShapeDtypeStruct((T, 2 * D_DIFF), BF16),
                   jax.ShapeDtypeStruct((T, D_DIFF), BF16),
                   jax.ShapeDtypeStruct((D_DIFF, T), BF16)],
        compiler_params=_params("parallel"),
        name="inproj",
    )(x2, g, cos, sin, wz, wxbc, wdt, wq, wk, wvt)


def _pair_expand(cols, j, lane):
    return jnp.where(lane < SSD_HEAD_DIM, cols[:, 2 * j:2 * j + 1], cols[:, 2 * j + 1:2 * j + 2])


def _ssd_kernel(xbc_ref, dt_ref, z_ref, cw_ref, cb_ref, dtb_ref, alog_ref, dskip_ref, nw_ref,
                y_ref, xpad_ref, state_ref):
    c = pl.program_id(1)

    @pl.when(c == 0)
    def _():
        xpad_ref[0:CONV_HALO, :] = jnp.zeros((CONV_HALO, D_XBC), F32)
        state_ref[...] = jnp.zeros_like(state_ref)

    xpad_ref[CONV_HALO:CONV_HALO + CHUNK, :] = xbc_ref[...]
    acc = cb_ref[...] + cw_ref[CONV_WIDTH - 1:CONV_WIDTH, :] * xpad_ref[CONV_HALO:CONV_HALO + CHUNK, :]
    for j in range(CONV_WIDTH - 1):
        off = CONV_HALO - (CONV_WIDTH - 1) + j
        acc = acc + cw_ref[j:j + 1, :] * xpad_ref[off:off + CHUNK, :]
    xpad_ref[0:CONV_HALO, :] = xpad_ref[CHUNK:CHUNK + CONV_HALO, :]
    xc = acc * jax.nn.sigmoid(acc)
    xs = xc[:, :D_SSD]
    gn = SSD_GROUPS * SSD_STATE
    bm = xc[:, D_SSD:D_SSD + gn]
    cm = xc[:, D_SSD + gn:]

    dt = jax.nn.softplus(dt_ref[...] + dtb_ref[...])
    da = dt * (-jnp.exp(alog_ref[...]))
    row = lax.broadcasted_iota(jnp.int32, (CHUNK, LANES), 0)
    lane = lax.broadcasted_iota(jnp.int32, (CHUNK, LANES), 1)
    cum = da
    sh = 1
    while sh < CHUNK:
        cum = cum + jnp.where(row >= sh, pltpu.roll(cum, sh, 0), 0.0)
        sh *= 2
    cum_t = cum.T
    cum_last = cum[CHUNK - 1:CHUNK, :]
    ecum = jnp.exp(cum)
    to_end = jnp.exp(cum_last - cum)
    causal = row >= lane

    npairs = SSD_HEADS // 2
    dt_x = [_pair_expand(dt, j, lane) for j in range(npairs)]
    ecum_x = [_pair_expand(ecum, j, lane) for j in range(npairs)]
    toend_x = [_pair_expand(to_end, j, lane) for j in range(npairs)]
    xdt = [xs[:, j * LANES:(j + 1) * LANES] * dt_x[j] for j in range(npairs)]

    heads_per_group = SSD_HEADS // SSD_GROUPS
    pairs_per_group = heads_per_group // 2
    y_pairs = []
    for g in range(SSD_GROUPS):
        bg = bm[:, g * SSD_STATE:(g + 1) * SSD_STATE]
        cg = cm[:, g * SSD_STATE:(g + 1) * SSD_STATE].astype(BF16)
        bg16 = bg.astype(BF16)
        cbm = _dot_nt(cg, bg16)
        st = state_ref[g]
        y_off = _dot(cg, st.astype(BF16))
        upd = []
        for jp in range(pairs_per_group):
            j = g * pairs_per_group + jp
            yp = jnp.zeros((CHUNK, LANES), F32)
            for half in range(2):
                hd = 2 * j + half
                seg = cum[:, hd:hd + 1] - cum_t[hd:hd + 1, :]
                w = (cbm * jnp.where(causal, jnp.exp(seg), 0.0)).astype(BF16)
                in_half = (lane >= SSD_HEAD_DIM) if half else (lane < SSD_HEAD_DIM)
                yp = yp + _dot(w, jnp.where(in_half, xdt[j], 0.0).astype(BF16))
            y_pairs.append(yp + y_off[:, jp * LANES:(jp + 1) * LANES] * ecum_x[j])
            upd.append((xdt[j] * toend_x[j]).astype(BF16))
        upd = jnp.concatenate(upd, axis=1)
        cdec = jnp.concatenate([ecum_x[g * pairs_per_group + jp][CHUNK - 1:CHUNK, :]
                                for jp in range(pairs_per_group)], axis=1)
        state_ref[g] = st * cdec + _dot(bg.T.astype(BF16), upd)

    y = jnp.concatenate(y_pairs, axis=1) + dskip_ref[...] * xs
    zf = z_ref[...].astype(F32)
    y = y * (zf * jax.nn.sigmoid(zf))
    y_ref[...] = _rms(y, nw_ref[...]).astype(BF16)


def _ssd(xbc, dt, z, cw, cb, dtb, alog, dskip, nw, B, S):
    nc = S // CHUNK
    row = lambda n: pl.BlockSpec((CHUNK, n), lambda b, c: (b * nc + c, 0))
    return pl.pallas_call(
        _ssd_kernel,
        grid=(B, nc),
        in_specs=[row(D_XBC), row(LANES), row(D_SSD),
                  _const_spec(cw.shape), _const_spec(cb.shape), _const_spec(dtb.shape),
                  _const_spec(alog.shape), _const_spec(dskip.shape), _const_spec(nw.shape)],
        out_specs=row(D_SSD),
        out_shape=jax.ShapeDtypeStruct((B * S, D_SSD), BF16),
        scratch_shapes=[pltpu.VMEM((CONV_HALO + CHUNK, D_XBC), F32),
                        pltpu.VMEM((SSD_GROUPS, SSD_STATE, D_SSD // SSD_GROUPS), F32)],
        compiler_params=_params("parallel", "arbitrary"),
        name="ssd",
    )(xbc, dt, z, cw, cb, dtb, alog, dskip, nw)


def _diffattn_kernel(lam_ref, subw_ref, q_ref, k_ref, vt_ref, o_ref, sa_scr, sb_scr, acc_scr, m_scr, *, tq, tk, nq):
    hd = 2 * DIFF_HEAD_DIM
    ones = jnp.ones((ONES_ROWS, tk), BF16)
    krow = lax.broadcasted_iota(jnp.int32, (tk, tk), 0)
    qcol = lax.broadcasted_iota(jnp.int32, (tk, tk), 1)
    tri = krow <= qcol
    lv = lam_ref[...]
    lam = (jnp.exp(jnp.sum(lv[0:1] * lv[1:2], axis=-1, keepdims=True))
           - jnp.exp(jnp.sum(lv[2:3] * lv[3:4], axis=-1, keepdims=True)) + LAMBDA_INIT)

    steps = []
    for i in range(nq):
        steps += [(i, kb, "full") for kb in range(2 * i)] + [(i, 2 * i, "diag0"), (i, 2 * i + 1, "diag1")]

    def query_cols(kind):
        return (tk, tq) if kind == "diag1" else (0, tq)

    def scores(step, s_ref):
        i, kb, kind = step
        lo, hi = query_cols(kind)
        kblk = k_ref[kb * tk:(kb + 1) * tk, :]
        for c in range(2):
            s_ref[c, :, lo:hi] = _dot_nt(kblk, q_ref[i * tq + lo:i * tq + hi, c * LANES:(c + 1) * LANES])

    def softmax_pv(step, s_ref):
        i, kb, kind = step
        lo, hi = query_cols(kind)
        par = i % 2
        lhs = jnp.concatenate([vt_ref[:, kb * tk:(kb + 1) * tk], ones], axis=0)
        for c in range(2):
            s = s_ref[c, :, lo:hi]
            if kind == "diag0":
                s = jnp.concatenate([jnp.where(tri, s[:, :tk], -jnp.inf), s[:, tk:]], axis=1)
            elif kind == "diag1":
                s = jnp.where(tri, s, -jnp.inf)
            m_blk = jnp.max(s, axis=0, keepdims=True)
            if kb == 0:
                m_new = m_blk
                acc_scr[par, c, :, lo:hi] = _dot(lhs, jnp.exp2(s - m_new).astype(BF16))
            else:
                m_old = m_scr[par, c, :, lo:hi]
                m_new = jnp.maximum(m_old, m_blk)
                alpha = jnp.exp2(m_old - m_new)
                p = jnp.exp2(s - m_new).astype(BF16)
                acc_scr[par, c, :, lo:hi] = alpha * acc_scr[par, c, :, lo:hi] + _dot(lhs, p)
            m_scr[par, c, :, lo:hi] = m_new

    def finish(i):
        par = i % 2
        a0 = acc_scr[par, 0]
        a1 = acc_scr[par, 1]
        ot = a0[:hd] * (1.0 / a0[hd:hd + 1]) - a1[:hd] * (lam / a1[hd:hd + 1])
        ms = jnp.mean(ot * ot, axis=0, keepdims=True)
        ot = ot * (lax.rsqrt(ms + NORM_EPS) * (1.0 - LAMBDA_INIT)) * subw_ref[...]
        o_ref[i * tq:(i + 1) * tq, :] = ot.T.astype(BF16)

    bufs = (sa_scr, sb_scr)
    scores(steps[0], bufs[0])
    for n, step in enumerate(steps):
        if n + 1 < len(steps):
            scores(steps[n + 1], bufs[(n + 1) % 2])
        softmax_pv(step, bufs[n % 2])
        if step[2] == "diag1":
            finish(step[0])


def _diffattn(lamv, subw_col, q, k, vt, B, S, tq):
    nq = S // tq
    tk = tq // 2
    hd = 2 * DIFF_HEAD_DIM
    return pl.pallas_call(
        functools.partial(_diffattn_kernel, tq=tq, tk=tk, nq=nq),
        grid=(B, DIFF_HEADS),
        in_specs=[_const_spec(lamv.shape), _const_spec(subw_col.shape),
                  pl.BlockSpec((S, 2 * hd), lambda b, h: (b, h)),
                  pl.BlockSpec((S, hd), lambda b, h: (b, h)),
                  pl.BlockSpec((hd, S), lambda b, h: (h, b))],
        out_specs=pl.BlockSpec((S, hd), lambda b, h: (b, h)),
        out_shape=jax.ShapeDtypeStruct((B * S, D_DIFF), BF16),
        scratch_shapes=[pltpu.VMEM((2, tk, tq), F32),
                        pltpu.VMEM((2, tk, tq), F32),
                        pltpu.VMEM((2, 2, hd + ONES_ROWS, tq), F32),
                        pltpu.VMEM((2, 2, 1, tq), F32)],
        compiler_params=_params("parallel", "parallel"),
        name="diffattn",
    )(lamv, subw_col, q, k, vt)


def _outproj_kernel(ys_ref, yd_ref, x_ref, wo_s_ref, wo_d_ref, gpost_ref, gq_ref, wmq_ref, x1_ref, qm_ref):
    mixed = _dot(ys_ref[...], wo_s_ref[...]) + _dot(yd_ref[...], wo_d_ref[...])
    x1 = x_ref[...] + _rms(mixed, gpost_ref[...])
    x1_ref[...] = x1
    hq = _rms(x1, gq_ref[...]).astype(BF16)
    qm_ref[...] = (_dot(hq, wmq_ref[...]) * (MEM_HEAD_DIM ** -0.5)).astype(BF16)


def _outproj(ys, yd, x2, wo_s, wo_d, gpost, gq, wmq, tm):
    T = x2.shape[0]
    row = lambda n: pl.BlockSpec((tm, n), lambda i: (i, 0))
    return pl.pallas_call(
        _outproj_kernel,
        grid=(T // tm,),
        in_specs=[row(D_SSD), row(D_DIFF), row(D_MODEL), _const_spec(wo_s.shape), _const_spec(wo_d.shape),
                  _const_spec(gpost.shape), _const_spec(gq.shape), _const_spec(wmq.shape)],
        out_specs=[row(D_MODEL), row(D_MODEL)],
        out_shape=[jax.ShapeDtypeStruct((T, D_MODEL), F32), jax.ShapeDtypeStruct((T, D_MODEL), BF16)],
        compiler_params=_params("parallel"),
        name="outproj",
    )(ys, yd, x2, wo_s, wo_d, gpost, gq, wmq)


def _memkv_kernel(mem_ref, g_ref, wk_ref, wv_ref, k_ref, v_ref):
    h = _rms(mem_ref[...], g_ref[...]).astype(BF16)
    k_ref[...] = _dot(h, wk_ref[...]).astype(BF16)
    v_ref[...] = _dot(h, wv_ref[...]).astype(BF16)


def _memkv(mem2, g, wk, wv, M):
    R = mem2.shape[0]
    row = pl.BlockSpec((M, D_MODEL), lambda i: (i, 0))
    return pl.pallas_call(
        _memkv_kernel,
        grid=(R // M,),
        in_specs=[row, _const_spec(g.shape), _const_spec(wk.shape), _const_spec(wv.shape)],
        out_specs=[row, row],
        out_shape=[jax.ShapeDtypeStruct((R, D_MODEL), BF16)] * 2,
        compiler_params=_params("parallel"),
        name="memkv",
    )(mem2, g, wk, wv)


def _xattn_kernel(qm_ref, km_ref, vm_ref, x1_ref, wmo_ref, gpost_ref, x2_ref):
    outs = []
    for hd in range(MEM_HEADS):
        sl = slice(hd * MEM_HEAD_DIM, (hd + 1) * MEM_HEAD_DIM)
        s = _dot_nt(qm_ref[:, sl], km_ref[:, sl])
        e = jnp.exp(s - jnp.max(s, axis=-1, keepdims=True))
        p = e / jnp.sum(e, axis=-1, keepdims=True)
        outs.append(_dot(p.astype(BF16), vm_ref[:, sl]).astype(BF16))
    o = jnp.concatenate(outs, axis=1)
    x2_ref[...] = x1_ref[...] + _rms(_dot(o, wmo_ref[...]), gpost_ref[...])


def _xattn(qm, km, vm, x1, wmo, gpost, S, M, tm):
    T = x1.shape[0]
    per_b = S // tm
    row = lambda n: pl.BlockSpec((tm, n), lambda i: (i, 0))
    mem = pl.BlockSpec((M, D_MODEL), lambda i: (i // per_b, 0))
    return pl.pallas_call(
        _xattn_kernel,
        grid=(T // tm,),
        in_specs=[row(D_MODEL), mem, mem, row(D_MODEL), _const_spec(wmo.shape), _const_spec(gpost.shape)],
        out_specs=row(D_MODEL),
        out_shape=jax.ShapeDtypeStruct((T, D_MODEL), F32),
        compiler_params=_params("parallel"),
        name="xattn",
    )(qm, km, vm, x1, wmo, gpost)


def _mlp_kernel(x_ref, gpre_ref, wup_ref, wdn_ref, gpost_ref, o_ref, *, tf):
    x = x_ref[...]
    h = _rms(x, gpre_ref[...]).astype(BF16)
    acc = jnp.zeros(x.shape, F32)
    for c in range(D_FF // tf):
        u = jnp.maximum(_dot(h, wup_ref[:, c * tf:(c + 1) * tf]), 0.0)
        acc = acc + _dot((u * u).astype(BF16), wdn_ref[c * tf:(c + 1) * tf, :])
    o_ref[...] = x + _rms(acc, gpost_ref[...])


def _mlp(x2, gpre, wup, wdn, gpost, tm, tf):
    T = x2.shape[0]
    row = pl.BlockSpec((tm, D_MODEL), lambda i: (i, 0))
    return pl.pallas_call(
        functools.partial(_mlp_kernel, tf=tf),
        grid=(T // tm,),
        in_specs=[row, _const_spec(gpre.shape), _const_spec(wup.shape), _const_spec(wdn.shape),
                  _const_spec(gpost.shape)],
        out_specs=row,
        out_shape=jax.ShapeDtypeStruct((T, D_MODEL), F32),
        compiler_params=_params("parallel"),
        name="mlp",
    )(x2, gpre, wup, wdn, gpost)


def _rope_tables(positions):
    half = DIFF_HEAD_DIM // 2
    inv = ROPE_THETA ** (-jnp.arange(0, DIFF_HEAD_DIM, 2, dtype=F32) / DIFF_HEAD_DIM)
    ang = positions.astype(F32).reshape(-1, 1) * inv
    ang = jnp.concatenate([ang] * (LANES // half), axis=-1)
    sign = jnp.where((jnp.arange(LANES) % DIFF_HEAD_DIM) < half, -1.0, 1.0).astype(F32)
    return jnp.cos(ang), jnp.sin(ang) * sign


def kernel(x, mem, positions, norm_mix_pre, norm_mix_post, norm_mem_q, norm_mem_kv, norm_mem_post,
           norm_mlp_pre, norm_mlp_post, w_in, conv_w, conv_b, dt_bias, a_log, d_skip, ssd_norm_w,
           lambda_q1, lambda_k1, lambda_q2, lambda_k2, subln_w, w_out, w_mq, w_mk, w_mv, w_mo, w_up, w_down):
    B, S, _ = x.shape
    M = mem.shape[1]
    T = B * S
    assert norm_mix_pre.shape[0] == 1, "single-layer trunk"
    x2 = x.reshape(T, D_MODEL)
    cos, sin = _rope_tables(positions)

    w = w_in[0]
    o0 = D_SSD
    o1 = o0 + D_XBC
    o2 = o1 + SSD_HEADS
    o3 = o2 + D_DIFF
    o4 = o3 + D_DIFF
    wz = w[:, :o0].astype(BF16)
    wxbc = w[:, o0:o1].astype(BF16)
    wdt = jnp.pad(w[:, o1:o2], ((0, 0), (0, LANES - SSD_HEADS))).astype(BF16)
    wq = w[:, o2:o3].astype(BF16)
    wk = w[:, o3:o4].astype(BF16)
    wvt = w[:, o4:].T.astype(BF16)

    z, xbc, dt, q, k, vt = _inproj(x2, norm_mix_pre, cos, sin, wz, wxbc, wdt, wq, wk, wvt, tm=512)

    pad_heads = lambda p: jnp.pad(p, ((0, 0), (0, LANES - SSD_HEADS)))
    y_ssd = _ssd(xbc, dt, z, conv_w[0, :, 0, :], conv_b, pad_heads(dt_bias), pad_heads(a_log),
                 jnp.repeat(d_skip, SSD_HEAD_DIM, axis=1), ssd_norm_w, B, S)

    lamv = jnp.concatenate([lambda_q1, lambda_k1, lambda_q2, lambda_k2], axis=0)
    y_diff = _diffattn(lamv, subln_w.reshape(2 * DIFF_HEAD_DIM, 1), q, k, vt, B, S, tq=512)

    wo = w_out[0].astype(BF16)
    x1, qm = _outproj(y_ssd, y_diff, x2, wo[:D_SSD], wo[D_SSD:], norm_mix_post, norm_mem_q,
                      w_mq[0].astype(BF16), tm=512)
    km, vm = _memkv(mem.reshape(B * M, D_MODEL), norm_mem_kv, w_mk[0].astype(BF16), w_mv[0].astype(BF16), M)
    x2b = _xattn(qm, km, vm, x1, w_mo[0].astype(BF16), norm_mem_post, S, M, tm=512)
    out = _mlp(x2b, norm_mlp_pre, w_up[0].astype(BF16), w_down[0].astype(BF16), norm_mlp_post, tm=512, tf=1024)
    return out.reshape(B, S, D_MODEL)
```

```python
import functools
import itertools
import math

import jax
import jax.numpy as jnp
from jax import lax
from jax.experimental import pallas as pl
from jax.experimental.pallas import tpu as pltpu

F32 = jnp.float32
BF16 = jnp.bfloat16

D_MODEL = 1024
D_SSD = 512
SSD_HEAD_DIM = 64
SSD_HEADS = D_SSD // SSD_HEAD_DIM
SSD_GROUPS = 2
SSD_STATE = 128
CONV_WIDTH = 4
CHUNK = 128
D_XBC = D_SSD + 2 * SSD_GROUPS * SSD_STATE
D_DIFF = D_MODEL - D_SSD
DIFF_HEAD_DIM = 64
DIFF_HEADS = D_DIFF // (2 * DIFF_HEAD_DIM)
ROPE_THETA = 10000.0
MEM_HEADS = 4
MEM_HEAD_DIM = D_MODEL // MEM_HEADS
D_FF = 4 * D_MODEL
NORM_EPS = 1e-6
LAMBDA_INIT = 0.8 - 0.6 * math.exp(-0.3 * 0)

LOG2E = math.log2(math.e)
LANES = 128
ONES_ROWS = 16
CONV_HALO = 8
SSD_SEQS_PER_STEP = 4
DT_ROWS = 16
VMEM_LIMIT = 56 * 1024 * 1024


def _dot(a, b):
    return jnp.dot(a, b, preferred_element_type=F32)


def _dot_nt(a, b):
    return lax.dot_general(a, b, (((1,), (1,)), ((), ())), preferred_element_type=F32)


def _rms(x, g):
    ms = jnp.mean(x * x, axis=-1, keepdims=True)
    return x * lax.rsqrt(ms + NORM_EPS) * g


def _const_spec(shape):
    nd = len(shape)
    return pl.BlockSpec(shape, lambda *_: (0,) * nd)


def _params(*sem):
    return pltpu.CompilerParams(dimension_semantics=sem, vmem_limit_bytes=VMEM_LIMIT)


def _inproj_kernel(x_ref, g_ref, cos_ref, sin_ref, wz_ref, wxbc_ref, wq_ref, wk_ref, wvdt_ref,
                   cw_ref, cb_ref, dtb_ref,
                   zg_ref, xc_ref, dtt_ref, q_ref, k_ref, vt_ref, xpad_ref, *, tm, tiles_per_seq):
    h = _rms(x_ref[...], g_ref[...]).astype(BF16)
    z = _dot(h, wz_ref[...])
    zg_ref[...] = (z * jax.nn.sigmoid(z)).astype(BF16)

    @pl.when(pl.program_id(0) % tiles_per_seq == 0)
    def _():
        xpad_ref[0:CONV_HALO, :] = jnp.zeros((CONV_HALO, D_XBC), F32)

    xbc = _dot(h, wxbc_ref[...])
    xpad_ref[CONV_HALO:CONV_HALO + tm, :] = xbc
    acc = cb_ref[...] + cw_ref[CONV_WIDTH - 1:CONV_WIDTH, :] * xbc
    for j in range(CONV_WIDTH - 1):
        off = CONV_HALO - (CONV_WIDTH - 1) + j
        acc = acc + cw_ref[j:j + 1, :] * xpad_ref[off:off + tm, :]
    xpad_ref[0:CONV_HALO, :] = xpad_ref[tm:tm + CONV_HALO, :]
    xc_ref[...] = (acc * jax.nn.sigmoid(acc)).astype(BF16)

    r = _dot_nt(wvdt_ref[...], h)
    vt_ref[...] = r[:D_DIFF].astype(BF16)
    dtt_ref[...] = jax.nn.softplus(r[D_DIFF:] + dtb_ref[...])
    cos = cos_ref[...]
    sin = sin_ref[...]
    lane = lax.broadcasted_iota(jnp.int32, cos.shape, 1)
    first_half = (lane % DIFF_HEAD_DIM) < DIFF_HEAD_DIM // 2
    comp0 = lane < DIFF_HEAD_DIM

    def rope(a, c):
        ac = a[:, c * LANES:(c + 1) * LANES]
        rot = jnp.where(first_half,
                        pltpu.roll(ac, LANES - DIFF_HEAD_DIM // 2, 1),
                        pltpu.roll(ac, DIFF_HEAD_DIM // 2, 1))
        return ac * cos + rot * sin

    a = _dot(h, wk_ref[...])
    for c in range(DIFF_HEADS):
        k_ref[:, c * LANES:(c + 1) * LANES] = rope(a, c).astype(BF16)
    a = _dot(h, wq_ref[...])
    for c in range(DIFF_HEADS):
        qh = rope(a, c) * (DIFF_HEAD_DIM ** -0.5 * LOG2E)
        q_ref[:, (2 * c) * LANES:(2 * c + 1) * LANES] = jnp.where(comp0, qh, 0.0).astype(BF16)
        q_ref[:, (2 * c + 1) * LANES:(2 * c + 2) * LANES] = jnp.where(comp0, 0.0, qh).astype(BF16)


def _inproj(x2, g, cos, sin, wz, wxbc, wq, wk, wvdt, cw, cb, dtb_col, S, tm):
    T = x2.shape[0]
    row = lambda n: pl.BlockSpec((tm, n), lambda i: (i, 0))
    col = lambda n: pl.BlockSpec((n, tm), lambda i: (0, i))
    return pl.pallas_call(
        functools.partial(_inproj_kernel, tm=tm, tiles_per_seq=S // tm),
        grid=(T // tm,),
        in_specs=[row(D_MODEL), _const_spec((1, D_MODEL)), row(LANES), row(LANES),
                  _const_spec(wz.shape), _const_spec(wxbc.shape),
                  _const_spec(wq.shape), _const_spec(wk.shape), _const_spec(wvdt.shape),
                  _const_spec(cw.shape), _const_spec(cb.shape), _const_spec(dtb_col.shape)],
        out_specs=[row(D_SSD), row(D_XBC), col(DT_ROWS), row(2 * D_DIFF), row(D_DIFF), col(D_DIFF)],
        out_shape=[jax.ShapeDtypeStruct((T, D_SSD), BF16),
                   jax.ShapeDtypeStruct((T, D_XBC), BF16),
                   jax.ShapeDtypeStruct((DT_ROWS, T), F32),
                   jax.ShapeDtypeStruct((T, 2 * D_DIFF), BF16),
                   jax.ShapeDtypeStruct((T, D_DIFF), BF16),
                   jax.ShapeDtypeStruct((D_DIFF, T), BF16)],
        scratch_shapes=[pltpu.VMEM((CONV_HALO + tm, D_XBC), F32)],
        compiler_params=_params("arbitrary"),
        name="inproj",
    )(x2, g, cos, sin, wz, wxbc, wq, wk, wvdt, cw, cb, dtb_col)


def _ssd_chunk(xc_ref, zg_ref, dt_t, alog_ref, dskip_ref, nw_ref, y_ref, state_ref, lane, causal, tri_t):
    da_t = dt_t * (-LOG2E * jnp.exp(alog_ref[...]))
    hi = da_t.astype(BF16)
    rem = da_t - hi.astype(F32)
    mid = rem.astype(BF16)
    lo = (rem - mid.astype(F32)).astype(BF16)
    cum_t = _dot(hi, tri_t) + _dot(mid, tri_t) + _dot(lo, tri_t)
    yield
    last = cum_t[:, CHUNK - 1:CHUNK]
    wrow_t = dt_t * jnp.exp2(last - cum_t)
    cdec = jnp.exp2(last)
    cum = jnp.concatenate([cum_t, jnp.zeros((CHUNK - DT_ROWS, CHUNK), F32)], axis=0).T
    yield

    gn = SSD_GROUPS * SSD_STATE
    pairs_per_group = SSD_HEADS // SSD_GROUPS // 2
    y_pairs = []
    for g in range(SSD_GROUPS):
        bg = xc_ref[:, D_SSD + g * SSD_STATE:D_SSD + (g + 1) * SSD_STATE]
        cg = xc_ref[:, D_SSD + gn + g * SSD_STATE:D_SSD + gn + (g + 1) * SSD_STATE]
        cbm = _dot_nt(cg, bg)
        bgt = bg.astype(F32).T
        cg32 = cg.astype(F32)
        yield
        for jp in range(pairs_per_group):
            j = g * pairs_per_group + jp
            xs_pair = xc_ref[:, j * LANES:(j + 1) * LANES]
            yp = jnp.zeros((CHUNK, LANES), F32)
            for half in range(2):
                hd = 2 * j + half
                colb = jnp.broadcast_to(cum[:, hd:hd + 1], (CHUNK, LANES))
                dec = jnp.where(causal, jnp.exp2(colb - cum_t[hd:hd + 1, :]), 0.0)
                wp = (cbm * dec * dt_t[hd:hd + 1, :]).astype(BF16)
                gg = (cg32 * jnp.exp2(colb)).astype(BF16)
                in_half = (lane >= SSD_HEAD_DIM) if half else (lane < SSD_HEAD_DIM)
                xs_m = jnp.where(in_half, xs_pair, jnp.zeros_like(xs_pair))
                st = state_ref[hd]
                yp = yp + _dot(jnp.concatenate([wp, gg], axis=1),
                               jnp.concatenate([xs_m, st.astype(BF16)], axis=0))
                l2 = (bgt * wrow_t[hd:hd + 1, :]).astype(BF16)
                state_ref[hd] = st * cdec[hd:hd + 1, :] + _dot(l2, xs_m)
                yield
            y_pairs.append(yp)

    y = jnp.concatenate(y_pairs, axis=1) + dskip_ref[...] * xc_ref[:, :D_SSD].astype(F32)
    y = y * zg_ref[...].astype(F32)
    y_ref[...] = _rms(y, nw_ref[...]).astype(BF16)


def _ssd_kernel(xc_ref, zg_ref, dtt_ref, alog_ref, dskip_ref, nw_ref, y_ref, state_ref):
    @pl.when(pl.program_id(1) == 0)
    def _():
        state_ref[...] = jnp.zeros_like(state_ref)

    row = lax.broadcasted_iota(jnp.int32, (CHUNK, LANES), 0)
    lane = lax.broadcasted_iota(jnp.int32, (CHUNK, LANES), 1)
    causal = row >= lane
    tri_t = jnp.where(row <= lane, 1.0, 0.0).astype(BF16)
    chunks = [_ssd_chunk(xc_ref.at[0, p], zg_ref.at[0, p], dtt_ref[:, 0, p, :], alog_ref, dskip_ref, nw_ref,
                         y_ref.at[0, p], state_ref.at[p], lane, causal, tri_t)
              for p in range(SSD_SEQS_PER_STEP)]
    for _ in itertools.zip_longest(*chunks):
        pass


def _ssd(xc, zg, dtt, alog_col, dskip, nw, B, S):
    nc = S // CHUNK
    nb = SSD_SEQS_PER_STEP
    seqs = lambda a: a.reshape(B // nb, nb, S, a.shape[-1])
    row = lambda n: pl.BlockSpec((1, nb, CHUNK, n), lambda b, c: (b, 0, c, 0))
    y = pl.pallas_call(
        _ssd_kernel,
        grid=(B // nb, nc),
        in_specs=[row(D_XBC), row(D_SSD),
                  pl.BlockSpec((DT_ROWS, 1, nb, CHUNK), lambda b, c: (0, b, 0, c)),
                  _const_spec(alog_col.shape), _const_spec(dskip.shape), _const_spec(nw.shape)],
        out_specs=row(D_SSD),
        out_shape=jax.ShapeDtypeStruct((B // nb, nb, S, D_SSD), BF16),
        scratch_shapes=[pltpu.VMEM((nb, SSD_HEADS, SSD_STATE, LANES), F32)],
        compiler_params=_params("parallel", "arbitrary"),
        name="ssd",
    )(seqs(xc), seqs(zg), dtt.reshape(DT_ROWS, B // nb, nb, S), alog_col, dskip, nw)
    return y.reshape(B * S, D_SSD)


def _diffattn_kernel(lam_ref, subw_ref, q_ref, k_ref, vt_ref, o_ref, sa_scr, sb_scr, acc_scr, m_scr, *, tq, tk, nq):
    hd = 2 * DIFF_HEAD_DIM
    ones = jnp.ones((ONES_ROWS, tk), BF16)
    krow = lax.broadcasted_iota(jnp.int32, (tk, tk), 0)
    qcol = lax.broadcasted_iota(jnp.int32, (tk, tk), 1)
    tri = krow <= qcol
    lv = lam_ref[...]
    lam = (jnp.exp(jnp.sum(lv[0:1] * lv[1:2], axis=-1, keepdims=True))
           - jnp.exp(jnp.sum(lv[2:3] * lv[3:4], axis=-1, keepdims=True)) + LAMBDA_INIT)

    steps = []
    for i in range(nq):
        steps += [(i, kb, "full") for kb in range(2 * i)] + [(i, 2 * i, "diag0"), (i, 2 * i + 1, "diag1")]

    def query_cols(kind):
        return (tk, tq) if kind == "diag1" else (0, tq)

    def scores(step, s_ref):
        i, kb, kind = step
        lo, hi = query_cols(kind)
        kblk = k_ref[kb * tk:(kb + 1) * tk, :]
        for c in range(2):
            s_ref[c, :, lo:hi] = _dot_nt(kblk, q_ref[i * tq + lo:i * tq + hi, c * LANES:(c + 1) * LANES])

    def softmax_pv(step, s_ref):
        i, kb, kind = step
        lo, hi = query_cols(kind)
        par = i % 2
        lhs = jnp.concatenate([vt_ref[:, kb * tk:(kb + 1) * tk], ones], axis=0)
        for c in range(2):
            s = s_ref[c, :, lo:hi]
            if kind == "diag0":
                s = jnp.concatenate([jnp.where(tri, s[:, :tk], -jnp.inf), s[:, tk:]], axis=1)
            elif kind == "diag1":
                s = jnp.where(tri, s, -jnp.inf)
            m_blk = jnp.max(s, axis=0, keepdims=True)
            if kb == 0:
                m_new = m_blk
                acc_scr[par, c, :, lo:hi] = _dot(lhs, jnp.exp2(s - m_new).astype(BF16))
            else:
                m_old = m_scr[par, c, :, lo:hi]
                m_new = jnp.maximum(m_old, m_blk)
                alpha = jnp.exp2(m_old - m_new)
                p = jnp.exp2(s - m_new).astype(BF16)
                acc_scr[par, c, :, lo:hi] = alpha * acc_scr[par, c, :, lo:hi] + _dot(lhs, p)
            m_scr[par, c, :, lo:hi] = m_new

    def finish(i):
        par = i % 2
        a0 = acc_scr[par, 0]
        a1 = acc_scr[par, 1]
        ot = a0[:hd] * (1.0 / a0[hd:hd + 1]) - a1[:hd] * (lam / a1[hd:hd + 1])
        ms = jnp.mean(ot * ot, axis=0, keepdims=True)
        ot = ot * (lax.rsqrt(ms + NORM_EPS) * (1.0 - LAMBDA_INIT)) * subw_ref[...]
        o_ref[i * tq:(i + 1) * tq, :] = ot.T.astype(BF16)

    bufs = (sa_scr, sb_scr)
    scores(steps[0], bufs[0])
    for n, step in enumerate(steps):
        if n + 1 < len(steps):
            scores(steps[n + 1], bufs[(n + 1) % 2])
        softmax_pv(step, bufs[n % 2])
        if step[2] == "diag1":
            finish(step[0])


def _diffattn(lamv, subw_col, q, k, vt, B, S, tq):
    nq = S // tq
    tk = tq // 2
    hd = 2 * DIFF_HEAD_DIM
    return pl.pallas_call(
        functools.partial(_diffattn_kernel, tq=tq, tk=tk, nq=nq),
        grid=(B, DIFF_HEADS),
        in_specs=[_const_spec(lamv.shape), _const_spec(subw_col.shape),
                  pl.BlockSpec((S, 2 * hd), lambda b, h: (b, h)),
                  pl.BlockSpec((S, hd), lambda b, h: (b, h)),
                  pl.BlockSpec((hd, S), lambda b, h: (h, b))],
        out_specs=pl.BlockSpec((S, hd), lambda b, h: (b, h)),
        out_shape=jax.ShapeDtypeStruct((B * S, D_DIFF), BF16),
        scratch_shapes=[pltpu.VMEM((2, tk, tq), F32),
                        pltpu.VMEM((2, tk, tq), F32),
                        pltpu.VMEM((2, 2, hd + ONES_ROWS, tq), F32),
                        pltpu.VMEM((2, 2, 1, tq), F32)],
        compiler_params=_params("parallel", "parallel"),
        name="diffattn",
    )(lamv, subw_col, q, k, vt)


def _outproj_kernel(ys_ref, yd_ref, x_ref, wo_s_ref, wo_d_ref, gpost_ref, gq_ref, wmq_ref, x1_ref, qm_ref):
    mixed = _dot(ys_ref[...], wo_s_ref[...]) + _dot(yd_ref[...], wo_d_ref[...])
    x1 = x_ref[...] + _rms(mixed, gpost_ref[...])
    x1_ref[...] = x1
    hq = _rms(x1, gq_ref[...]).astype(BF16)
    qm_ref[...] = (_dot(hq, wmq_ref[...]) * (MEM_HEAD_DIM ** -0.5)).astype(BF16)


def _outproj(ys, yd, x2, wo_s, wo_d, gpost, gq, wmq, tm):
    T = x2.shape[0]
    row = lambda n: pl.BlockSpec((tm, n), lambda i: (i, 0))
    return pl.pallas_call(
        _outproj_kernel,
        grid=(T // tm,),
        in_specs=[row(D_SSD), row(D_DIFF), row(D_MODEL), _const_spec(wo_s.shape), _const_spec(wo_d.shape),
                  _const_spec(gpost.shape), _const_spec(gq.shape), _const_spec(wmq.shape)],
        out_specs=[row(D_MODEL), row(D_MODEL)],
        out_shape=[jax.ShapeDtypeStruct((T, D_MODEL), F32), jax.ShapeDtypeStruct((T, D_MODEL), BF16)],
        compiler_params=_params("parallel"),
        name="outproj",
    )(ys, yd, x2, wo_s, wo_d, gpost, gq, wmq)


def _memkv_kernel(mem_ref, g_ref, wk_ref, wv_ref, k_ref, v_ref):
    h = _rms(mem_ref[...], g_ref[...]).astype(BF16)
    k_ref[...] = _dot(h, wk_ref[...]).astype(BF16)
    v_ref[...] = _dot(h, wv_ref[...]).astype(BF16)


def _memkv(mem2, g, wk, wv, M):
    R = mem2.shape[0]
    row = pl.BlockSpec((M, D_MODEL), lambda i: (i, 0))
    return pl.pallas_call(
        _memkv_kernel,
        grid=(R // M,),
        in_specs=[row, _const_spec(g.shape), _const_spec(wk.shape), _const_spec(wv.shape)],
        out_specs=[row, row],
        out_shape=[jax.ShapeDtypeStruct((R, D_MODEL), BF16)] * 2,
        compiler_params=_params("parallel"),
        name="memkv",
    )(mem2, g, wk, wv)


def _xattn_kernel(qm_ref, km_ref, vm_ref, x1_ref, wmo_ref, gpost_ref, x2_ref):
    outs = []
    for hd in range(MEM_HEADS):
        sl = slice(hd * MEM_HEAD_DIM, (hd + 1) * MEM_HEAD_DIM)
        s = _dot_nt(qm_ref[:, sl], km_ref[:, sl])
        e = jnp.exp(s - jnp.max(s, axis=-1, keepdims=True))
        p = e / jnp.sum(e, axis=-1, keepdims=True)
        outs.append(_dot(p.astype(BF16), vm_ref[:, sl]).astype(BF16))
    o = jnp.concatenate(outs, axis=1)
    x2_ref[...] = x1_ref[...] + _rms(_dot(o, wmo_ref[...]), gpost_ref[...])


def _xattn(qm, km, vm, x1, wmo, gpost, S, M, tm):
    T = x1.shape[0]
    per_b = S // tm
    row = lambda n: pl.BlockSpec((tm, n), lambda i: (i, 0))
    mem = pl.BlockSpec((M, D_MODEL), lambda i: (i // per_b, 0))
    return pl.pallas_call(
        _xattn_kernel,
        grid=(T // tm,),
        in_specs=[row(D_MODEL), mem, mem, row(D_MODEL), _const_spec(wmo.shape), _const_spec(gpost.shape)],
        out_specs=row(D_MODEL),
        out_shape=jax.ShapeDtypeStruct((T, D_MODEL), F32),
        compiler_params=_params("parallel"),
        name="xattn",
    )(qm, km, vm, x1, wmo, gpost)


def _mlp_kernel(x_ref, gpre_ref, wup_ref, wdn_ref, gpost_ref, o_ref, *, tf):
    x = x_ref[...]
    h = _rms(x, gpre_ref[...]).astype(BF16)
    acc = jnp.zeros(x.shape, F32)
    for c in range(D_FF // tf):
        u = jnp.maximum(_dot(h, wup_ref[:, c * tf:(c + 1) * tf]), 0.0)
        acc = acc + _dot((u * u).astype(BF16), wdn_ref[c * tf:(c + 1) * tf, :])
    o_ref[...] = x + _rms(acc, gpost_ref[...])


def _mlp(x2, gpre, wup, wdn, gpost, tm, tf):
    T = x2.shape[0]
    row = pl.BlockSpec((tm, D_MODEL), lambda i: (i, 0))
    return pl.pallas_call(
        functools.partial(_mlp_kernel, tf=tf),
        grid=(T // tm,),
        in_specs=[row, _const_spec(gpre.shape), _const_spec(wup.shape), _const_spec(wdn.shape),
                  _const_spec(gpost.shape)],
        out_specs=row,
        out_shape=jax.ShapeDtypeStruct((T, D_MODEL), F32),
        compiler_params=_params("parallel"),
        name="mlp",
    )(x2, gpre, wup, wdn, gpost)


def _rope_tables(positions):
    half = DIFF_HEAD_DIM // 2
    inv = ROPE_THETA ** (-jnp.arange(0, DIFF_HEAD_DIM, 2, dtype=F32) / DIFF_HEAD_DIM)
    ang = positions.astype(F32).reshape(-1, 1) * inv
    ang = jnp.concatenate([ang] * (LANES // half), axis=-1)
    sign = jnp.where((jnp.arange(LANES) % DIFF_HEAD_DIM) < half, -1.0, 1.0).astype(F32)
    return jnp.cos(ang), jnp.sin(ang) * sign


def kernel(x, mem, positions, norm_mix_pre, norm_mix_post, norm_mem_q, norm_mem_kv, norm_mem_post,
           norm_mlp_pre, norm_mlp_post, w_in, conv_w, conv_b, dt_bias, a_log, d_skip, ssd_norm_w,
           lambda_q1, lambda_k1, lambda_q2, lambda_k2, subln_w, w_out, w_mq, w_mk, w_mv, w_mo, w_up, w_down):
    B, S, _ = x.shape
    M = mem.shape[1]
    T = B * S
    assert norm_mix_pre.shape[0] == 1, "single-layer trunk"
    x2 = x.reshape(T, D_MODEL)
    cos, sin = _rope_tables(positions)

    w = w_in[0]
    o0 = D_SSD
    o1 = o0 + D_XBC
    o2 = o1 + SSD_HEADS
    o3 = o2 + D_DIFF
    o4 = o3 + D_DIFF
    wz = w[:, :o0].astype(BF16)
    wxbc = w[:, o0:o1].astype(BF16)
    wq = w[:, o2:o3].astype(BF16)
    wk = w[:, o3:o4].astype(BF16)
    head_pad = ((0, DT_ROWS - SSD_HEADS), (0, 0))
    wvdt = jnp.concatenate([w[:, o4:].T, jnp.pad(w[:, o1:o2].T, head_pad)], axis=0).astype(BF16)
    head_col = lambda p: jnp.pad(p.reshape(SSD_HEADS, 1), head_pad)

    zg, xc, dtt, q, k, vt = _inproj(x2, norm_mix_pre, cos, sin, wz, wxbc, wq, wk, wvdt,
                                    conv_w[0, :, 0, :], conv_b, head_col(dt_bias), S, tm=512)

    y_ssd = _ssd(xc, zg, dtt, head_col(a_log), jnp.repeat(d_skip, SSD_HEAD_DIM, axis=1), ssd_norm_w, B, S)

    lamv = jnp.concatenate([lambda_q1, lambda_k1, lambda_q2, lambda_k2], axis=0)
    y_diff = _diffattn(lamv, subln_w.reshape(2 * DIFF_HEAD_DIM, 1), q, k, vt, B, S, tq=512)

    wo = w_out[0].astype(BF16)
    x1, qm = _outproj(y_ssd, y_diff, x2, wo[:D_SSD], wo[D_SSD:], norm_mix_post, norm_mem_q,
                      w_mq[0].astype(BF16), tm=512)
    km, vm = _memkv(mem.reshape(B * M, D_MODEL), norm_mem_kv, w_mk[0].astype(BF16), w_mv[0].astype(BF16), M)
    x2b = _xattn(qm, km, vm, x1, w_mo[0].astype(BF16), norm_mem_post, S, M, tm=512)
    out = _mlp(x2b, norm_mlp_pre, w_up[0].astype(BF16), w_down[0].astype(BF16), norm_mlp_post, tm=512, tf=1024)
    return out.reshape(B, S, D_MODEL)
```

```python
import functools
import itertools
import math

import jax
import jax.numpy as jnp
from jax import lax
from jax.experimental import pallas as pl
from jax.experimental.pallas import tpu as pltpu

F32 = jnp.float32
BF16 = jnp.bfloat16

D_MODEL = 1024
D_SSD = 512
SSD_HEAD_DIM = 64
SSD_HEADS = D_SSD // SSD_HEAD_DIM
SSD_GROUPS = 2
SSD_STATE = 128
CONV_WIDTH = 4
CHUNK = 128
D_XBC = D_SSD + 2 * SSD_GROUPS * SSD_STATE
D_DIFF = D_MODEL - D_SSD
DIFF_HEAD_DIM = 64
DIFF_HEADS = D_DIFF // (2 * DIFF_HEAD_DIM)
ROPE_THETA = 10000.0
MEM_HEADS = 4
MEM_HEAD_DIM = D_MODEL // MEM_HEADS
D_FF = 4 * D_MODEL
NORM_EPS = 1e-6
LAMBDA_INIT = 0.8 - 0.6 * math.exp(-0.3 * 0)

LOG2E = math.log2(math.e)
LANES = 128
SUBLANES = 8
ONES_ROWS = 16
DT_ROWS = 16
CONV_COLS = 256
SSD_SEQS_PER_STEP = 4
VMEM_LIMIT = 56 * 1024 * 1024


def _dot(a, b):
    return jnp.dot(a, b, preferred_element_type=F32)


def _dot_nt(a, b):
    return lax.dot_general(a, b, (((1,), (1,)), ((), ())), preferred_element_type=F32)


def _rms(x, g):
    ms = jnp.mean(x * x, axis=-1, keepdims=True)
    return x * lax.rsqrt(ms + NORM_EPS) * g


def _silu(a):
    half = 0.5 * a
    return half + half * jnp.tanh(half)


def _const_spec(shape):
    nd = len(shape)
    return pl.BlockSpec(shape, lambda *_: (0,) * nd)


def _params(*sem):
    return pltpu.CompilerParams(dimension_semantics=sem, vmem_limit_bytes=VMEM_LIMIT)


def _inproj_kernel(x_ref, g_ref, cos_ref, sin_ref, wz_ref, wxbc_ref, wq_ref, wk_ref, wvdt_ref,
                   cw_ref, cb_ref, dtb_ref,
                   zg_ref, xc_ref, dtt_ref, q_ref, k_ref, vt_ref, halo_ref, *, tm, tiles_per_seq):
    h = _rms(x_ref[...], g_ref[...]).astype(BF16)

    @pl.when(pl.program_id(0) % tiles_per_seq == 0)
    def _():
        halo_ref[...] = jnp.zeros_like(halo_ref)

    def gate_epilogue(z):
        zg_ref[...] = _silu(z).astype(BF16)

    groups = tm // SUBLANES
    sub = lax.broadcasted_iota(jnp.int32, (1, SUBLANES, 1), 1)

    def conv_epilogue(c, xbc):
        cols = slice(c * CONV_COLS, (c + 1) * CONV_COLS)
        xb = xbc.reshape(groups, SUBLANES, CONV_COLS)
        prev = halo_ref[:, cols]
        acc = cb_ref[:, cols] + cw_ref[CONV_WIDTH - 1:CONV_WIDTH, cols] * xb
        for k in range(1, CONV_WIDTH):
            w = cw_ref[CONV_WIDTH - 1 - k:CONV_WIDTH - k, cols]
            r = pltpu.roll(xb * w, k, 1)
            r_first = pltpu.roll(prev * w, k, 0)[None]
            acc = acc + jnp.where(sub >= k, r, jnp.concatenate([r_first, r[:groups - 1]], axis=0))
        halo_ref[:, cols] = xb[groups - 1]
        xc_ref[:, cols] = _silu(acc.reshape(tm, CONV_COLS)).astype(BF16)

    def vdt_epilogue(r):
        vt_ref[...] = r[:D_DIFF].astype(BF16)
        dtt_ref[...] = jax.nn.softplus(r[D_DIFF:] + dtb_ref[...])

    cos = cos_ref[...]
    sin = sin_ref[...]
    lane = lax.broadcasted_iota(jnp.int32, cos.shape, 1)
    comp0 = (lane % DIFF_HEAD_DIM) < DIFF_HEAD_DIM // 2

    def rope(a, c):
        ac = a[:, c * LANES:(c + 1) * LANES]
        return ac * cos + pltpu.roll(ac, LANES // 2, 1) * sin

    def key_epilogue(a):
        for c in range(DIFF_HEADS):
            k_ref[:, c * LANES:(c + 1) * LANES] = rope(a, c).astype(BF16)

    def query_epilogue(a):
        for c in range(DIFF_HEADS):
            qh = rope(a, c) * (DIFF_HEAD_DIM ** -0.5 * LOG2E)
            q_ref[:, (2 * c) * LANES:(2 * c + 1) * LANES] = jnp.where(comp0, qh, 0.0).astype(BF16)
            q_ref[:, (2 * c + 1) * LANES:(2 * c + 2) * LANES] = jnp.where(comp0, 0.0, qh).astype(BF16)

    stages = [(lambda: _dot(h, wz_ref[...]), gate_epilogue)]
    for c in range(D_XBC // CONV_COLS):
        stages.append((functools.partial(lambda c: _dot(h, wxbc_ref[:, c * CONV_COLS:(c + 1) * CONV_COLS]), c),
                       functools.partial(conv_epilogue, c)))
    stages += [(lambda: _dot_nt(wvdt_ref[...], h), vdt_epilogue),
               (lambda: _dot(h, wk_ref[...]), key_epilogue),
               (lambda: _dot(h, wq_ref[...]), query_epilogue)]
    pending = stages[0][0]()
    for n, (_, epilogue) in enumerate(stages):
        ahead = stages[n + 1][0]() if n + 1 < len(stages) else None
        epilogue(pending)
        pending = ahead


def _inproj(x2, g, cos, sin, wz, wxbc, wq, wk, wvdt, cw, cb, dtb_col, S, tm):
    T = x2.shape[0]
    row = lambda n: pl.BlockSpec((tm, n), lambda i: (i, 0))
    col = lambda n: pl.BlockSpec((n, tm), lambda i: (0, i))
    return pl.pallas_call(
        functools.partial(_inproj_kernel, tm=tm, tiles_per_seq=S // tm),
        grid=(T // tm,),
        in_specs=[row(D_MODEL), _const_spec((1, D_MODEL)), row(LANES), row(LANES),
                  _const_spec(wz.shape), _const_spec(wxbc.shape),
                  _const_spec(wq.shape), _const_spec(wk.shape), _const_spec(wvdt.shape),
                  _const_spec(cw.shape), _const_spec(cb.shape), _const_spec(dtb_col.shape)],
        out_specs=[row(D_SSD), row(D_XBC), col(DT_ROWS), row(2 * D_DIFF), row(D_DIFF), col(D_DIFF)],
        out_shape=[jax.ShapeDtypeStruct((T, D_SSD), BF16),
                   jax.ShapeDtypeStruct((T, D_XBC), BF16),
                   jax.ShapeDtypeStruct((DT_ROWS, T), F32),
                   jax.ShapeDtypeStruct((T, 2 * D_DIFF), BF16),
                   jax.ShapeDtypeStruct((T, D_DIFF), BF16),
                   jax.ShapeDtypeStruct((D_DIFF, T), BF16)],
        scratch_shapes=[pltpu.VMEM((SUBLANES, D_XBC), F32)],
        compiler_params=_params("arbitrary"),
        name="inproj",
    )(x2, g, cos, sin, wz, wxbc, wq, wk, wvdt, cw, cb, dtb_col)


def _ssd_chunk(xc_ref, zg_ref, dt_t, alog_ref, dskip_ref, nw_ref, y_ref, state_ref, lane, causal, tri_t):
    da_t = dt_t * (-LOG2E * jnp.exp(alog_ref[...]))
    hi = da_t.astype(BF16)
    rem = da_t - hi.astype(F32)
    mid = rem.astype(BF16)
    lo = (rem - mid.astype(F32)).astype(BF16)
    cum_t = _dot(hi, tri_t) + _dot(mid, tri_t) + _dot(lo, tri_t)
    yield
    last = cum_t[:, CHUNK - 1:CHUNK]
    wrow_t = dt_t * jnp.exp2(last - cum_t)
    cdec = jnp.exp2(last)
    cum = jnp.concatenate([cum_t, jnp.zeros((CHUNK - DT_ROWS, CHUNK), F32)], axis=0).T
    yield

    gn = SSD_GROUPS * SSD_STATE
    pairs_per_group = SSD_HEADS // SSD_GROUPS // 2
    y_pairs = []
    for g in range(SSD_GROUPS):
        bg = xc_ref[:, D_SSD + g * SSD_STATE:D_SSD + (g + 1) * SSD_STATE]
        cg = xc_ref[:, D_SSD + gn + g * SSD_STATE:D_SSD + gn + (g + 1) * SSD_STATE]
        cbm = _dot_nt(cg, bg)
        bgt = bg.astype(F32).T
        cg32 = cg.astype(F32)
        yield
        for jp in range(pairs_per_group):
            j = g * pairs_per_group + jp
            xs_pair = xc_ref[:, j * LANES:(j + 1) * LANES]
            yp = jnp.zeros((CHUNK, LANES), F32)
            for half in range(2):
                hd = 2 * j + half
                colb = jnp.broadcast_to(cum[:, hd:hd + 1], (CHUNK, LANES))
                dec = jnp.where(causal, jnp.exp2(colb - cum_t[hd:hd + 1, :]), 0.0)
                wp = (cbm * dec * dt_t[hd:hd + 1, :]).astype(BF16)
                gg = (cg32 * jnp.exp2(colb)).astype(BF16)
                in_half = (lane >= SSD_HEAD_DIM) if half else (lane < SSD_HEAD_DIM)
                xs_m = jnp.where(in_half, xs_pair, jnp.zeros_like(xs_pair))
                st = state_ref[hd]
                yp = yp + _dot(jnp.concatenate([wp, gg], axis=1),
                               jnp.concatenate([xs_m, st.astype(BF16)], axis=0))
                l2 = (bgt * wrow_t[hd:hd + 1, :]).astype(BF16)
                state_ref[hd] = st * cdec[hd:hd + 1, :] + _dot(l2, xs_m)
                yield
            y_pairs.append(yp)

    y = jnp.concatenate(y_pairs, axis=1) + dskip_ref[...] * xc_ref[:, :D_SSD].astype(F32)
    y = y * zg_ref[...].astype(F32)
    y_ref[...] = _rms(y, nw_ref[...]).astype(BF16)


def _ssd_kernel(xc_ref, zg_ref, dtt_ref, alog_ref, dskip_ref, nw_ref, y_ref, state_ref):
    @pl.when(pl.program_id(1) == 0)
    def _():
        state_ref[...] = jnp.zeros_like(state_ref)

    row = lax.broadcasted_iota(jnp.int32, (CHUNK, LANES), 0)
    lane = lax.broadcasted_iota(jnp.int32, (CHUNK, LANES), 1)
    causal = row >= lane
    tri_t = jnp.where(row <= lane, 1.0, 0.0).astype(BF16)
    chunks = [_ssd_chunk(xc_ref.at[0, p], zg_ref.at[0, p], dtt_ref[:, 0, p, :], alog_ref, dskip_ref, nw_ref,
                         y_ref.at[0, p], state_ref.at[p], lane, causal, tri_t)
              for p in range(SSD_SEQS_PER_STEP)]
    for _ in itertools.zip_longest(*chunks):
        pass


def _ssd(xc, zg, dtt, alog_col, dskip, nw, B, S):
    nc = S // CHUNK
    nb = SSD_SEQS_PER_STEP
    seqs = lambda a: a.reshape(B // nb, nb, S, a.shape[-1])
    row = lambda n: pl.BlockSpec((1, nb, CHUNK, n), lambda b, c: (b, 0, c, 0))
    y = pl.pallas_call(
        _ssd_kernel,
        grid=(B // nb, nc),
        in_specs=[row(D_XBC), row(D_SSD),
                  pl.BlockSpec((DT_ROWS, 1, nb, CHUNK), lambda b, c: (0, b, 0, c)),
                  _const_spec(alog_col.shape), _const_spec(dskip.shape), _const_spec(nw.shape)],
        out_specs=row(D_SSD),
        out_shape=jax.ShapeDtypeStruct((B // nb, nb, S, D_SSD), BF16),
        scratch_shapes=[pltpu.VMEM((nb, SSD_HEADS, SSD_STATE, LANES), F32)],
        compiler_params=_params("parallel", "arbitrary"),
        name="ssd",
    )(seqs(xc), seqs(zg), dtt.reshape(DT_ROWS, B // nb, nb, S), alog_col, dskip, nw)
    return y.reshape(B * S, D_SSD)


def _diffattn_kernel(lam_ref, subw_ref, q_ref, k_ref, vt_ref, o_ref, sa_scr, sb_scr, acc_scr, m_scr, *, tq, tk, nq):
    hd = 2 * DIFF_HEAD_DIM
    ones = jnp.ones((ONES_ROWS, tk), BF16)
    krow = lax.broadcasted_iota(jnp.int32, (tk, tk), 0)
    qcol = lax.broadcasted_iota(jnp.int32, (tk, tk), 1)
    tri = krow <= qcol
    lv = lam_ref[...]
    lam = (jnp.exp(jnp.sum(lv[0:1] * lv[1:2], axis=-1, keepdims=True))
           - jnp.exp(jnp.sum(lv[2:3] * lv[3:4], axis=-1, keepdims=True)) + LAMBDA_INIT)

    steps = []
    for i in range(nq):
        steps += [(i, kb, "full") for kb in range(2 * i)] + [(i, 2 * i, "diag0"), (i, 2 * i + 1, "diag1")]

    def query_cols(kind):
        return (tk, tq) if kind == "diag1" else (0, tq)

    def scores(step, s_ref):
        i, kb, kind = step
        lo, hi = query_cols(kind)
        kblk = k_ref[kb * tk:(kb + 1) * tk, :]
        for c in range(2):
            s_ref[c, :, lo:hi] = _dot_nt(kblk, q_ref[i * tq + lo:i * tq + hi, c * LANES:(c + 1) * LANES])

    def softmax_pv(step, s_ref):
        i, kb, kind = step
        lo, hi = query_cols(kind)
        par = i % 2
        lhs = jnp.concatenate([vt_ref[:, kb * tk:(kb + 1) * tk], ones], axis=0)
        for c in range(2):
            s = s_ref[c, :, lo:hi]
            if kind == "diag0":
                s = jnp.concatenate([jnp.where(tri, s[:, :tk], -jnp.inf), s[:, tk:]], axis=1)
            elif kind == "diag1":
                s = jnp.where(tri, s, -jnp.inf)
            m_blk = jnp.max(s, axis=0, keepdims=True)
            if kb == 0:
                m_new = m_blk
                acc_scr[par, c, :, lo:hi] = _dot(lhs, jnp.exp2(s - m_new).astype(BF16))
            else:
                m_old = m_scr[par, c, :, lo:hi]
                m_new = jnp.maximum(m_old, m_blk)
                alpha = jnp.exp2(m_old - m_new)
                p = jnp.exp2(s - m_new).astype(BF16)
                acc_scr[par, c, :, lo:hi] = alpha * acc_scr[par, c, :, lo:hi] + _dot(lhs, p)
            m_scr[par, c, :, lo:hi] = m_new

    def finish(i):
        par = i % 2
        a0 = acc_scr[par, 0]
        a1 = acc_scr[par, 1]
        ot = a0[:hd] * (1.0 / a0[hd:hd + 1]) - a1[:hd] * (lam / a1[hd:hd + 1])
        ms = jnp.mean(ot * ot, axis=0, keepdims=True)
        ot = ot * (lax.rsqrt(ms + NORM_EPS) * (1.0 - LAMBDA_INIT)) * subw_ref[...]
        o_ref[i * tq:(i + 1) * tq, :] = ot.T.astype(BF16)

    bufs = (sa_scr, sb_scr)
    scores(steps[0], bufs[0])
    for n, step in enumerate(steps):
        if n + 1 < len(steps):
            scores(steps[n + 1], bufs[(n + 1) % 2])
        softmax_pv(step, bufs[n % 2])
        if step[2] == "diag1":
            finish(step[0])


def _diffattn(lamv, subw_col, q, k, vt, B, S, tq):
    nq = S // tq
    tk = tq // 2
    hd = 2 * DIFF_HEAD_DIM
    return pl.pallas_call(
        functools.partial(_diffattn_kernel, tq=tq, tk=tk, nq=nq),
        grid=(B, DIFF_HEADS),
        in_specs=[_const_spec(lamv.shape), _const_spec(subw_col.shape),
                  pl.BlockSpec((S, 2 * hd), lambda b, h: (b, h)),
                  pl.BlockSpec((S, hd), lambda b, h: (b, h)),
                  pl.BlockSpec((hd, S), lambda b, h: (h, b))],
        out_specs=pl.BlockSpec((S, hd), lambda b, h: (b, h)),
        out_shape=jax.ShapeDtypeStruct((B * S, D_DIFF), BF16),
        scratch_shapes=[pltpu.VMEM((2, tk, tq), F32),
                        pltpu.VMEM((2, tk, tq), F32),
                        pltpu.VMEM((2, 2, hd + ONES_ROWS, tq), F32),
                        pltpu.VMEM((2, 2, 1, tq), F32)],
        compiler_params=_params("parallel", "parallel"),
        name="diffattn",
    )(lamv, subw_col, q, k, vt)


def _outproj_kernel(ys_ref, yd_ref, x_ref, wo_s_ref, wo_d_ref, gpost_ref, gq_ref, wmq_ref, x1_ref, qm_ref):
    mixed = _dot(ys_ref[...], wo_s_ref[...]) + _dot(yd_ref[...], wo_d_ref[...])
    x1 = x_ref[...] + _rms(mixed, gpost_ref[...])
    x1_ref[...] = x1
    hq = _rms(x1, gq_ref[...]).astype(BF16)
    qm_ref[...] = (_dot(hq, wmq_ref[...]) * (MEM_HEAD_DIM ** -0.5)).astype(BF16)


def _outproj(ys, yd, x2, wo_s, wo_d, gpost, gq, wmq, tm):
    T = x2.shape[0]
    row = lambda n: pl.BlockSpec((tm, n), lambda i: (i, 0))
    return pl.pallas_call(
        _outproj_kernel,
        grid=(T // tm,),
        in_specs=[row(D_SSD), row(D_DIFF), row(D_MODEL), _const_spec(wo_s.shape), _const_spec(wo_d.shape),
                  _const_spec(gpost.shape), _const_spec(gq.shape), _const_spec(wmq.shape)],
        out_specs=[row(D_MODEL), row(D_MODEL)],
        out_shape=[jax.ShapeDtypeStruct((T, D_MODEL), F32), jax.ShapeDtypeStruct((T, D_MODEL), BF16)],
        compiler_params=_params("parallel"),
        name="outproj",
    )(ys, yd, x2, wo_s, wo_d, gpost, gq, wmq)


def _memkv_kernel(mem_ref, g_ref, wk_ref, wv_ref, k_ref, v_ref):
    h = _rms(mem_ref[...], g_ref[...]).astype(BF16)
    k_ref[...] = _dot(h, wk_ref[...]).astype(BF16)
    v_ref[...] = _dot(h, wv_ref[...]).astype(BF16)


def _memkv(mem2, g, wk, wv, M):
    R = mem2.shape[0]
    row = pl.BlockSpec((M, D_MODEL), lambda i: (i, 0))
    return pl.pallas_call(
        _memkv_kernel,
        grid=(R // M,),
        in_specs=[row, _const_spec(g.shape), _const_spec(wk.shape), _const_spec(wv.shape)],
        out_specs=[row, row],
        out_shape=[jax.ShapeDtypeStruct((R, D_MODEL), BF16)] * 2,
        compiler_params=_params("parallel"),
        name="memkv",
    )(mem2, g, wk, wv)


def _xattn_kernel(qm_ref, km_ref, vm_ref, x1_ref, wmo_ref, gpost_ref, x2_ref):
    outs = []
    for hd in range(MEM_HEADS):
        sl = slice(hd * MEM_HEAD_DIM, (hd + 1) * MEM_HEAD_DIM)
        s = _dot_nt(qm_ref[:, sl], km_ref[:, sl])
        e = jnp.exp(s - jnp.max(s, axis=-1, keepdims=True))
        p = e / jnp.sum(e, axis=-1, keepdims=True)
        outs.append(_dot(p.astype(BF16), vm_ref[:, sl]).astype(BF16))
    o = jnp.concatenate(outs, axis=1)
    x2_ref[...] = x1_ref[...] + _rms(_dot(o, wmo_ref[...]), gpost_ref[...])


def _xattn(qm, km, vm, x1, wmo, gpost, S, M, tm):
    T = x1.shape[0]
    per_b = S // tm
    row = lambda n: pl.BlockSpec((tm, n), lambda i: (i, 0))
    mem = pl.BlockSpec((M, D_MODEL), lambda i: (i // per_b, 0))
    return pl.pallas_call(
        _xattn_kernel,
        grid=(T // tm,),
        in_specs=[row(D_MODEL), mem, mem, row(D_MODEL), _const_spec(wmo.shape), _const_spec(gpost.shape)],
        out_specs=row(D_MODEL),
        out_shape=jax.ShapeDtypeStruct((T, D_MODEL), F32),
        compiler_params=_params("parallel"),
        name="xattn",
    )(qm, km, vm, x1, wmo, gpost)


def _mlp_kernel(x_ref, gpre_ref, wup_ref, wdn_ref, gpost_ref, o_ref, *, tf):
    x = x_ref[...]
    h = _rms(x, gpre_ref[...]).astype(BF16)
    acc = jnp.zeros(x.shape, F32)
    for c in range(D_FF // tf):
        u = jnp.maximum(_dot(h, wup_ref[:, c * tf:(c + 1) * tf]), 0.0)
        acc = acc + _dot((u * u).astype(BF16), wdn_ref[c * tf:(c + 1) * tf, :])
    o_ref[...] = x + _rms(acc, gpost_ref[...])


def _mlp(x2, gpre, wup, wdn, gpost, tm, tf):
    T = x2.shape[0]
    row = pl.BlockSpec((tm, D_MODEL), lambda i: (i, 0))
    return pl.pallas_call(
        functools.partial(_mlp_kernel, tf=tf),
        grid=(T // tm,),
        in_specs=[row, _const_spec(gpre.shape), _const_spec(wup.shape), _const_spec(wdn.shape),
                  _const_spec(gpost.shape)],
        out_specs=row,
        out_shape=jax.ShapeDtypeStruct((T, D_MODEL), F32),
        compiler_params=_params("parallel"),
        name="mlp",
    )(x2, gpre, wup, wdn, gpost)


def _rope_tables(positions):
    half = DIFF_HEAD_DIM // 2
    inv = ROPE_THETA ** (-jnp.arange(0, DIFF_HEAD_DIM, 2, dtype=F32) / DIFF_HEAD_DIM)
    ang = positions.astype(F32).reshape(-1, 1) * inv
    sign = jnp.where(jnp.arange(LANES) < LANES // 2, -1.0, 1.0).astype(F32)
    reps = LANES // half
    return jnp.tile(jnp.cos(ang), (1, reps)), jnp.tile(jnp.sin(ang), (1, reps)) * sign


def _head_lane_order(wcols):
    half = DIFF_HEAD_DIM // 2
    lane = jnp.arange(LANES)
    src = ((lane % DIFF_HEAD_DIM) // half) * DIFF_HEAD_DIM + (lane // DIFF_HEAD_DIM) * half + lane % half
    idx = (jnp.arange(DIFF_HEADS)[:, None] * LANES + src[None, :]).reshape(-1)
    return wcols[:, idx]


def kernel(x, mem, positions, norm_mix_pre, norm_mix_post, norm_mem_q, norm_mem_kv, norm_mem_post,
           norm_mlp_pre, norm_mlp_post, w_in, conv_w, conv_b, dt_bias, a_log, d_skip, ssd_norm_w,
           lambda_q1, lambda_k1, lambda_q2, lambda_k2, subln_w, w_out, w_mq, w_mk, w_mv, w_mo, w_up, w_down):
    B, S, _ = x.shape
    M = mem.shape[1]
    T = B * S
    assert norm_mix_pre.shape[0] == 1, "single-layer trunk"
    x2 = x.reshape(T, D_MODEL)
    cos, sin = _rope_tables(positions)

    w = w_in[0]
    o0 = D_SSD
    o1 = o0 + D_XBC
    o2 = o1 + SSD_HEADS
    o3 = o2 + D_DIFF
    o4 = o3 + D_DIFF
    wz = w[:, :o0].astype(BF16)
    wxbc = w[:, o0:o1].astype(BF16)
    wq = _head_lane_order(w[:, o2:o3]).astype(BF16)
    wk = _head_lane_order(w[:, o3:o4]).astype(BF16)
    head_pad = ((0, DT_ROWS - SSD_HEADS), (0, 0))
    wvdt = jnp.concatenate([w[:, o4:].T, jnp.pad(w[:, o1:o2].T, head_pad)], axis=0).astype(BF16)
    head_col = lambda p: jnp.pad(p.reshape(SSD_HEADS, 1), head_pad)

    zg, xc, dtt, q, k, vt = _inproj(x2, norm_mix_pre, cos, sin, wz, wxbc, wq, wk, wvdt,
                                    conv_w[0, :, 0, :], conv_b, head_col(dt_bias), S, tm=512)

    y_ssd = _ssd(xc, zg, dtt, head_col(a_log), jnp.repeat(d_skip, SSD_HEAD_DIM, axis=1), ssd_norm_w, B, S)

    lamv = jnp.concatenate([lambda_q1, lambda_k1, lambda_q2, lambda_k2], axis=0)
    y_diff = _diffattn(lamv, subln_w.reshape(2 * DIFF_HEAD_DIM, 1), q, k, vt, B, S, tq=512)

    wo = w_out[0].astype(BF16)
    x1, qm = _outproj(y_ssd, y_diff, x2, wo[:D_SSD], wo[D_SSD:], norm_mix_post, norm_mem_q,
                      w_mq[0].astype(BF16), tm=512)
    km, vm = _memkv(mem.reshape(B * M, D_MODEL), norm_mem_kv, w_mk[0].astype(BF16), w_mv[0].astype(BF16), M)
    x2b = _xattn(qm, km, vm, x1, w_mo[0].astype(BF16), norm_mem_post, S, M, tm=512)
    out = _mlp(x2b, norm_mlp_pre, w_up[0].astype(BF16), w_down[0].astype(BF16), norm_mlp_post, tm=512, tf=1024)
    return out.reshape(B, S, D_MODEL)
```

```python
import functools
import itertools
import math

import jax
import jax.numpy as jnp
from jax import lax
from jax.experimental import pallas as pl
from jax.experimental.pallas import tpu as pltpu

F32 = jnp.float32
BF16 = jnp.bfloat16

D_MODEL = 1024
D_SSD = 512
SSD_HEAD_DIM = 64
SSD_HEADS = D_SSD // SSD_HEAD_DIM
SSD_GROUPS = 2
SSD_STATE = 128
CONV_WIDTH = 4
CHUNK = 128
D_XBC = D_SSD + 2 * SSD_GROUPS * SSD_STATE
D_DIFF = D_MODEL - D_SSD
DIFF_HEAD_DIM = 64
DIFF_HEADS = D_DIFF // (2 * DIFF_HEAD_DIM)
ROPE_THETA = 10000.0
MEM_HEADS = 4
MEM_HEAD_DIM = D_MODEL // MEM_HEADS
D_FF = 4 * D_MODEL
NORM_EPS = 1e-6
LAMBDA_INIT = 0.8 - 0.6 * math.exp(-0.3 * 0)

LOG2E = math.log2(math.e)
LANES = 128
SUBLANES = 8
ONES_ROWS = 16
DT_ROWS = 16
CONV_COLS = 256
SSD_SEQS_PER_STEP = 4
ROW_GROUP = 256
VMEM_LIMIT = 56 * 1024 * 1024


def _dot(a, b):
    return jnp.dot(a, b, preferred_element_type=F32)


def _dot_nt(a, b):
    return lax.dot_general(a, b, (((1,), (1,)), ((), ())), preferred_element_type=F32)


def _rms(x, g):
    ms = jnp.mean(x * x, axis=-1, keepdims=True)
    return x * lax.rsqrt(ms + NORM_EPS) * g


def _silu(a):
    half = 0.5 * a
    return half + half * jnp.tanh(half)


def _const_spec(shape):
    nd = len(shape)
    return pl.BlockSpec(shape, lambda *_: (0,) * nd)


def _params(*sem):
    return pltpu.CompilerParams(dimension_semantics=sem, vmem_limit_bytes=VMEM_LIMIT)


def _inproj_kernel(x_ref, g_ref, cos_ref, sin_ref, wz_ref, wxbc_ref, wq_ref, wk_ref, wvdt_ref,
                   cw_ref, cb_ref, dtb_ref,
                   zg_ref, xc_ref, dtt_ref, q_ref, k_ref, vt_ref, halo_ref, *, tm, tiles_per_seq):
    @pl.when(pl.program_id(0) % tiles_per_seq == 0)
    def _():
        halo_ref[...] = jnp.zeros_like(halo_ref)

    vregs = ROW_GROUP // SUBLANES
    sub = lax.broadcasted_iota(jnp.int32, (1, SUBLANES, 1), 1)
    reps = LANES // (DIFF_HEAD_DIM // 2)
    lane = lax.broadcasted_iota(jnp.int32, (ROW_GROUP, LANES), 1)
    comp0 = (lane % DIFF_HEAD_DIM) < DIFF_HEAD_DIM // 2
    history = {}

    def row_group(rows):
        h = _rms(x_ref[rows, :], g_ref[...]).astype(BF16)

        z = _dot(h, wz_ref[...])
        yield
        zg_ref[rows, :] = _silu(z).astype(BF16)

        for c in range(D_XBC // CONV_COLS):
            cols = slice(c * CONV_COLS, (c + 1) * CONV_COLS)
            xbc = _dot(h, wxbc_ref[:, cols])
            yield
            xb = xbc.reshape(vregs, SUBLANES, CONV_COLS)
            prev = history[c] if c in history else halo_ref[:, cols]
            acc = cb_ref[:, cols] + cw_ref[CONV_WIDTH - 1:CONV_WIDTH, cols] * xb
            for k in range(1, CONV_WIDTH):
                w = cw_ref[CONV_WIDTH - 1 - k:CONV_WIDTH - k, cols]
                r = pltpu.roll(xb * w, k, 1)
                r_first = pltpu.roll(prev * w, k, 0)[None]
                acc = acc + jnp.where(sub >= k, r, jnp.concatenate([r_first, r[:vregs - 1]], axis=0))
            history[c] = xb[vregs - 1]
            xc_ref[rows, cols] = _silu(acc.reshape(ROW_GROUP, CONV_COLS)).astype(BF16)

        r = _dot_nt(wvdt_ref[...], h)
        yield
        vt_ref[:, rows] = r[:D_DIFF].astype(BF16)
        dtt_ref[:, rows] = jax.nn.softplus(r[D_DIFF:] + dtb_ref[...])

        cos = jnp.concatenate([cos_ref[rows, :]] * reps, axis=1)
        sin = jnp.concatenate([sin_ref[rows, :]] * reps, axis=1)
        sin = jnp.where(lane < LANES // 2, -sin, sin)

        def rope(a, c):
            ac = a[:, c * LANES:(c + 1) * LANES]
            return ac * cos + pltpu.roll(ac, LANES // 2, 1) * sin

        a = _dot(h, wk_ref[...])
        yield
        for c in range(DIFF_HEADS):
            k_ref[rows, c * LANES:(c + 1) * LANES] = rope(a, c).astype(BF16)

        a = _dot(h, wq_ref[...])
        yield
        for c in range(DIFF_HEADS):
            qh = rope(a, c) * (DIFF_HEAD_DIM ** -0.5 * LOG2E)
            q_ref[rows, (2 * c) * LANES:(2 * c + 1) * LANES] = jnp.where(comp0, qh, 0.0).astype(BF16)
            q_ref[rows, (2 * c + 1) * LANES:(2 * c + 2) * LANES] = jnp.where(comp0, 0.0, qh).astype(BF16)

    for _ in itertools.zip_longest(*[row_group(slice(r, r + ROW_GROUP)) for r in range(0, tm, ROW_GROUP)]):
        pass
    for c, last_rows in history.items():
        halo_ref[:, c * CONV_COLS:(c + 1) * CONV_COLS] = last_rows


def _inproj(x2, g, cos, sin, wz, wxbc, wq, wk, wvdt, cw, cb, dtb_col, S, tm):
    T = x2.shape[0]
    row = lambda n: pl.BlockSpec((tm, n), lambda i: (i, 0))
    col = lambda n: pl.BlockSpec((n, tm), lambda i: (0, i))
    return pl.pallas_call(
        functools.partial(_inproj_kernel, tm=tm, tiles_per_seq=S // tm),
        grid=(T // tm,),
        in_specs=[row(D_MODEL), _const_spec((1, D_MODEL)), row(cos.shape[1]), row(sin.shape[1]),
                  _const_spec(wz.shape), _const_spec(wxbc.shape),
                  _const_spec(wq.shape), _const_spec(wk.shape), _const_spec(wvdt.shape),
                  _const_spec(cw.shape), _const_spec(cb.shape), _const_spec(dtb_col.shape)],
        out_specs=[row(D_SSD), row(D_XBC), col(DT_ROWS), row(2 * D_DIFF), row(D_DIFF), col(D_DIFF)],
        out_shape=[jax.ShapeDtypeStruct((T, D_SSD), BF16),
                   jax.ShapeDtypeStruct((T, D_XBC), BF16),
                   jax.ShapeDtypeStruct((DT_ROWS, T), F32),
                   jax.ShapeDtypeStruct((T, 2 * D_DIFF), BF16),
                   jax.ShapeDtypeStruct((T, D_DIFF), BF16),
                   jax.ShapeDtypeStruct((D_DIFF, T), BF16)],
        scratch_shapes=[pltpu.VMEM((SUBLANES, D_XBC), F32)],
        compiler_params=_params("arbitrary"),
        name="inproj",
    )(x2, g, cos, sin, wz, wxbc, wq, wk, wvdt, cw, cb, dtb_col)


def _ssd_chunk(xc_ref, zg_ref, dt_t, alog_ref, dskip_ref, nw_ref, y_ref, state_ref, lane, causal, tri_t):
    da_t = dt_t * (-LOG2E * jnp.exp(alog_ref[...]))
    hi = da_t.astype(BF16)
    rem = da_t - hi.astype(F32)
    mid = rem.astype(BF16)
    lo = (rem - mid.astype(F32)).astype(BF16)
    cum_t = _dot(hi, tri_t) + _dot(mid, tri_t) + _dot(lo, tri_t)
    yield
    last = cum_t[:, CHUNK - 1:CHUNK]
    wrow_t = dt_t * jnp.exp2(last - cum_t)
    cdec = jnp.exp2(last)
    cum = jnp.concatenate([cum_t, jnp.zeros((CHUNK - DT_ROWS, CHUNK), F32)], axis=0).T
    yield

    gn = SSD_GROUPS * SSD_STATE
    pairs_per_group = SSD_HEADS // SSD_GROUPS // 2
    y_pairs = []
    for g in range(SSD_GROUPS):
        bg = xc_ref[:, D_SSD + g * SSD_STATE:D_SSD + (g + 1) * SSD_STATE]
        cg = xc_ref[:, D_SSD + gn + g * SSD_STATE:D_SSD + gn + (g + 1) * SSD_STATE]
        cbm = _dot_nt(cg, bg)
        bgt = bg.astype(F32).T
        cg32 = cg.astype(F32)
        yield
        for jp in range(pairs_per_group):
            j = g * pairs_per_group + jp
            xs_pair = xc_ref[:, j * LANES:(j + 1) * LANES]
            yp = jnp.zeros((CHUNK, LANES), F32)
            for half in range(2):
                hd = 2 * j + half
                colb = jnp.broadcast_to(cum[:, hd:hd + 1], (CHUNK, LANES))
                dec = jnp.where(causal, jnp.exp2(colb - cum_t[hd:hd + 1, :]), 0.0)
                wp = (cbm * dec * dt_t[hd:hd + 1, :]).astype(BF16)
                gg = (cg32 * jnp.exp2(colb)).astype(BF16)
                in_half = (lane >= SSD_HEAD_DIM) if half else (lane < SSD_HEAD_DIM)
                xs_m = jnp.where(in_half, xs_pair, jnp.zeros_like(xs_pair))
                st = state_ref[hd]
                yp = yp + _dot(jnp.concatenate([wp, gg], axis=1),
                               jnp.concatenate([xs_m, st.astype(BF16)], axis=0))
                l2 = (bgt * wrow_t[hd:hd + 1, :]).astype(BF16)
                state_ref[hd] = st * cdec[hd:hd + 1, :] + _dot(l2, xs_m)
                yield
            y_pairs.append(yp)

    y = jnp.concatenate(y_pairs, axis=1) + dskip_ref[...] * xc_ref[:, :D_SSD].astype(F32)
    y = y * zg_ref[...].astype(F32)
    y_ref[...] = _rms(y, nw_ref[...]).astype(BF16)


def _ssd_kernel(xc_ref, zg_ref, *rest):
    dtt_refs = rest[:SSD_SEQS_PER_STEP]
    alog_ref, dskip_ref, nw_ref, y_ref, state_ref = rest[SSD_SEQS_PER_STEP:]
    @pl.when(pl.program_id(1) == 0)
    def _():
        state_ref[...] = jnp.zeros_like(state_ref)

    row = lax.broadcasted_iota(jnp.int32, (CHUNK, LANES), 0)
    lane = lax.broadcasted_iota(jnp.int32, (CHUNK, LANES), 1)
    causal = row >= lane
    tri_t = jnp.where(row <= lane, 1.0, 0.0).astype(BF16)
    chunks = [_ssd_chunk(xc_ref.at[0, p], zg_ref.at[0, p], dtt_refs[p][...], alog_ref, dskip_ref, nw_ref,
                         y_ref.at[0, p], state_ref.at[p], lane, causal, tri_t)
              for p in range(SSD_SEQS_PER_STEP)]
    for _ in itertools.zip_longest(*chunks):
        pass


def _ssd(xc, zg, dtt, alog_col, dskip, nw, B, S):
    nc = S // CHUNK
    nb = SSD_SEQS_PER_STEP
    seqs = lambda a: a.reshape(B // nb, nb, S, a.shape[-1])
    row = lambda n: pl.BlockSpec((1, nb, CHUNK, n), lambda b, c: (b, 0, c, 0))
    y = pl.pallas_call(
        _ssd_kernel,
        grid=(B // nb, nc),
        in_specs=[row(D_XBC), row(D_SSD)]
        + [pl.BlockSpec((DT_ROWS, CHUNK), functools.partial(lambda p, b, c: (0, (b * nb + p) * nc + c), p))
           for p in range(nb)]
        + [_const_spec(alog_col.shape), _const_spec(dskip.shape), _const_spec(nw.shape)],
        out_specs=row(D_SSD),
        out_shape=jax.ShapeDtypeStruct((B // nb, nb, S, D_SSD), BF16),
        scratch_shapes=[pltpu.VMEM((nb, SSD_HEADS, SSD_STATE, LANES), F32)],
        compiler_params=_params("parallel", "arbitrary"),
        name="ssd",
    )(seqs(xc), seqs(zg), *([dtt] * nb), alog_col, dskip, nw)
    return y.reshape(B * S, D_SSD)


def _diffattn_kernel(lam_ref, subw_ref, q_ref, k_ref, vt_ref, o_ref, sa_scr, sb_scr, acc_scr, m_scr, *, tq, tk, nq):
    hd = 2 * DIFF_HEAD_DIM
    ones = jnp.ones((ONES_ROWS, tk), BF16)
    krow = lax.broadcasted_iota(jnp.int32, (tk, tk), 0)
    qcol = lax.broadcasted_iota(jnp.int32, (tk, tk), 1)
    tri = krow <= qcol
    lv = lam_ref[...]
    lam = (jnp.exp(jnp.sum(lv[0:1] * lv[1:2], axis=-1, keepdims=True))
           - jnp.exp(jnp.sum(lv[2:3] * lv[3:4], axis=-1, keepdims=True)) + LAMBDA_INIT)

    steps = []
    for i in range(nq):
        steps += [(i, kb, "full") for kb in range(2 * i)] + [(i, 2 * i, "diag0"), (i, 2 * i + 1, "diag1")]

    def query_cols(kind):
        return (tk, tq) if kind == "diag1" else (0, tq)

    def scores(step, s_ref):
        i, kb, kind = step
        lo, hi = query_cols(kind)
        kblk = k_ref[kb * tk:(kb + 1) * tk, :]
        for c in range(2):
            s = _dot_nt(kblk, q_ref[i * tq + lo:i * tq + hi, c * LANES:(c + 1) * LANES])
            if kind == "diag0":
                s = jnp.concatenate([jnp.where(tri, s[:, :tk], -jnp.inf), s[:, tk:]], axis=1)
            elif kind == "diag1":
                s = jnp.where(tri, s, -jnp.inf)
            s_ref[c, :, lo:hi] = s

    def softmax_pv(step, s_ref):
        i, kb, kind = step
        lo, hi = query_cols(kind)
        par = i % 2
        lhs = jnp.concatenate([vt_ref[:, kb * tk:(kb + 1) * tk], ones], axis=0)
        for c in range(2):
            m_blk = jnp.max(s_ref[c, :, lo:hi], axis=0, keepdims=True)
            if kb == 0:
                m_new = m_blk
                acc_scr[par, c, :, lo:hi] = _dot(lhs, jnp.exp2(s_ref[c, :, lo:hi] - m_new).astype(BF16))
            else:
                m_old = m_scr[par, c, :, lo:hi]
                m_new = jnp.maximum(m_old, m_blk)
                alpha = jnp.exp2(m_old - m_new)
                p = jnp.exp2(s_ref[c, :, lo:hi] - m_new).astype(BF16)
                acc_scr[par, c, :, lo:hi] = alpha * acc_scr[par, c, :, lo:hi] + _dot(lhs, p)
            m_scr[par, c, :, lo:hi] = m_new

    def finish(i):
        par = i % 2
        a0 = acc_scr[par, 0]
        a1 = acc_scr[par, 1]
        ot = a0[:hd] * (1.0 / a0[hd:hd + 1]) - a1[:hd] * (lam / a1[hd:hd + 1])
        ms = jnp.mean(ot * ot, axis=0, keepdims=True)
        ot = ot * (lax.rsqrt(ms + NORM_EPS) * (1.0 - LAMBDA_INIT)) * subw_ref[...]
        o_ref[i * tq:(i + 1) * tq, :] = ot.T.astype(BF16)

    bufs = (sa_scr, sb_scr)
    scores(steps[0], bufs[0])
    for n, step in enumerate(steps):
        if n + 1 < len(steps):
            scores(steps[n + 1], bufs[(n + 1) % 2])
        softmax_pv(step, bufs[n % 2])
        if step[2] == "diag1":
            finish(step[0])


def _diffattn(lamv, subw_col, q, k, vt, B, S, tq):
    nq = S // tq
    tk = tq // 2
    hd = 2 * DIFF_HEAD_DIM
    return pl.pallas_call(
        functools.partial(_diffattn_kernel, tq=tq, tk=tk, nq=nq),
        grid=(B, DIFF_HEADS),
        in_specs=[_const_spec(lamv.shape), _const_spec(subw_col.shape),
                  pl.BlockSpec((S, 2 * hd), lambda b, h: (b, h)),
                  pl.BlockSpec((S, hd), lambda b, h: (b, h)),
                  pl.BlockSpec((hd, S), lambda b, h: (h, b))],
        out_specs=pl.BlockSpec((S, hd), lambda b, h: (b, h)),
        out_shape=jax.ShapeDtypeStruct((B * S, D_DIFF), BF16),
        scratch_shapes=[pltpu.VMEM((2, tk, tq), F32),
                        pltpu.VMEM((2, tk, tq), F32),
                        pltpu.VMEM((2, 2, hd + ONES_ROWS, tq), F32),
                        pltpu.VMEM((2, 2, 1, tq), F32)],
        compiler_params=_params("parallel", "parallel"),
        name="diffattn",
    )(lamv, subw_col, q, k, vt)


def _outproj_rows(rows, ys_ref, yd_ref, x_ref, wo_s_ref, wo_d_ref, gpost_ref, gq_ref, wmq_ref, x1_ref, qm_ref):
    mixed = _dot(ys_ref[rows, :], wo_s_ref[...]) + _dot(yd_ref[rows, :], wo_d_ref[...])
    yield
    x1 = x_ref[rows, :] + _rms(mixed, gpost_ref[...])
    x1_ref[rows, :] = x1
    hq = _rms(x1, gq_ref[...]).astype(BF16)
    qm = _dot(hq, wmq_ref[...])
    yield
    qm_ref[rows, :] = (qm * (MEM_HEAD_DIM ** -0.5)).astype(BF16)


def _outproj_kernel(*refs):
    tm = refs[0].shape[0]
    groups = [_outproj_rows(slice(r, r + ROW_GROUP), *refs) for r in range(0, tm, ROW_GROUP)]
    for _ in itertools.zip_longest(*groups):
        pass


def _outproj(ys, yd, x2, wo_s, wo_d, gpost, gq, wmq, tm):
    T = x2.shape[0]
    row = lambda n: pl.BlockSpec((tm, n), lambda i: (i, 0))
    return pl.pallas_call(
        _outproj_kernel,
        grid=(T // tm,),
        in_specs=[row(D_SSD), row(D_DIFF), row(D_MODEL), _const_spec(wo_s.shape), _const_spec(wo_d.shape),
                  _const_spec(gpost.shape), _const_spec(gq.shape), _const_spec(wmq.shape)],
        out_specs=[row(D_MODEL), row(D_MODEL)],
        out_shape=[jax.ShapeDtypeStruct((T, D_MODEL), F32), jax.ShapeDtypeStruct((T, D_MODEL), BF16)],
        compiler_params=_params("parallel"),
        name="outproj",
    )(ys, yd, x2, wo_s, wo_d, gpost, gq, wmq)


def _memkv_kernel(mem_ref, g_ref, wk_ref, wv_ref, k_ref, v_ref):
    h = _rms(mem_ref[...], g_ref[...]).astype(BF16)
    k_ref[...] = _dot(h, wk_ref[...]).astype(BF16)
    v_ref[...] = _dot(h, wv_ref[...]).astype(BF16)


def _memkv(mem2, g, wk, wv, M):
    R = mem2.shape[0]
    row = pl.BlockSpec((M, D_MODEL), lambda i: (i, 0))
    return pl.pallas_call(
        _memkv_kernel,
        grid=(R // M,),
        in_specs=[row, _const_spec(g.shape), _const_spec(wk.shape), _const_spec(wv.shape)],
        out_specs=[row, row],
        out_shape=[jax.ShapeDtypeStruct((R, D_MODEL), BF16)] * 2,
        compiler_params=_params("parallel"),
        name="memkv",
    )(mem2, g, wk, wv)


def _xattn_rows(rows, qm_ref, km_ref, vm_ref, x1_ref, wmo_ref, gpost_ref, x2_ref):
    outs = []
    for hd in range(MEM_HEADS):
        sl = slice(hd * MEM_HEAD_DIM, (hd + 1) * MEM_HEAD_DIM)
        s = _dot_nt(qm_ref[rows, sl], km_ref[:, sl])
        yield
        e = jnp.exp(s - jnp.max(s, axis=-1, keepdims=True))
        p = e / jnp.sum(e, axis=-1, keepdims=True)
        outs.append(_dot(p.astype(BF16), vm_ref[:, sl]).astype(BF16))
        yield
    c = _dot(jnp.concatenate(outs, axis=1), wmo_ref[...])
    yield
    x2_ref[rows, :] = x1_ref[rows, :] + _rms(c, gpost_ref[...])


def _xattn_kernel(*refs):
    tm = refs[0].shape[0]
    groups = [_xattn_rows(slice(r, r + ROW_GROUP), *refs) for r in range(0, tm, ROW_GROUP)]
    for _ in itertools.zip_longest(*groups):
        pass


def _xattn(qm, km, vm, x1, wmo, gpost, S, M, tm):
    T = x1.shape[0]
    per_b = S // tm
    row = lambda n: pl.BlockSpec((tm, n), lambda i: (i, 0))
    mem = pl.BlockSpec((M, D_MODEL), lambda i: (i // per_b, 0))
    return pl.pallas_call(
        _xattn_kernel,
        grid=(T // tm,),
        in_specs=[row(D_MODEL), mem, mem, row(D_MODEL), _const_spec(wmo.shape), _const_spec(gpost.shape)],
        out_specs=row(D_MODEL),
        out_shape=jax.ShapeDtypeStruct((T, D_MODEL), F32),
        compiler_params=_params("parallel"),
        name="xattn",
    )(qm, km, vm, x1, wmo, gpost)


def _mlp_rows(rows, tf, x_ref, gpre_ref, wup_ref, wdn_ref, gpost_ref, o_ref):
    x = x_ref[rows, :]
    h = _rms(x, gpre_ref[...]).astype(BF16)
    acc = jnp.zeros(x.shape, F32)
    for c in range(D_FF // tf):
        u = _dot(h, wup_ref[:, c * tf:(c + 1) * tf])
        yield
        u = jnp.maximum(u, 0.0)
        acc = acc + _dot((u * u).astype(BF16), wdn_ref[c * tf:(c + 1) * tf, :])
        yield
    o_ref[rows, :] = x + _rms(acc, gpost_ref[...])


def _mlp_kernel(*refs, tf):
    tm = refs[0].shape[0]
    groups = [_mlp_rows(slice(r, r + ROW_GROUP), tf, *refs) for r in range(0, tm, ROW_GROUP)]
    for _ in itertools.zip_longest(*groups):
        pass


def _mlp(x2, gpre, wup, wdn, gpost, tm, tf):
    T = x2.shape[0]
    row = pl.BlockSpec((tm, D_MODEL), lambda i: (i, 0))
    return pl.pallas_call(
        functools.partial(_mlp_kernel, tf=tf),
        grid=(T // tm,),
        in_specs=[row, _const_spec(gpre.shape), _const_spec(wup.shape), _const_spec(wdn.shape),
                  _const_spec(gpost.shape)],
        out_specs=row,
        out_shape=jax.ShapeDtypeStruct((T, D_MODEL), F32),
        compiler_params=_params("parallel"),
        name="mlp",
    )(x2, gpre, wup, wdn, gpost)


def _rope_tables(positions):
    inv = ROPE_THETA ** (-jnp.arange(0, DIFF_HEAD_DIM, 2, dtype=F32) / DIFF_HEAD_DIM)
    ang = positions.astype(F32).reshape(-1, 1) * inv
    return jnp.cos(ang), jnp.sin(ang)


def _head_lane_order(wcols):
    half = DIFF_HEAD_DIM // 2
    rows = wcols.shape[0]
    return wcols.reshape(rows, DIFF_HEADS, 2, 2, half).transpose(0, 1, 3, 2, 4).reshape(rows, D_DIFF)


def kernel(x, mem, positions, norm_mix_pre, norm_mix_post, norm_mem_q, norm_mem_kv, norm_mem_post,
           norm_mlp_pre, norm_mlp_post, w_in, conv_w, conv_b, dt_bias, a_log, d_skip, ssd_norm_w,
           lambda_q1, lambda_k1, lambda_q2, lambda_k2, subln_w, w_out, w_mq, w_mk, w_mv, w_mo, w_up, w_down):
    B, S, _ = x.shape
    M = mem.shape[1]
    T = B * S
    assert norm_mix_pre.shape[0] == 1, "single-layer trunk"
    x2 = x.reshape(T, D_MODEL)
    cos, sin = _rope_tables(positions)

    w = w_in[0]
    o0 = D_SSD
    o1 = o0 + D_XBC
    o2 = o1 + SSD_HEADS
    o3 = o2 + D_DIFF
    o4 = o3 + D_DIFF
    wz = w[:, :o0].astype(BF16)
    wxbc = w[:, o0:o1].astype(BF16)
    wq = _head_lane_order(w[:, o2:o3]).astype(BF16)
    wk = _head_lane_order(w[:, o3:o4]).astype(BF16)
    head_pad = ((0, DT_ROWS - SSD_HEADS), (0, 0))
    wvdt = jnp.concatenate([w[:, o4:].T, jnp.pad(w[:, o1:o2].T, head_pad)], axis=0).astype(BF16)
    head_col = lambda p: jnp.pad(p.reshape(SSD_HEADS, 1), head_pad)

    zg, xc, dtt, q, k, vt = _inproj(x2, norm_mix_pre, cos, sin, wz, wxbc, wq, wk, wvdt,
                                    conv_w[0, :, 0, :], conv_b, head_col(dt_bias), S, tm=1024)

    y_ssd = _ssd(xc, zg, dtt, head_col(a_log), jnp.repeat(d_skip, SSD_HEAD_DIM, axis=1), ssd_norm_w, B, S)

    lamv = jnp.concatenate([lambda_q1, lambda_k1, lambda_q2, lambda_k2], axis=0)
    y_diff = _diffattn(lamv, subln_w.reshape(2 * DIFF_HEAD_DIM, 1), q, k, vt, B, S, tq=512)

    wo = w_out[0].astype(BF16)
    x1, qm = _outproj(y_ssd, y_diff, x2, wo[:D_SSD], wo[D_SSD:], norm_mix_post, norm_mem_q,
                      w_mq[0].astype(BF16), tm=1024)
    km, vm = _memkv(mem.reshape(B * M, D_MODEL), norm_mem_kv, w_mk[0].astype(BF16), w_mv[0].astype(BF16), M)
    x2b = _xattn(qm, km, vm, x1, w_mo[0].astype(BF16), norm_mem_post, S, M, tm=1024)
    out = _mlp(x2b, norm_mlp_pre, w_up[0].astype(BF16), w_down[0].astype(BF16), norm_mlp_post, tm=1024, tf=1024)
    return out.reshape(B, S, D_MODEL)
```

```python
import functools
import itertools
import math

import jax
import jax.numpy as jnp
from jax import lax
from jax.experimental import pallas as pl
from jax.experimental.pallas import tpu as pltpu

F32 = jnp.float32
BF16 = jnp.bfloat16

D_MODEL = 1024
D_SSD = 512
SSD_HEAD_DIM = 64
SSD_HEADS = D_SSD // SSD_HEAD_DIM
SSD_GROUPS = 2
SSD_STATE = 128
CONV_WIDTH = 4
CHUNK = 128
D_XBC = D_SSD + 2 * SSD_GROUPS * SSD_STATE
D_DIFF = D_MODEL - D_SSD
DIFF_HEAD_DIM = 64
DIFF_HEADS = D_DIFF // (2 * DIFF_HEAD_DIM)
ROPE_THETA = 10000.0
MEM_HEADS = 4
MEM_HEAD_DIM = D_MODEL // MEM_HEADS
D_FF = 4 * D_MODEL
NORM_EPS = 1e-6
LAMBDA_INIT = 0.8 - 0.6 * math.exp(-0.3 * 0)

LOG2E = math.log2(math.e)
LANES = 128
SUBLANES = 8
ONES_ROWS = 16
DT_ROWS = 16
CONV_COLS = 256
SSD_SEQS_PER_STEP = 4
DIFF_HEADS_PER_STEP = 2
ROW_GROUP = 256
VMEM_LIMIT = 56 * 1024 * 1024


def _dot(a, b):
    return jnp.dot(a, b, preferred_element_type=F32)


def _dot_nt(a, b):
    return lax.dot_general(a, b, (((1,), (1,)), ((), ())), preferred_element_type=F32)


def _rms(x, g):
    ms = jnp.mean(x * x, axis=-1, keepdims=True)
    return x * lax.rsqrt(ms + NORM_EPS) * g


def _silu(a):
    half = 0.5 * a
    return half + half * jnp.tanh(half)


def _const_spec(shape):
    nd = len(shape)
    return pl.BlockSpec(shape, lambda *_: (0,) * nd)


def _params(*sem):
    return pltpu.CompilerParams(dimension_semantics=sem, vmem_limit_bytes=VMEM_LIMIT)


def _inproj_kernel(x_ref, g_ref, cos_ref, sin_ref, wz_ref, wxbc_ref, wq_ref, wk_ref, wvdt_ref,
                   cw_ref, cb_ref, dtb_ref,
                   zg_ref, xc_ref, dtt_ref, q_ref, k_ref, vt_ref, halo_ref, *, tm, tiles_per_seq):
    @pl.when(pl.program_id(0) % tiles_per_seq == 0)
    def _():
        halo_ref[...] = jnp.zeros_like(halo_ref)

    vregs = ROW_GROUP // SUBLANES
    sub = lax.broadcasted_iota(jnp.int32, (1, SUBLANES, 1), 1)
    reps = LANES // (DIFF_HEAD_DIM // 2)
    lane = lax.broadcasted_iota(jnp.int32, (ROW_GROUP, LANES), 1)
    comp0 = (lane % DIFF_HEAD_DIM) < DIFF_HEAD_DIM // 2
    history = {}

    def row_group(rows):
        h = _rms(x_ref[rows, :], g_ref[...]).astype(BF16)

        z = _dot(h, wz_ref[...])
        yield
        zg_ref[rows, :] = _silu(z).astype(BF16)

        for c in range(D_XBC // CONV_COLS):
            cols = slice(c * CONV_COLS, (c + 1) * CONV_COLS)
            xbc = _dot(h, wxbc_ref[:, cols])
            yield
            xb = xbc.reshape(vregs, SUBLANES, CONV_COLS)
            prev = history[c] if c in history else halo_ref[:, cols]
            acc = cb_ref[:, cols] + cw_ref[CONV_WIDTH - 1:CONV_WIDTH, cols] * xb
            for k in range(1, CONV_WIDTH):
                w = cw_ref[CONV_WIDTH - 1 - k:CONV_WIDTH - k, cols]
                r = pltpu.roll(xb * w, k, 1)
                r_first = pltpu.roll(prev * w, k, 0)[None]
                acc = acc + jnp.where(sub >= k, r, jnp.concatenate([r_first, r[:vregs - 1]], axis=0))
            history[c] = xb[vregs - 1]
            xc_ref[rows, cols] = _silu(acc.reshape(ROW_GROUP, CONV_COLS)).astype(BF16)

        r = _dot_nt(wvdt_ref[...], h)
        yield
        vt_ref[:, rows] = r[:D_DIFF].astype(BF16)
        dtt_ref[:, rows] = jax.nn.softplus(r[D_DIFF:] + dtb_ref[...])

        cos = jnp.concatenate([cos_ref[:, rows]] * reps, axis=0).T
        sin = jnp.concatenate([sin_ref[:, rows]] * reps, axis=0).T
        sin = jnp.where(lane < LANES // 2, -sin, sin)

        def rope(a, c):
            ac = a[:, c * LANES:(c + 1) * LANES]
            return ac * cos + pltpu.roll(ac, LANES // 2, 1) * sin

        a = _dot(h, wk_ref[...])
        yield
        for c in range(DIFF_HEADS):
            k_ref[rows, c * LANES:(c + 1) * LANES] = rope(a, c).astype(BF16)

        a = _dot(h, wq_ref[...])
        yield
        for c in range(DIFF_HEADS):
            qh = rope(a, c) * (DIFF_HEAD_DIM ** -0.5 * LOG2E)
            q_ref[rows, (2 * c) * LANES:(2 * c + 1) * LANES] = jnp.where(comp0, qh, 0.0).astype(BF16)
            q_ref[rows, (2 * c + 1) * LANES:(2 * c + 2) * LANES] = jnp.where(comp0, 0.0, qh).astype(BF16)

    for _ in itertools.zip_longest(*[row_group(slice(r, r + ROW_GROUP)) for r in range(0, tm, ROW_GROUP)]):
        pass
    for c, last_rows in history.items():
        halo_ref[:, c * CONV_COLS:(c + 1) * CONV_COLS] = last_rows


def _inproj(x2, g, cos, sin, wz, wxbc, wq, wk, wvdt, cw, cb, dtb_col, S, tm):
    T = x2.shape[0]
    row = lambda n: pl.BlockSpec((tm, n), lambda i: (i, 0))
    col = lambda n: pl.BlockSpec((n, tm), lambda i: (0, i))
    return pl.pallas_call(
        functools.partial(_inproj_kernel, tm=tm, tiles_per_seq=S // tm),
        grid=(T // tm,),
        in_specs=[row(D_MODEL), _const_spec((1, D_MODEL)), col(cos.shape[0]), col(sin.shape[0]),
                  _const_spec(wz.shape), _const_spec(wxbc.shape),
                  _const_spec(wq.shape), _const_spec(wk.shape), _const_spec(wvdt.shape),
                  _const_spec(cw.shape), _const_spec(cb.shape), _const_spec(dtb_col.shape)],
        out_specs=[row(D_SSD), row(D_XBC), col(DT_ROWS), row(2 * D_DIFF), row(D_DIFF), col(D_DIFF)],
        out_shape=[jax.ShapeDtypeStruct((T, D_SSD), BF16),
                   jax.ShapeDtypeStruct((T, D_XBC), BF16),
                   jax.ShapeDtypeStruct((DT_ROWS, T), F32),
                   jax.ShapeDtypeStruct((T, 2 * D_DIFF), BF16),
                   jax.ShapeDtypeStruct((T, D_DIFF), BF16),
                   jax.ShapeDtypeStruct((D_DIFF, T), BF16)],
        scratch_shapes=[pltpu.VMEM((SUBLANES, D_XBC), F32)],
        compiler_params=_params("arbitrary"),
        name="inproj",
    )(x2, g, cos, sin, wz, wxbc, wq, wk, wvdt, cw, cb, dtb_col)


def _ssd_chunk(xc_ref, zg_ref, dt_t, alog_ref, dskip_ref, nw_ref, y_ref, state_ref, lane, causal, tri_t):
    da_t = dt_t * (-LOG2E * jnp.exp(alog_ref[...]))
    hi = da_t.astype(BF16)
    rem = da_t - hi.astype(F32)
    mid = rem.astype(BF16)
    lo = (rem - mid.astype(F32)).astype(BF16)
    cum_t = _dot(hi, tri_t) + _dot(mid, tri_t) + _dot(lo, tri_t)
    yield
    last = cum_t[:, CHUNK - 1:CHUNK]
    wrow_t = dt_t * jnp.exp2(last - cum_t)
    cdec = jnp.exp2(last)
    cum = jnp.concatenate([cum_t, jnp.zeros((CHUNK - DT_ROWS, CHUNK), F32)], axis=0).T
    yield

    gn = SSD_GROUPS * SSD_STATE
    pairs_per_group = SSD_HEADS // SSD_GROUPS // 2
    y_pairs = []
    for g in range(SSD_GROUPS):
        bg = xc_ref[:, D_SSD + g * SSD_STATE:D_SSD + (g + 1) * SSD_STATE]
        cg = xc_ref[:, D_SSD + gn + g * SSD_STATE:D_SSD + gn + (g + 1) * SSD_STATE]
        cbm = _dot_nt(cg, bg)
        bgt = bg.astype(F32).T
        cg32 = cg.astype(F32)
        yield
        for jp in range(pairs_per_group):
            j = g * pairs_per_group + jp
            xs_pair = xc_ref[:, j * LANES:(j + 1) * LANES]
            yp = jnp.zeros((CHUNK, LANES), F32)
            for half in range(2):
                hd = 2 * j + half
                colb = jnp.broadcast_to(cum[:, hd:hd + 1], (CHUNK, LANES))
                dec = jnp.where(causal, jnp.exp2(colb - cum_t[hd:hd + 1, :]), 0.0)
                wp = (cbm * dec * dt_t[hd:hd + 1, :]).astype(BF16)
                gg = (cg32 * jnp.exp2(colb)).astype(BF16)
                in_half = (lane >= SSD_HEAD_DIM) if half else (lane < SSD_HEAD_DIM)
                xs_m = jnp.where(in_half, xs_pair, jnp.zeros_like(xs_pair))
                st = state_ref[hd]
                yp = yp + _dot(jnp.concatenate([wp, gg], axis=1),
                               jnp.concatenate([xs_m, st.astype(BF16)], axis=0))
                l2 = (bgt * wrow_t[hd:hd + 1, :]).astype(BF16)
                state_ref[hd] = st * cdec[hd:hd + 1, :] + _dot(l2, xs_m)
                yield
            y_pairs.append(yp)

    y = jnp.concatenate(y_pairs, axis=1) + dskip_ref[...] * xc_ref[:, :D_SSD].astype(F32)
    y = y * zg_ref[...].astype(F32)
    y_ref[...] = _rms(y, nw_ref[...]).astype(BF16)


def _ssd_kernel(xc_ref, zg_ref, *rest):
    dtt_refs = rest[:SSD_SEQS_PER_STEP]
    alog_ref, dskip_ref, nw_ref, y_ref, state_ref = rest[SSD_SEQS_PER_STEP:]
    @pl.when(pl.program_id(1) == 0)
    def _():
        state_ref[...] = jnp.zeros_like(state_ref)

    row = lax.broadcasted_iota(jnp.int32, (CHUNK, LANES), 0)
    lane = lax.broadcasted_iota(jnp.int32, (CHUNK, LANES), 1)
    causal = row >= lane
    tri_t = jnp.where(row <= lane, 1.0, 0.0).astype(BF16)
    chunks = [_ssd_chunk(xc_ref.at[0, p], zg_ref.at[0, p], dtt_refs[p][...], alog_ref, dskip_ref, nw_ref,
                         y_ref.at[0, p], state_ref.at[p], lane, causal, tri_t)
              for p in range(SSD_SEQS_PER_STEP)]
    for _ in itertools.zip_longest(*chunks):
        pass


def _ssd(xc, zg, dtt, alog_col, dskip, nw, B, S):
    nc = S // CHUNK
    nb = SSD_SEQS_PER_STEP
    seqs = lambda a: a.reshape(B // nb, nb, S, a.shape[-1])
    row = lambda n: pl.BlockSpec((1, nb, CHUNK, n), lambda b, c: (b, 0, c, 0))
    y = pl.pallas_call(
        _ssd_kernel,
        grid=(B // nb, nc),
        in_specs=[row(D_XBC), row(D_SSD)]
        + [pl.BlockSpec((DT_ROWS, CHUNK), functools.partial(lambda p, b, c: (0, (b * nb + p) * nc + c), p))
           for p in range(nb)]
        + [_const_spec(alog_col.shape), _const_spec(dskip.shape), _const_spec(nw.shape)],
        out_specs=row(D_SSD),
        out_shape=jax.ShapeDtypeStruct((B // nb, nb, S, D_SSD), BF16),
        scratch_shapes=[pltpu.VMEM((nb, SSD_HEADS, SSD_STATE, LANES), F32)],
        compiler_params=_params("parallel", "arbitrary"),
        name="ssd",
    )(seqs(xc), seqs(zg), *([dtt] * nb), alog_col, dskip, nw)
    return y.reshape(B * S, D_SSD)


def _diffattn_kernel(lam_ref, subw_ref, q_ref, k_ref, vt_ref, o_ref, sa_scr, sb_scr, acc_scr, m_scr, *,
                     tq, tk, nq, heads):
    hd = 2 * DIFF_HEAD_DIM
    ones = jnp.ones((ONES_ROWS, tk), BF16)
    krow = lax.broadcasted_iota(jnp.int32, (tk, tk), 0)
    qcol = lax.broadcasted_iota(jnp.int32, (tk, tk), 1)
    tri = krow <= qcol
    lv = lam_ref[...]
    lam = (jnp.exp(jnp.sum(lv[0:1] * lv[1:2], axis=-1, keepdims=True))
           - jnp.exp(jnp.sum(lv[2:3] * lv[3:4], axis=-1, keepdims=True)) + LAMBDA_INIT)

    steps = []
    for i in range(nq):
        steps += [(i, kb, "full") for kb in range(2 * i)] + [(i, 2 * i, "diag0"), (i, 2 * i + 1, "diag1")]

    def query_cols(kind):
        return (tk, tq) if kind == "diag1" else (0, tq)

    def head(hh):
        acc = acc_scr.at[hh]
        mx = m_scr.at[hh]
        bufs = (sa_scr.at[hh], sb_scr.at[hh])
        feat = slice(hh * hd, (hh + 1) * hd)

        def scores(step, s_ref):
            i, kb, kind = step
            lo, hi = query_cols(kind)
            kblk = k_ref[kb * tk:(kb + 1) * tk, feat]
            for c in range(2):
                s = _dot_nt(kblk, q_ref[i * tq + lo:i * tq + hi, (2 * hh + c) * LANES:(2 * hh + c + 1) * LANES])
                if kind == "diag0":
                    s = jnp.concatenate([jnp.where(tri, s[:, :tk], -jnp.inf), s[:, tk:]], axis=1)
                elif kind == "diag1":
                    s = jnp.where(tri, s, -jnp.inf)
                s_ref[c, :, lo:hi] = s

        def softmax_pv(step, s_ref):
            i, kb, kind = step
            lo, hi = query_cols(kind)
            par = i % 2
            lhs = jnp.concatenate([vt_ref[feat, kb * tk:(kb + 1) * tk], ones], axis=0)
            for c in range(2):
                m_blk = jnp.max(s_ref[c, :, lo:hi], axis=0, keepdims=True)
                if kb == 0:
                    m_new = m_blk
                    acc[par, c, :, lo:hi] = _dot(lhs, jnp.exp2(s_ref[c, :, lo:hi] - m_new).astype(BF16))
                else:
                    m_old = mx[par, c, :, lo:hi]
                    m_new = jnp.maximum(m_old, m_blk)
                    alpha = jnp.exp2(m_old - m_new)
                    p = jnp.exp2(s_ref[c, :, lo:hi] - m_new).astype(BF16)
                    acc[par, c, :, lo:hi] = alpha * acc[par, c, :, lo:hi] + _dot(lhs, p)
                mx[par, c, :, lo:hi] = m_new

        def finish(i):
            par = i % 2
            a0 = acc[par, 0]
            a1 = acc[par, 1]
            ot = a0[:hd] * (1.0 / a0[hd:hd + 1]) - a1[:hd] * (lam / a1[hd:hd + 1])
            ms = jnp.mean(ot * ot, axis=0, keepdims=True)
            ot = ot * (lax.rsqrt(ms + NORM_EPS) * (1.0 - LAMBDA_INIT)) * subw_ref[...]
            o_ref[i * tq:(i + 1) * tq, feat] = ot.T.astype(BF16)

        scores(steps[0], bufs[0])
        yield
        for n, step in enumerate(steps):
            if n + 1 < len(steps):
                scores(steps[n + 1], bufs[(n + 1) % 2])
                yield
            softmax_pv(step, bufs[n % 2])
            yield
            if step[2] == "diag1":
                finish(step[0])

    for _ in itertools.zip_longest(*[head(hh) for hh in range(heads)]):
        pass


def _diffattn(lamv, subw_col, q, k, vt, B, S, tq):
    nq = S // tq
    tk = tq // 2
    hd = 2 * DIFF_HEAD_DIM
    hp = DIFF_HEADS_PER_STEP
    return pl.pallas_call(
        functools.partial(_diffattn_kernel, tq=tq, tk=tk, nq=nq, heads=hp),
        grid=(B, DIFF_HEADS // hp),
        in_specs=[_const_spec(lamv.shape), _const_spec(subw_col.shape),
                  pl.BlockSpec((S, 2 * hd * hp), lambda b, h: (b, h)),
                  pl.BlockSpec((S, hd * hp), lambda b, h: (b, h)),
                  pl.BlockSpec((hd * hp, S), lambda b, h: (h, b))],
        out_specs=pl.BlockSpec((S, hd * hp), lambda b, h: (b, h)),
        out_shape=jax.ShapeDtypeStruct((B * S, D_DIFF), BF16),
        scratch_shapes=[pltpu.VMEM((hp, 2, tk, tq), F32),
                        pltpu.VMEM((hp, 2, tk, tq), F32),
                        pltpu.VMEM((hp, 2, 2, hd + ONES_ROWS, tq), F32),
                        pltpu.VMEM((hp, 2, 2, 1, tq), F32)],
        compiler_params=_params("parallel", "parallel"),
        name="diffattn",
    )(lamv, subw_col, q, k, vt)


def _outproj_rows(rows, ys_ref, yd_ref, x_ref, wo_s_ref, wo_d_ref, gpost_ref, gq_ref, wmq_ref, x1_ref, qm_ref):
    mixed = _dot(ys_ref[rows, :], wo_s_ref[...]) + _dot(yd_ref[rows, :], wo_d_ref[...])
    yield
    x1 = x_ref[rows, :] + _rms(mixed, gpost_ref[...])
    x1_ref[rows, :] = x1
    hq = _rms(x1, gq_ref[...]).astype(BF16)
    qm = _dot(hq, wmq_ref[...])
    yield
    qm_ref[rows, :] = (qm * (MEM_HEAD_DIM ** -0.5)).astype(BF16)


def _outproj_kernel(*refs):
    tm = refs[0].shape[0]
    groups = [_outproj_rows(slice(r, r + ROW_GROUP), *refs) for r in range(0, tm, ROW_GROUP)]
    for _ in itertools.zip_longest(*groups):
        pass


def _outproj(ys, yd, x2, wo_s, wo_d, gpost, gq, wmq, tm):
    T = x2.shape[0]
    row = lambda n: pl.BlockSpec((tm, n), lambda i: (i, 0))
    return pl.pallas_call(
        _outproj_kernel,
        grid=(T // tm,),
        in_specs=[row(D_SSD), row(D_DIFF), row(D_MODEL), _const_spec(wo_s.shape), _const_spec(wo_d.shape),
                  _const_spec(gpost.shape), _const_spec(gq.shape), _const_spec(wmq.shape)],
        out_specs=[row(D_MODEL), row(D_MODEL)],
        out_shape=[jax.ShapeDtypeStruct((T, D_MODEL), F32), jax.ShapeDtypeStruct((T, D_MODEL), BF16)],
        compiler_params=_params("parallel"),
        name="outproj",
    )(ys, yd, x2, wo_s, wo_d, gpost, gq, wmq)


def _memkv_kernel(mem_ref, g_ref, wk_ref, wv_ref, k_ref, v_ref):
    h = _rms(mem_ref[...], g_ref[...]).astype(BF16)
    k_ref[...] = _dot(h, wk_ref[...]).astype(BF16)
    v_ref[...] = _dot(h, wv_ref[...]).astype(BF16)


def _memkv(mem2, g, wk, wv, M):
    R = mem2.shape[0]
    row = pl.BlockSpec((M, D_MODEL), lambda i: (i, 0))
    return pl.pallas_call(
        _memkv_kernel,
        grid=(R // M,),
        in_specs=[row, _const_spec(g.shape), _const_spec(wk.shape), _const_spec(wv.shape)],
        out_specs=[row, row],
        out_shape=[jax.ShapeDtypeStruct((R, D_MODEL), BF16)] * 2,
        compiler_params=_params("parallel"),
        name="memkv",
    )(mem2, g, wk, wv)


def _xattn_rows(rows, qm_ref, km_ref, vm_ref, x1_ref, wmo_ref, gpost_ref, x2_ref):
    outs = []
    for hd in range(MEM_HEADS):
        sl = slice(hd * MEM_HEAD_DIM, (hd + 1) * MEM_HEAD_DIM)
        s = _dot_nt(qm_ref[rows, sl], km_ref[:, sl])
        yield
        e = jnp.exp(s - jnp.max(s, axis=-1, keepdims=True))
        p = e / jnp.sum(e, axis=-1, keepdims=True)
        outs.append(_dot(p.astype(BF16), vm_ref[:, sl]).astype(BF16))
        yield
    c = _dot(jnp.concatenate(outs, axis=1), wmo_ref[...])
    yield
    x2_ref[rows, :] = x1_ref[rows, :] + _rms(c, gpost_ref[...])


def _xattn_kernel(*refs):
    tm = refs[0].shape[0]
    groups = [_xattn_rows(slice(r, r + ROW_GROUP), *refs) for r in range(0, tm, ROW_GROUP)]
    for _ in itertools.zip_longest(*groups):
        pass


def _xattn(qm, km, vm, x1, wmo, gpost, S, M, tm):
    T = x1.shape[0]
    per_b = S // tm
    row = lambda n: pl.BlockSpec((tm, n), lambda i: (i, 0))
    mem = pl.BlockSpec((M, D_MODEL), lambda i: (i // per_b, 0))
    return pl.pallas_call(
        _xattn_kernel,
        grid=(T // tm,),
        in_specs=[row(D_MODEL), mem, mem, row(D_MODEL), _const_spec(wmo.shape), _const_spec(gpost.shape)],
        out_specs=row(D_MODEL),
        out_shape=jax.ShapeDtypeStruct((T, D_MODEL), F32),
        compiler_params=_params("parallel"),
        name="xattn",
    )(qm, km, vm, x1, wmo, gpost)


def _mlp_rows(rows, tf, x_ref, gpre_ref, wup_ref, wdn_ref, gpost_ref, o_ref):
    x = x_ref[rows, :]
    h = _rms(x, gpre_ref[...]).astype(BF16)
    acc = jnp.zeros(x.shape, F32)
    for c in range(D_FF // tf):
        u = _dot(h, wup_ref[:, c * tf:(c + 1) * tf])
        yield
        u = jnp.maximum(u, 0.0)
        acc = acc + _dot((u * u).astype(BF16), wdn_ref[c * tf:(c + 1) * tf, :])
        yield
    o_ref[rows, :] = x + _rms(acc, gpost_ref[...])


def _mlp_kernel(*refs, tf):
    tm = refs[0].shape[0]
    groups = [_mlp_rows(slice(r, r + ROW_GROUP), tf, *refs) for r in range(0, tm, ROW_GROUP)]
    for _ in itertools.zip_longest(*groups):
        pass


def _mlp(x2, gpre, wup, wdn, gpost, tm, tf):
    T = x2.shape[0]
    row = pl.BlockSpec((tm, D_MODEL), lambda i: (i, 0))
    return pl.pallas_call(
        functools.partial(_mlp_kernel, tf=tf),
        grid=(T // tm,),
        in_specs=[row, _const_spec(gpre.shape), _const_spec(wup.shape), _const_spec(wdn.shape),
                  _const_spec(gpost.shape)],
        out_specs=row,
        out_shape=jax.ShapeDtypeStruct((T, D_MODEL), F32),
        compiler_params=_params("parallel"),
        name="mlp",
    )(x2, gpre, wup, wdn, gpost)


def _rope_tables(positions):
    inv = ROPE_THETA ** (-jnp.arange(0, DIFF_HEAD_DIM, 2, dtype=F32) / DIFF_HEAD_DIM)
    ang = inv.reshape(-1, 1) * positions.astype(F32).reshape(1, -1)
    return jnp.cos(ang), jnp.sin(ang)


def _head_lane_order(wcols):
    half = DIFF_HEAD_DIM // 2
    rows = wcols.shape[0]
    return wcols.reshape(rows, DIFF_HEADS, 2, 2, half).transpose(0, 1, 3, 2, 4).reshape(rows, D_DIFF)


def kernel(x, mem, positions, norm_mix_pre, norm_mix_post, norm_mem_q, norm_mem_kv, norm_mem_post,
           norm_mlp_pre, norm_mlp_post, w_in, conv_w, conv_b, dt_bias, a_log, d_skip, ssd_norm_w,
           lambda_q1, lambda_k1, lambda_q2, lambda_k2, subln_w, w_out, w_mq, w_mk, w_mv, w_mo, w_up, w_down):
    B, S, _ = x.shape
    M = mem.shape[1]
    T = B * S
    assert norm_mix_pre.shape[0] == 1, "single-layer trunk"
    x2 = x.reshape(T, D_MODEL)
    cos, sin = _rope_tables(positions)

    w = w_in[0]
    o0 = D_SSD
    o1 = o0 + D_XBC
    o2 = o1 + SSD_HEADS
    o3 = o2 + D_DIFF
    o4 = o3 + D_DIFF
    wz = w[:, :o0].astype(BF16)
    wxbc = w[:, o0:o1].astype(BF16)
    wq = _head_lane_order(w[:, o2:o3]).astype(BF16)
    wk = _head_lane_order(w[:, o3:o4]).astype(BF16)
    head_pad = ((0, DT_ROWS - SSD_HEADS), (0, 0))
    wvdt = jnp.concatenate([w[:, o4:].T, jnp.pad(w[:, o1:o2].T, head_pad)], axis=0).astype(BF16)
    head_col = lambda p: jnp.pad(p.reshape(SSD_HEADS, 1), head_pad)

    zg, xc, dtt, q, k, vt = _inproj(x2, norm_mix_pre, cos, sin, wz, wxbc, wq, wk, wvdt,
                                    conv_w[0, :, 0, :], conv_b, head_col(dt_bias), S, tm=1024)

    y_ssd = _ssd(xc, zg, dtt, head_col(a_log), jnp.repeat(d_skip, SSD_HEAD_DIM, axis=1), ssd_norm_w, B, S)

    lamv = jnp.concatenate([lambda_q1, lambda_k1, lambda_q2, lambda_k2], axis=0)
    y_diff = _diffattn(lamv, subln_w.reshape(2 * DIFF_HEAD_DIM, 1), q, k, vt, B, S, tq=512)

    wo = w_out[0].astype(BF16)
    x1, qm = _outproj(y_ssd, y_diff, x2, wo[:D_SSD], wo[D_SSD:], norm_mix_post, norm_mem_q,
                      w_mq[0].astype(BF16), tm=1024)
    km, vm = _memkv(mem.reshape(B * M, D_MODEL), norm_mem_kv, w_mk[0].astype(BF16), w_mv[0].astype(BF16), M)
    x2b = _xattn(qm, km, vm, x1, w_mo[0].astype(BF16), norm_mem_post, S, M, tm=1024)
    out = _mlp(x2b, norm_mlp_pre, w_up[0].astype(BF16), w_down[0].astype(BF16), norm_mlp_post, tm=1024, tf=1024)
    return out.reshape(B, S, D_MODEL)
```

```python
import functools
import itertools
import math

import jax
import jax.numpy as jnp
from jax import lax
from jax.experimental import pallas as pl
from jax.experimental.pallas import tpu as pltpu

F32 = jnp.float32
BF16 = jnp.bfloat16

D_MODEL = 1024
D_SSD = 512
SSD_HEAD_DIM = 64
SSD_HEADS = D_SSD // SSD_HEAD_DIM
SSD_GROUPS = 2
SSD_STATE = 128
CONV_WIDTH = 4
CHUNK = 128
D_XBC = D_SSD + 2 * SSD_GROUPS * SSD_STATE
D_DIFF = D_MODEL - D_SSD
DIFF_HEAD_DIM = 64
DIFF_HEADS = D_DIFF // (2 * DIFF_HEAD_DIM)
ROPE_THETA = 10000.0
MEM_HEADS = 4
MEM_HEAD_DIM = D_MODEL // MEM_HEADS
D_FF = 4 * D_MODEL
NORM_EPS = 1e-6
LAMBDA_INIT = 0.8 - 0.6 * math.exp(-0.3 * 0)

LOG2E = math.log2(math.e)
LANES = 128
SUBLANES = 8
ONES_ROWS = 16
DT_ROWS = 16
CONV_COLS = 256
SSD_SEQS_PER_STEP = 4
DIFF_HEADS_PER_STEP = 2
ROW_GROUP = 256
VMEM_LIMIT = 56 * 1024 * 1024


def _dot(a, b):
    return jnp.dot(a, b, preferred_element_type=F32)


def _dot_nt(a, b):
    return lax.dot_general(a, b, (((1,), (1,)), ((), ())), preferred_element_type=F32)


def _rms(x, g):
    ms = jnp.mean(x * x, axis=-1, keepdims=True)
    return x * lax.rsqrt(ms + NORM_EPS) * g


def _silu(a):
    half = 0.5 * a
    return half + half * jnp.tanh(half)


def _const_spec(shape):
    nd = len(shape)
    return pl.BlockSpec(shape, lambda *_: (0,) * nd, pipeline_mode=pl.Buffered(1))


def _params(*sem):
    return pltpu.CompilerParams(dimension_semantics=sem, vmem_limit_bytes=VMEM_LIMIT)


def _inproj_kernel(x_ref, g_ref, cos_ref, sin_ref, wz_ref, wxbc_ref, wq_ref, wk_ref, wvdt_ref,
                   cw_ref, cb_ref, dtb_ref,
                   zg_ref, xc_ref, dtt_ref, q_ref, k_ref, vt_ref, halo_ref, *, tm, tiles_per_seq):
    @pl.when(pl.program_id(0) % tiles_per_seq == 0)
    def _():
        halo_ref[...] = jnp.zeros_like(halo_ref)

    vregs = ROW_GROUP // SUBLANES
    sub = lax.broadcasted_iota(jnp.int32, (1, SUBLANES, 1), 1)
    reps = LANES // (DIFF_HEAD_DIM // 2)
    lane = lax.broadcasted_iota(jnp.int32, (ROW_GROUP, LANES), 1)
    comp0 = (lane % DIFF_HEAD_DIM) < DIFF_HEAD_DIM // 2
    history = {}

    def row_group(rows):
        h = _rms(x_ref[rows, :], g_ref[...]).astype(BF16)

        def gate(z):
            zg_ref[rows, :] = _silu(z).astype(BF16)

        def conv(c, xbc):
            cols = slice(c * CONV_COLS, (c + 1) * CONV_COLS)
            xb = xbc.reshape(vregs, SUBLANES, CONV_COLS)
            prev = history[c] if c in history else halo_ref[:, cols]
            acc = cb_ref[:, cols] + cw_ref[CONV_WIDTH - 1:CONV_WIDTH, cols] * xb
            for k in range(1, CONV_WIDTH):
                w = cw_ref[CONV_WIDTH - 1 - k:CONV_WIDTH - k, cols]
                r = pltpu.roll(xb * w, k, 1)
                r_first = pltpu.roll(prev * w, k, 0)[None]
                acc = acc + jnp.where(sub >= k, r, jnp.concatenate([r_first, r[:vregs - 1]], axis=0))
            history[c] = xb[vregs - 1]
            xc_ref[rows, cols] = _silu(acc.reshape(ROW_GROUP, CONV_COLS)).astype(BF16)

        def values_and_steps(r):
            vt_ref[:, rows] = r[:D_DIFF].astype(BF16)
            dtt_ref[:, rows] = jax.nn.softplus(r[D_DIFF:] + dtb_ref[...])

        tables = []

        def rope(a, c):
            if not tables:
                cos = jnp.concatenate([cos_ref[:, rows]] * reps, axis=0).T
                sin = jnp.concatenate([sin_ref[:, rows]] * reps, axis=0).T
                tables.extend([cos, jnp.where(lane < LANES // 2, -sin, sin)])
            ac = a[:, c * LANES:(c + 1) * LANES]
            return ac * tables[0] + pltpu.roll(ac, LANES // 2, 1) * tables[1]

        def keys(a):
            for c in range(DIFF_HEADS):
                k_ref[rows, c * LANES:(c + 1) * LANES] = rope(a, c).astype(BF16)

        def queries(a):
            for c in range(DIFF_HEADS):
                qh = rope(a, c) * (DIFF_HEAD_DIM ** -0.5 * LOG2E)
                q_ref[rows, (2 * c) * LANES:(2 * c + 1) * LANES] = jnp.where(comp0, qh, 0.0).astype(BF16)
                q_ref[rows, (2 * c + 1) * LANES:(2 * c + 2) * LANES] = jnp.where(comp0, 0.0, qh).astype(BF16)

        light = [(lambda: _dot_nt(wvdt_ref[...], h), values_and_steps), (lambda: _dot(h, wz_ref[...]), gate),
                 (lambda: _dot(h, wk_ref[...]), keys), (lambda: _dot(h, wq_ref[...]), queries)]
        for c in range(D_XBC // CONV_COLS):
            for matmul, epilogue in ((functools.partial(lambda c: _dot(h, wxbc_ref[:, c * CONV_COLS:(c + 1) * CONV_COLS]), c),
                                      functools.partial(conv, c)), light[c]):
                result = matmul()
                yield
                epilogue(result)

    for _ in itertools.zip_longest(*[row_group(slice(r, r + ROW_GROUP)) for r in range(0, tm, ROW_GROUP)]):
        pass
    for c, last_rows in history.items():
        halo_ref[:, c * CONV_COLS:(c + 1) * CONV_COLS] = last_rows


def _inproj(x2, g, cos, sin, wz, wxbc, wq, wk, wvdt, cw, cb, dtb_col, S, tm):
    T = x2.shape[0]
    row = lambda n: pl.BlockSpec((tm, n), lambda i: (i, 0))
    col = lambda n: pl.BlockSpec((n, tm), lambda i: (0, i))
    return pl.pallas_call(
        functools.partial(_inproj_kernel, tm=tm, tiles_per_seq=S // tm),
        grid=(T // tm,),
        in_specs=[row(D_MODEL), _const_spec((1, D_MODEL)), col(cos.shape[0]), col(sin.shape[0]),
                  _const_spec(wz.shape), _const_spec(wxbc.shape),
                  _const_spec(wq.shape), _const_spec(wk.shape), _const_spec(wvdt.shape),
                  _const_spec(cw.shape), _const_spec(cb.shape), _const_spec(dtb_col.shape)],
        out_specs=[row(D_SSD), row(D_XBC), col(DT_ROWS), row(2 * D_DIFF), row(D_DIFF), col(D_DIFF)],
        out_shape=[jax.ShapeDtypeStruct((T, D_SSD), BF16),
                   jax.ShapeDtypeStruct((T, D_XBC), BF16),
                   jax.ShapeDtypeStruct((DT_ROWS, T), F32),
                   jax.ShapeDtypeStruct((T, 2 * D_DIFF), BF16),
                   jax.ShapeDtypeStruct((T, D_DIFF), BF16),
                   jax.ShapeDtypeStruct((D_DIFF, T), BF16)],
        scratch_shapes=[pltpu.VMEM((SUBLANES, D_XBC), F32)],
        compiler_params=_params("arbitrary"),
        name="inproj",
    )(x2, g, cos, sin, wz, wxbc, wq, wk, wvdt, cw, cb, dtb_col)


def _ssd_chunk(xc_ref, zg_ref, dt_t, alog_ref, dskip_ref, nw_ref, y_ref, state_ref, lane, causal, tri_t):
    da_t = dt_t * (-LOG2E * jnp.exp(alog_ref[...]))
    hi = da_t.astype(BF16)
    rem = da_t - hi.astype(F32)
    mid = rem.astype(BF16)
    lo = (rem - mid.astype(F32)).astype(BF16)
    cum_t = _dot(hi, tri_t) + _dot(mid, tri_t) + _dot(lo, tri_t)
    yield
    last = cum_t[:, CHUNK - 1:CHUNK]
    wrow_t = dt_t * jnp.exp2(last - cum_t)
    cdec = jnp.exp2(last)
    cum = jnp.concatenate([cum_t, jnp.zeros((CHUNK - DT_ROWS, CHUNK), F32)], axis=0).T
    yield

    gn = SSD_GROUPS * SSD_STATE
    pairs_per_group = SSD_HEADS // SSD_GROUPS // 2
    y_pairs = []
    for g in range(SSD_GROUPS):
        bg = xc_ref[:, D_SSD + g * SSD_STATE:D_SSD + (g + 1) * SSD_STATE]
        cg = xc_ref[:, D_SSD + gn + g * SSD_STATE:D_SSD + gn + (g + 1) * SSD_STATE]
        cbm = _dot_nt(cg, bg)
        bgt = bg.astype(F32).T
        cg32 = cg.astype(F32)
        yield
        for jp in range(pairs_per_group):
            j = g * pairs_per_group + jp
            xs_pair = xc_ref[:, j * LANES:(j + 1) * LANES]
            yp = jnp.zeros((CHUNK, LANES), F32)
            for half in range(2):
                hd = 2 * j + half
                colb = jnp.broadcast_to(cum[:, hd:hd + 1], (CHUNK, LANES))
                dec = jnp.where(causal, jnp.exp2(colb - cum_t[hd:hd + 1, :]), 0.0)
                wp = (cbm * dec * dt_t[hd:hd + 1, :]).astype(BF16)
                gg = (cg32 * jnp.exp2(colb)).astype(BF16)
                in_half = (lane >= SSD_HEAD_DIM) if half else (lane < SSD_HEAD_DIM)
                xs_m = jnp.where(in_half, xs_pair, jnp.zeros_like(xs_pair))
                st = state_ref[hd]
                yp = yp + _dot(jnp.concatenate([wp, gg], axis=1),
                               jnp.concatenate([xs_m, st.astype(BF16)], axis=0))
                l2 = (bgt * wrow_t[hd:hd + 1, :]).astype(BF16)
                state_ref[hd] = st * cdec[hd:hd + 1, :] + _dot(l2, xs_m)
                yield
            y_pairs.append(yp)

    y = jnp.concatenate(y_pairs, axis=1) + dskip_ref[...] * xc_ref[:, :D_SSD].astype(F32)
    y = y * zg_ref[...].astype(F32)
    y_ref[...] = _rms(y, nw_ref[...]).astype(BF16)


def _ssd_kernel(xc_ref, zg_ref, *rest):
    dtt_refs = rest[:SSD_SEQS_PER_STEP]
    alog_ref, dskip_ref, nw_ref, y_ref, state_ref = rest[SSD_SEQS_PER_STEP:]
    @pl.when(pl.program_id(1) == 0)
    def _():
        state_ref[...] = jnp.zeros_like(state_ref)

    row = lax.broadcasted_iota(jnp.int32, (CHUNK, LANES), 0)
    lane = lax.broadcasted_iota(jnp.int32, (CHUNK, LANES), 1)
    causal = row >= lane
    tri_t = jnp.where(row <= lane, 1.0, 0.0).astype(BF16)
    chunks = [_ssd_chunk(xc_ref.at[0, p], zg_ref.at[0, p], dtt_refs[p][...], alog_ref, dskip_ref, nw_ref,
                         y_ref.at[0, p], state_ref.at[p], lane, causal, tri_t)
              for p in range(SSD_SEQS_PER_STEP)]
    for _ in itertools.zip_longest(*chunks):
        pass


def _ssd(xc, zg, dtt, alog_col, dskip, nw, B, S):
    nc = S // CHUNK
    nb = SSD_SEQS_PER_STEP
    seqs = lambda a: a.reshape(B // nb, nb, S, a.shape[-1])
    row = lambda n: pl.BlockSpec((1, nb, CHUNK, n), lambda b, c: (b, 0, c, 0))
    y = pl.pallas_call(
        _ssd_kernel,
        grid=(B // nb, nc),
        in_specs=[row(D_XBC), row(D_SSD)]
        + [pl.BlockSpec((DT_ROWS, CHUNK), functools.partial(lambda p, b, c: (0, (b * nb + p) * nc + c), p))
           for p in range(nb)]
        + [_const_spec(alog_col.shape), _const_spec(dskip.shape), _const_spec(nw.shape)],
        out_specs=row(D_SSD),
        out_shape=jax.ShapeDtypeStruct((B // nb, nb, S, D_SSD), BF16),
        scratch_shapes=[pltpu.VMEM((nb, SSD_HEADS, SSD_STATE, LANES), F32)],
        compiler_params=_params("parallel", "arbitrary"),
        name="ssd",
    )(seqs(xc), seqs(zg), *([dtt] * nb), alog_col, dskip, nw)
    return y.reshape(B * S, D_SSD)


def _diffattn_kernel(lam_ref, subw_ref, q_ref, k_ref, vt_ref, o_ref, sa_scr, sb_scr, acc_scr, m_scr, *,
                     tq, tk, nq, heads):
    hd = 2 * DIFF_HEAD_DIM
    ones = jnp.ones((ONES_ROWS, tk), BF16)
    krow = lax.broadcasted_iota(jnp.int32, (tk, tk), 0)
    qcol = lax.broadcasted_iota(jnp.int32, (tk, tk), 1)
    tri = krow <= qcol
    lv = lam_ref[...]
    lam = (jnp.exp(jnp.sum(lv[0:1] * lv[1:2], axis=-1, keepdims=True))
           - jnp.exp(jnp.sum(lv[2:3] * lv[3:4], axis=-1, keepdims=True)) + LAMBDA_INIT)

    steps = []
    for i in range(nq):
        steps += [(i, kb, "full") for kb in range(2 * i)] + [(i, 2 * i, "diag0"), (i, 2 * i + 1, "diag1")]

    def query_cols(kind):
        return (tk, tq) if kind == "diag1" else (0, tq)

    def head(hh):
        acc = acc_scr.at[hh]
        mx = m_scr.at[hh]
        bufs = (sa_scr.at[hh], sb_scr.at[hh])
        feat = slice(hh * hd, (hh + 1) * hd)

        def scores(step, s_ref):
            i, kb, kind = step
            lo, hi = query_cols(kind)
            kblk = k_ref[kb * tk:(kb + 1) * tk, feat]
            for c in range(2):
                s = _dot_nt(kblk, q_ref[i * tq + lo:i * tq + hi, (2 * hh + c) * LANES:(2 * hh + c + 1) * LANES])
                if kind == "diag0":
                    s = jnp.concatenate([jnp.where(tri, s[:, :tk], -jnp.inf), s[:, tk:]], axis=1)
                elif kind == "diag1":
                    s = jnp.where(tri, s, -jnp.inf)
                s_ref[c, :, lo:hi] = s

        def softmax_pv(step, s_ref):
            i, kb, kind = step
            lo, hi = query_cols(kind)
            par = i % 2
            lhs = jnp.concatenate([vt_ref[feat, kb * tk:(kb + 1) * tk], ones], axis=0)
            for c in range(2):
                m_blk = jnp.max(s_ref[c, :, lo:hi], axis=0, keepdims=True)
                if kb == 0:
                    m_new = m_blk
                    acc[par, c, :, lo:hi] = _dot(lhs, jnp.exp2(s_ref[c, :, lo:hi] - m_new).astype(BF16))
                else:
                    m_old = mx[par, c, :, lo:hi]
                    m_new = jnp.maximum(m_old, m_blk)
                    alpha = jnp.exp2(m_old - m_new)
                    p = jnp.exp2(s_ref[c, :, lo:hi] - m_new).astype(BF16)
                    acc[par, c, :, lo:hi] = alpha * acc[par, c, :, lo:hi] + _dot(lhs, p)
                mx[par, c, :, lo:hi] = m_new

        def finish(i):
            par = i % 2
            a0 = acc[par, 0]
            a1 = acc[par, 1]
            ot = a0[:hd] * (1.0 / a0[hd:hd + 1]) - a1[:hd] * (lam / a1[hd:hd + 1])
            ms = jnp.mean(ot * ot, axis=0, keepdims=True)
            ot = ot * (lax.rsqrt(ms + NORM_EPS) * (1.0 - LAMBDA_INIT)) * subw_ref[...]
            o_ref[i * tq:(i + 1) * tq, feat] = ot.T.astype(BF16)

        scores(steps[0], bufs[0])
        yield
        for n, step in enumerate(steps):
            if n + 1 < len(steps):
                scores(steps[n + 1], bufs[(n + 1) % 2])
                yield
            softmax_pv(step, bufs[n % 2])
            yield
            if step[2] == "diag1":
                finish(step[0])

    for _ in itertools.zip_longest(*[head(hh) for hh in range(heads)]):
        pass


def _diffattn(lamv, subw_col, q, k, vt, B, S, tq):
    nq = S // tq
    tk = tq // 2
    hd = 2 * DIFF_HEAD_DIM
    hp = DIFF_HEADS_PER_STEP
    return pl.pallas_call(
        functools.partial(_diffattn_kernel, tq=tq, tk=tk, nq=nq, heads=hp),
        grid=(B, DIFF_HEADS // hp),
        in_specs=[_const_spec(lamv.shape), _const_spec(subw_col.shape),
                  pl.BlockSpec((S, 2 * hd * hp), lambda b, h: (b, h)),
                  pl.BlockSpec((S, hd * hp), lambda b, h: (b, h)),
                  pl.BlockSpec((hd * hp, S), lambda b, h: (h, b))],
        out_specs=pl.BlockSpec((S, hd * hp), lambda b, h: (b, h)),
        out_shape=jax.ShapeDtypeStruct((B * S, D_DIFF), BF16),
        scratch_shapes=[pltpu.VMEM((hp, 2, tk, tq), F32),
                        pltpu.VMEM((hp, 2, tk, tq), F32),
                        pltpu.VMEM((hp, 2, 2, hd + ONES_ROWS, tq), F32),
                        pltpu.VMEM((hp, 2, 2, 1, tq), F32)],
        compiler_params=_params("parallel", "parallel"),
        name="diffattn",
    )(lamv, subw_col, q, k, vt)


def _memkv_kernel(mem_ref, g_ref, wk_ref, wv_ref, k_ref, v_ref):
    h = _rms(mem_ref[...], g_ref[...]).astype(BF16)
    k_ref[...] = _dot(h, wk_ref[...]).astype(BF16)
    v_ref[...] = _dot(h, wv_ref[...]).astype(BF16)


def _memkv(mem2, g, wk, wv, M):
    R = mem2.shape[0]
    row = pl.BlockSpec((M, D_MODEL), lambda i: (i, 0))
    return pl.pallas_call(
        _memkv_kernel,
        grid=(R // M,),
        in_specs=[row, _const_spec(g.shape), _const_spec(wk.shape), _const_spec(wv.shape)],
        out_specs=[row, row],
        out_shape=[jax.ShapeDtypeStruct((R, D_MODEL), BF16)] * 2,
        compiler_params=_params("parallel"),
        name="memkv",
    )(mem2, g, wk, wv)


def _mixmem_rows(rows, ys_ref, yd_ref, x_ref, km_ref, vm_ref, wo_s_ref, wo_d_ref, gmix_ref, gq_ref, wmq_ref,
                 wmo_ref, gmem_ref, x2_ref):
    mixed = _dot(ys_ref[rows, :], wo_s_ref[...]) + _dot(yd_ref[rows, :], wo_d_ref[...])
    yield
    x1 = x_ref[rows, :] + _rms(mixed, gmix_ref[...])
    qm = _dot(_rms(x1, gq_ref[...]).astype(BF16), wmq_ref[...])
    yield
    qm = (qm * (MEM_HEAD_DIM ** -0.5)).astype(BF16)
    outs = []
    for hd in range(MEM_HEADS):
        sl = slice(hd * MEM_HEAD_DIM, (hd + 1) * MEM_HEAD_DIM)
        s = _dot_nt(qm[:, sl], km_ref[:, sl])
        yield
        e = jnp.exp(s - jnp.max(s, axis=-1, keepdims=True))
        p = e / jnp.sum(e, axis=-1, keepdims=True)
        outs.append(_dot(p.astype(BF16), vm_ref[:, sl]).astype(BF16))
        yield
    c = _dot(jnp.concatenate(outs, axis=1), wmo_ref[...])
    yield
    x2_ref[rows, :] = x1 + _rms(c, gmem_ref[...])


def _mixmem_kernel(*refs):
    tm = refs[0].shape[0]
    groups = [_mixmem_rows(slice(r, r + ROW_GROUP), *refs) for r in range(0, tm, ROW_GROUP)]
    for _ in itertools.zip_longest(*groups):
        pass


def _mixmem(ys, yd, x2, km, vm, wo_s, wo_d, gmix, gq, wmq, wmo, gmem, S, M, tm):
    T = x2.shape[0]
    per_b = S // tm
    row = lambda n: pl.BlockSpec((tm, n), lambda i: (i, 0))
    mem = pl.BlockSpec((M, D_MODEL), lambda i: (i // per_b, 0))
    consts = (wo_s, wo_d, gmix, gq, wmq, wmo, gmem)
    return pl.pallas_call(
        _mixmem_kernel,
        grid=(T // tm,),
        in_specs=[row(D_SSD), row(D_DIFF), row(D_MODEL), mem, mem] + [_const_spec(c.shape) for c in consts],
        out_specs=row(D_MODEL),
        out_shape=jax.ShapeDtypeStruct((T, D_MODEL), F32),
        compiler_params=_params("parallel"),
        name="mixmem",
    )(ys, yd, x2, km, vm, *consts)


def _mlp_rows(rows, tf, x_ref, gpre_ref, wup_ref, wdn_ref, gpost_ref, o_ref):
    x = x_ref[rows, :]
    h = _rms(x, gpre_ref[...]).astype(BF16)
    acc = jnp.zeros(x.shape, F32)
    for c in range(D_FF // tf):
        u = _dot(h, wup_ref[:, c * tf:(c + 1) * tf])
        yield
        u = jnp.maximum(u, 0.0)
        acc = acc + _dot((u * u).astype(BF16), wdn_ref[c * tf:(c + 1) * tf, :])
        yield
    o_ref[rows, :] = x + _rms(acc, gpost_ref[...])


def _mlp_kernel(*refs, tf):
    tm = refs[0].shape[0]
    groups = [_mlp_rows(slice(r, r + ROW_GROUP), tf, *refs) for r in range(0, tm, ROW_GROUP)]
    for _ in itertools.zip_longest(*groups):
        pass


def _mlp(x2, gpre, wup, wdn, gpost, tm, tf):
    T = x2.shape[0]
    row = pl.BlockSpec((tm, D_MODEL), lambda i: (i, 0))
    return pl.pallas_call(
        functools.partial(_mlp_kernel, tf=tf),
        grid=(T // tm,),
        in_specs=[row, _const_spec(gpre.shape), _const_spec(wup.shape), _const_spec(wdn.shape),
                  _const_spec(gpost.shape)],
        out_specs=row,
        out_shape=jax.ShapeDtypeStruct((T, D_MODEL), F32),
        compiler_params=_params("parallel"),
        name="mlp",
    )(x2, gpre, wup, wdn, gpost)


def _rope_tables(positions):
    inv = ROPE_THETA ** (-jnp.arange(0, DIFF_HEAD_DIM, 2, dtype=F32) / DIFF_HEAD_DIM)
    ang = inv.reshape(-1, 1) * positions.astype(F32).reshape(1, -1)
    return jnp.cos(ang), jnp.sin(ang)


def _head_lane_order(wcols):
    half = DIFF_HEAD_DIM // 2
    rows = wcols.shape[0]
    return wcols.reshape(rows, DIFF_HEADS, 2, 2, half).transpose(0, 1, 3, 2, 4).reshape(rows, D_DIFF)


def kernel(x, mem, positions, norm_mix_pre, norm_mix_post, norm_mem_q, norm_mem_kv, norm_mem_post,
           norm_mlp_pre, norm_mlp_post, w_in, conv_w, conv_b, dt_bias, a_log, d_skip, ssd_norm_w,
           lambda_q1, lambda_k1, lambda_q2, lambda_k2, subln_w, w_out, w_mq, w_mk, w_mv, w_mo, w_up, w_down):
    B, S, _ = x.shape
    M = mem.shape[1]
    T = B * S
    assert norm_mix_pre.shape[0] == 1, "single-layer trunk"
    x2 = x.reshape(T, D_MODEL)
    cos, sin = _rope_tables(positions)

    w = w_in[0]
    o0 = D_SSD
    o1 = o0 + D_XBC
    o2 = o1 + SSD_HEADS
    o3 = o2 + D_DIFF
    o4 = o3 + D_DIFF
    wz = w[:, :o0].astype(BF16)
    wxbc = w[:, o0:o1].astype(BF16)
    wq = _head_lane_order(w[:, o2:o3]).astype(BF16)
    wk = _head_lane_order(w[:, o3:o4]).astype(BF16)
    head_pad = ((0, DT_ROWS - SSD_HEADS), (0, 0))
    wvdt = jnp.concatenate([w[:, o4:].T, jnp.pad(w[:, o1:o2].T, head_pad)], axis=0).astype(BF16)
    head_col = lambda p: jnp.pad(p.reshape(SSD_HEADS, 1), head_pad)

    zg, xc, dtt, q, k, vt = _inproj(x2, norm_mix_pre, cos, sin, wz, wxbc, wq, wk, wvdt,
                                    conv_w[0, :, 0, :], conv_b, head_col(dt_bias), S, tm=1024)

    y_ssd = _ssd(xc, zg, dtt, head_col(a_log), jnp.repeat(d_skip, SSD_HEAD_DIM, axis=1), ssd_norm_w, B, S)

    lamv = jnp.concatenate([lambda_q1, lambda_k1, lambda_q2, lambda_k2], axis=0)
    y_diff = _diffattn(lamv, subln_w.reshape(2 * DIFF_HEAD_DIM, 1), q, k, vt, B, S, tq=512)

    wo = w_out[0].astype(BF16)
    km, vm = _memkv(mem.reshape(B * M, D_MODEL), norm_mem_kv, w_mk[0].astype(BF16), w_mv[0].astype(BF16), M)
    x2b = _mixmem(y_ssd, y_diff, x2, km, vm, wo[:D_SSD], wo[D_SSD:], norm_mix_post, norm_mem_q,
                  w_mq[0].astype(BF16), w_mo[0].astype(BF16), norm_mem_post, S, M, tm=1024)
    out = _mlp(x2b, norm_mlp_pre, w_up[0].astype(BF16), w_down[0].astype(BF16), norm_mlp_post, tm=1024, tf=1024)
    return out.reshape(B, S, D_MODEL)
```

```python
import functools
import itertools
import math

import jax
import jax.numpy as jnp
from jax import lax
from jax.experimental import pallas as pl
from jax.experimental.pallas import tpu as pltpu

F32 = jnp.float32
BF16 = jnp.bfloat16

D_MODEL = 1024
D_SSD = 512
SSD_HEAD_DIM = 64
SSD_HEADS = D_SSD // SSD_HEAD_DIM
SSD_GROUPS = 2
SSD_STATE = 128
CONV_WIDTH = 4
CHUNK = 128
D_XBC = D_SSD + 2 * SSD_GROUPS * SSD_STATE
D_DIFF = D_MODEL - D_SSD
DIFF_HEAD_DIM = 64
DIFF_HEADS = D_DIFF // (2 * DIFF_HEAD_DIM)
ROPE_THETA = 10000.0
MEM_HEADS = 4
MEM_HEAD_DIM = D_MODEL // MEM_HEADS
D_FF = 4 * D_MODEL
NORM_EPS = 1e-6
LAMBDA_INIT = 0.8 - 0.6 * math.exp(-0.3 * 0)

LOG2E = math.log2(math.e)
LANES = 128
SUBLANES = 8
ONES_ROWS = 16
DT_ROWS = 16
CONV_COLS = 256
SSD_SEQS_PER_STEP = 4
SSD_CHUNKS_PER_STEP = 4
DIFF_HEADS_PER_STEP = 2
ROW_GROUP = 256
VMEM_LIMIT = 56 * 1024 * 1024


def _dot(a, b):
    return jnp.dot(a, b, preferred_element_type=F32)


def _dot_nt(a, b):
    return lax.dot_general(a, b, (((1,), (1,)), ((), ())), preferred_element_type=F32)


def _rms(x, g):
    ms = jnp.mean(x * x, axis=-1, keepdims=True)
    return x * lax.rsqrt(ms + NORM_EPS) * g


def _silu(a):
    half = 0.5 * a
    return half + half * jnp.tanh(half)


def _const_spec(shape):
    nd = len(shape)
    return pl.BlockSpec(shape, lambda *_: (0,) * nd, pipeline_mode=pl.Buffered(1))


def _params(*sem):
    return pltpu.CompilerParams(dimension_semantics=sem, vmem_limit_bytes=VMEM_LIMIT)


def _inproj_kernel(x_ref, g_ref, cos_ref, sin_ref, wz_ref, wxbc_ref, wq_ref, wk_ref, wvdt_ref,
                   cw_ref, cb_ref, dtb_ref,
                   zg_ref, xc_ref, dtt_ref, q_ref, k_ref, vt_ref, halo_ref, *, tm, tiles_per_seq):
    @pl.when(pl.program_id(0) % tiles_per_seq == 0)
    def _():
        halo_ref[...] = jnp.zeros_like(halo_ref)

    vregs = ROW_GROUP // SUBLANES
    sub = lax.broadcasted_iota(jnp.int32, (1, SUBLANES, 1), 1)
    reps = LANES // (DIFF_HEAD_DIM // 2)
    lane = lax.broadcasted_iota(jnp.int32, (ROW_GROUP, LANES), 1)
    comp0 = (lane % DIFF_HEAD_DIM) < DIFF_HEAD_DIM // 2
    history = {}

    def row_group(rows):
        h = _rms(x_ref[rows, :], g_ref[...]).astype(BF16)

        def gate(z):
            zg_ref[rows, :] = _silu(z).astype(BF16)

        def conv(c, xbc):
            cols = slice(c * CONV_COLS, (c + 1) * CONV_COLS)
            xb = xbc.reshape(vregs, SUBLANES, CONV_COLS)
            prev = history[c] if c in history else halo_ref[:, cols]
            acc = cb_ref[:, cols] + cw_ref[CONV_WIDTH - 1:CONV_WIDTH, cols] * xb
            for k in range(1, CONV_WIDTH):
                w = cw_ref[CONV_WIDTH - 1 - k:CONV_WIDTH - k, cols]
                r = pltpu.roll(xb * w, k, 1)
                r_first = pltpu.roll(prev * w, k, 0)[None]
                acc = acc + jnp.where(sub >= k, r, jnp.concatenate([r_first, r[:vregs - 1]], axis=0))
            history[c] = xb[vregs - 1]
            xc_ref[rows, cols] = _silu(acc.reshape(ROW_GROUP, CONV_COLS)).astype(BF16)

        def values_and_steps(r):
            vt_ref[:, rows] = r[:D_DIFF].astype(BF16)
            dtt_ref[:, rows] = jax.nn.softplus(r[D_DIFF:] + dtb_ref[...])

        tables = []

        def rope(a, c):
            if not tables:
                cos = jnp.concatenate([cos_ref[:, rows]] * reps, axis=0).T
                sin = jnp.concatenate([sin_ref[:, rows]] * reps, axis=0).T
                tables.extend([cos, jnp.where(lane < LANES // 2, -sin, sin)])
            ac = a[:, c * LANES:(c + 1) * LANES]
            return ac * tables[0] + pltpu.roll(ac, LANES // 2, 1) * tables[1]

        def keys(a):
            for c in range(DIFF_HEADS):
                k_ref[rows, c * LANES:(c + 1) * LANES] = rope(a, c).astype(BF16)

        def queries(a):
            for c in range(DIFF_HEADS):
                qh = rope(a, c) * (DIFF_HEAD_DIM ** -0.5 * LOG2E)
                q_ref[rows, (2 * c) * LANES:(2 * c + 1) * LANES] = jnp.where(comp0, qh, 0.0).astype(BF16)
                q_ref[rows, (2 * c + 1) * LANES:(2 * c + 2) * LANES] = jnp.where(comp0, 0.0, qh).astype(BF16)

        light = [(lambda: _dot_nt(wvdt_ref[...], h), values_and_steps), (lambda: _dot(h, wz_ref[...]), gate),
                 (lambda: _dot(h, wk_ref[...]), keys), (lambda: _dot(h, wq_ref[...]), queries)]
        for c in range(D_XBC // CONV_COLS):
            for matmul, epilogue in ((functools.partial(lambda c: _dot(h, wxbc_ref[:, c * CONV_COLS:(c + 1) * CONV_COLS]), c),
                                      functools.partial(conv, c)), light[c]):
                result = matmul()
                yield
                epilogue(result)

    for _ in itertools.zip_longest(*[row_group(slice(r, r + ROW_GROUP)) for r in range(0, tm, ROW_GROUP)]):
        pass
    for c, last_rows in history.items():
        halo_ref[:, c * CONV_COLS:(c + 1) * CONV_COLS] = last_rows


def _inproj(x2, g, cos, sin, wz, wxbc, wq, wk, wvdt, cw, cb, dtb_col, S, tm):
    T = x2.shape[0]
    row = lambda n: pl.BlockSpec((tm, n), lambda i: (i, 0))
    col = lambda n: pl.BlockSpec((n, tm), lambda i: (0, i))
    return pl.pallas_call(
        functools.partial(_inproj_kernel, tm=tm, tiles_per_seq=S // tm),
        grid=(T // tm,),
        in_specs=[row(D_MODEL), _const_spec((1, D_MODEL)), col(cos.shape[0]), col(sin.shape[0]),
                  _const_spec(wz.shape), _const_spec(wxbc.shape),
                  _const_spec(wq.shape), _const_spec(wk.shape), _const_spec(wvdt.shape),
                  _const_spec(cw.shape), _const_spec(cb.shape), _const_spec(dtb_col.shape)],
        out_specs=[row(D_SSD), row(D_XBC), col(DT_ROWS), row(2 * D_DIFF), row(D_DIFF), col(D_DIFF)],
        out_shape=[jax.ShapeDtypeStruct((T, D_SSD), BF16),
                   jax.ShapeDtypeStruct((T, D_XBC), BF16),
                   jax.ShapeDtypeStruct((DT_ROWS, T), F32),
                   jax.ShapeDtypeStruct((T, 2 * D_DIFF), BF16),
                   jax.ShapeDtypeStruct((T, D_DIFF), BF16),
                   jax.ShapeDtypeStruct((D_DIFF, T), BF16)],
        scratch_shapes=[pltpu.VMEM((SUBLANES, D_XBC), F32)],
        compiler_params=_params("arbitrary"),
        name="inproj",
    )(x2, g, cos, sin, wz, wxbc, wq, wk, wvdt, cw, cb, dtb_col)


def _ssd_chunk(xc_ref, zg_ref, dt_t, alog_ref, dskip_ref, nw_ref, y_ref, state_ref, lane, causal, tri_t):
    da_t = dt_t * (-LOG2E * jnp.exp(alog_ref[...]))
    hi = da_t.astype(BF16)
    rem = da_t - hi.astype(F32)
    mid = rem.astype(BF16)
    lo = (rem - mid.astype(F32)).astype(BF16)
    cum_t = _dot(hi, tri_t) + _dot(mid, tri_t) + _dot(lo, tri_t)
    yield
    last = cum_t[:, CHUNK - 1:CHUNK]
    wrow_t = dt_t * jnp.exp2(last - cum_t)
    cdec = jnp.exp2(last)
    cum = jnp.concatenate([cum_t, jnp.zeros((CHUNK - DT_ROWS, CHUNK), F32)], axis=0).T
    yield

    gn = SSD_GROUPS * SSD_STATE
    pairs_per_group = SSD_HEADS // SSD_GROUPS // 2
    y_pairs = []
    for g in range(SSD_GROUPS):
        bg = xc_ref[:, D_SSD + g * SSD_STATE:D_SSD + (g + 1) * SSD_STATE]
        cg = xc_ref[:, D_SSD + gn + g * SSD_STATE:D_SSD + gn + (g + 1) * SSD_STATE]
        cbm = _dot_nt(cg, bg)
        bgt = bg.astype(F32).T
        cg32 = cg.astype(F32)
        yield
        for jp in range(pairs_per_group):
            j = g * pairs_per_group + jp
            xs_pair = xc_ref[:, j * LANES:(j + 1) * LANES]
            yp = jnp.zeros((CHUNK, LANES), F32)
            for half in range(2):
                hd = 2 * j + half
                colb = jnp.broadcast_to(cum[:, hd:hd + 1], (CHUNK, LANES))
                dec = jnp.where(causal, jnp.exp2(colb - cum_t[hd:hd + 1, :]), 0.0)
                wp = (cbm * dec * dt_t[hd:hd + 1, :]).astype(BF16)
                gg = (cg32 * jnp.exp2(colb)).astype(BF16)
                in_half = (lane >= SSD_HEAD_DIM) if half else (lane < SSD_HEAD_DIM)
                xs_m = jnp.where(in_half, xs_pair, jnp.zeros_like(xs_pair))
                st = state_ref[hd]
                yp = yp + _dot(jnp.concatenate([wp, gg], axis=1),
                               jnp.concatenate([xs_m, st.astype(BF16)], axis=0))
                l2 = (bgt * wrow_t[hd:hd + 1, :]).astype(BF16)
                state_ref[hd] = st * cdec[hd:hd + 1, :] + _dot(l2, xs_m)
                yield
            y_pairs.append(yp)

    y = jnp.concatenate(y_pairs, axis=1) + dskip_ref[...] * xc_ref[:, :D_SSD].astype(F32)
    y = y * zg_ref[...].astype(F32)
    y_ref[...] = _rms(y, nw_ref[...]).astype(BF16)


def _ssd_kernel(xc_ref, zg_ref, *rest):
    dtt_refs = rest[:SSD_SEQS_PER_STEP]
    alog_ref, dskip_ref, nw_ref, y_ref, state_ref = rest[SSD_SEQS_PER_STEP:]

    @pl.when(pl.program_id(1) == 0)
    def _():
        state_ref[...] = jnp.zeros_like(state_ref)

    row = lax.broadcasted_iota(jnp.int32, (CHUNK, LANES), 0)
    lane = lax.broadcasted_iota(jnp.int32, (CHUNK, LANES), 1)
    causal = row >= lane
    tri_t = jnp.where(row <= lane, 1.0, 0.0).astype(BF16)
    for sc in range(SSD_CHUNKS_PER_STEP):
        t = pl.ds(sc * CHUNK, CHUNK)
        chunks = [_ssd_chunk(xc_ref.at[0, p, t], zg_ref.at[0, p, t], dtt_refs[p][:, sc * CHUNK:(sc + 1) * CHUNK],
                             alog_ref, dskip_ref, nw_ref, y_ref.at[0, p, t], state_ref.at[p], lane, causal, tri_t)
                  for p in range(SSD_SEQS_PER_STEP)]
        for _ in itertools.zip_longest(*chunks):
            pass


def _ssd(xc, zg, dtt, alog_col, dskip, nw, B, S):
    span = SSD_CHUNKS_PER_STEP * CHUNK
    nc = S // span
    nb = SSD_SEQS_PER_STEP
    seqs = lambda a: a.reshape(B // nb, nb, S, a.shape[-1])
    row = lambda n: pl.BlockSpec((1, nb, span, n), lambda b, c: (b, 0, c, 0))
    y = pl.pallas_call(
        _ssd_kernel,
        grid=(B // nb, nc),
        in_specs=[row(D_XBC), row(D_SSD)]
        + [pl.BlockSpec((DT_ROWS, span), functools.partial(lambda p, b, c: (0, (b * nb + p) * nc + c), p))
           for p in range(nb)]
        + [_const_spec(alog_col.shape), _const_spec(dskip.shape), _const_spec(nw.shape)],
        out_specs=row(D_SSD),
        out_shape=jax.ShapeDtypeStruct((B // nb, nb, S, D_SSD), BF16),
        scratch_shapes=[pltpu.VMEM((nb, SSD_HEADS, SSD_STATE, LANES), F32)],
        compiler_params=_params("parallel", "arbitrary"),
        name="ssd",
    )(seqs(xc), seqs(zg), *([dtt] * nb), alog_col, dskip, nw)
    return y.reshape(B * S, D_SSD)


def _diffattn_kernel(lam_ref, subw_ref, q_ref, k_ref, vt_ref, o_ref, sa_scr, sb_scr, acc_scr, m_scr, *,
                     tq, tk, nq, heads):
    hd = 2 * DIFF_HEAD_DIM
    ones = jnp.ones((ONES_ROWS, tk), BF16)
    krow = lax.broadcasted_iota(jnp.int32, (tk, tk), 0)
    qcol = lax.broadcasted_iota(jnp.int32, (tk, tk), 1)
    tri = krow <= qcol
    lv = lam_ref[...]
    lam = (jnp.exp(jnp.sum(lv[0:1] * lv[1:2], axis=-1, keepdims=True))
           - jnp.exp(jnp.sum(lv[2:3] * lv[3:4], axis=-1, keepdims=True)) + LAMBDA_INIT)

    steps = []
    for i in range(nq):
        steps += [(i, kb, "full") for kb in range(2 * i)] + [(i, 2 * i, "diag0"), (i, 2 * i + 1, "diag1")]

    def query_cols(kind):
        return (tk, tq) if kind == "diag1" else (0, tq)

    def head(hh):
        acc = acc_scr.at[hh]
        mx = m_scr.at[hh]
        bufs = (sa_scr.at[hh], sb_scr.at[hh])
        feat = slice(hh * hd, (hh + 1) * hd)

        def scores(step, s_ref):
            i, kb, kind = step
            lo, hi = query_cols(kind)
            kblk = k_ref[kb * tk:(kb + 1) * tk, feat]
            for c in range(2):
                s = _dot_nt(kblk, q_ref[i * tq + lo:i * tq + hi, (2 * hh + c) * LANES:(2 * hh + c + 1) * LANES])
                if kind == "diag0":
                    s = jnp.concatenate([jnp.where(tri, s[:, :tk], -jnp.inf), s[:, tk:]], axis=1)
                elif kind == "diag1":
                    s = jnp.where(tri, s, -jnp.inf)
                s_ref[c, :, lo:hi] = s

        def softmax_pv(step, s_ref):
            i, kb, kind = step
            lo, hi = query_cols(kind)
            par = i % 2
            lhs = jnp.concatenate([vt_ref[feat, kb * tk:(kb + 1) * tk], ones], axis=0)
            for c in range(2):
                m_blk = jnp.max(s_ref[c, :, lo:hi], axis=0, keepdims=True)
                if kb == 0:
                    m_new = m_blk
                    acc[par, c, :, lo:hi] = _dot(lhs, jnp.exp2(s_ref[c, :, lo:hi] - m_new).astype(BF16))
                else:
                    m_old = mx[par, c, :, lo:hi]
                    m_new = jnp.maximum(m_old, m_blk)
                    alpha = jnp.exp2(m_old - m_new)
                    p = jnp.exp2(s_ref[c, :, lo:hi] - m_new).astype(BF16)
                    acc[par, c, :, lo:hi] = alpha * acc[par, c, :, lo:hi] + _dot(lhs, p)
                mx[par, c, :, lo:hi] = m_new

        def finish(i):
            par = i % 2
            a0 = acc[par, 0]
            a1 = acc[par, 1]
            ot = a0[:hd] * (1.0 / a0[hd:hd + 1]) - a1[:hd] * (lam / a1[hd:hd + 1])
            ms = jnp.mean(ot * ot, axis=0, keepdims=True)
            ot = ot * (lax.rsqrt(ms + NORM_EPS) * (1.0 - LAMBDA_INIT)) * subw_ref[...]
            o_ref[i * tq:(i + 1) * tq, feat] = ot.T.astype(BF16)

        scores(steps[0], bufs[0])
        yield
        for n, step in enumerate(steps):
            if n + 1 < len(steps):
                scores(steps[n + 1], bufs[(n + 1) % 2])
                yield
            softmax_pv(step, bufs[n % 2])
            yield
            if step[2] == "diag1":
                finish(step[0])

    for _ in itertools.zip_longest(*[head(hh) for hh in range(heads)]):
        pass


def _diffattn(lamv, subw_col, q, k, vt, B, S, tq):
    nq = S // tq
    tk = tq // 2
    hd = 2 * DIFF_HEAD_DIM
    hp = DIFF_HEADS_PER_STEP
    return pl.pallas_call(
        functools.partial(_diffattn_kernel, tq=tq, tk=tk, nq=nq, heads=hp),
        grid=(B, DIFF_HEADS // hp),
        in_specs=[_const_spec(lamv.shape), _const_spec(subw_col.shape),
                  pl.BlockSpec((S, 2 * hd * hp), lambda b, h: (b, h)),
                  pl.BlockSpec((S, hd * hp), lambda b, h: (b, h)),
                  pl.BlockSpec((hd * hp, S), lambda b, h: (h, b))],
        out_specs=pl.BlockSpec((S, hd * hp), lambda b, h: (b, h)),
        out_shape=jax.ShapeDtypeStruct((B * S, D_DIFF), BF16),
        scratch_shapes=[pltpu.VMEM((hp, 2, tk, tq), F32),
                        pltpu.VMEM((hp, 2, tk, tq), F32),
                        pltpu.VMEM((hp, 2, 2, hd + ONES_ROWS, tq), F32),
                        pltpu.VMEM((hp, 2, 2, 1, tq), F32)],
        compiler_params=_params("parallel", "parallel"),
        name="diffattn",
    )(lamv, subw_col, q, k, vt)


def _memkv_kernel(mem_ref, g_ref, wk_ref, wv_ref, k_ref, v_ref):
    h = _rms(mem_ref[...], g_ref[...]).astype(BF16)
    k_ref[...] = _dot(h, wk_ref[...]).astype(BF16)
    v_ref[...] = _dot(h, wv_ref[...]).astype(BF16)


def _memkv(mem2, g, wk, wv, M):
    R = mem2.shape[0]
    row = pl.BlockSpec((M, D_MODEL), lambda i: (i, 0))
    return pl.pallas_call(
        _memkv_kernel,
        grid=(R // M,),
        in_specs=[row, _const_spec(g.shape), _const_spec(wk.shape), _const_spec(wv.shape)],
        out_specs=[row, row],
        out_shape=[jax.ShapeDtypeStruct((R, D_MODEL), BF16)] * 2,
        compiler_params=_params("parallel"),
        name="memkv",
    )(mem2, g, wk, wv)


def _mixmem_rows(rows, ys_ref, yd_ref, x_ref, km_ref, vm_ref, wo_s_ref, wo_d_ref, gmix_ref, gq_ref, wmq_ref,
                 wmo_ref, gmem_ref, x2_ref):
    mixed = _dot(ys_ref[rows, :], wo_s_ref[...]) + _dot(yd_ref[rows, :], wo_d_ref[...])
    yield
    x1 = x_ref[rows, :] + _rms(mixed, gmix_ref[...])
    qm = _dot(_rms(x1, gq_ref[...]).astype(BF16), wmq_ref[...])
    yield
    qm = (qm * (MEM_HEAD_DIM ** -0.5)).astype(BF16)
    outs = []
    for hd in range(MEM_HEADS):
        sl = slice(hd * MEM_HEAD_DIM, (hd + 1) * MEM_HEAD_DIM)
        s = _dot_nt(qm[:, sl], km_ref[:, sl])
        yield
        e = jnp.exp(s - jnp.max(s, axis=-1, keepdims=True))
        p = e / jnp.sum(e, axis=-1, keepdims=True)
        outs.append(_dot(p.astype(BF16), vm_ref[:, sl]).astype(BF16))
        yield
    c = _dot(jnp.concatenate(outs, axis=1), wmo_ref[...])
    yield
    x2_ref[rows, :] = x1 + _rms(c, gmem_ref[...])


def _mixmem_kernel(*refs):
    tm = refs[0].shape[0]
    groups = [_mixmem_rows(slice(r, r + ROW_GROUP), *refs) for r in range(0, tm, ROW_GROUP)]
    for _ in itertools.zip_longest(*groups):
        pass


def _mixmem(ys, yd, x2, km, vm, wo_s, wo_d, gmix, gq, wmq, wmo, gmem, S, M, tm):
    T = x2.shape[0]
    per_b = S // tm
    row = lambda n: pl.BlockSpec((tm, n), lambda i: (i, 0))
    mem = pl.BlockSpec((M, D_MODEL), lambda i: (i // per_b, 0))
    consts = (wo_s, wo_d, gmix, gq, wmq, wmo, gmem)
    return pl.pallas_call(
        _mixmem_kernel,
        grid=(T // tm,),
        in_specs=[row(D_SSD), row(D_DIFF), row(D_MODEL), mem, mem] + [_const_spec(c.shape) for c in consts],
        out_specs=row(D_MODEL),
        out_shape=jax.ShapeDtypeStruct((T, D_MODEL), F32),
        compiler_params=_params("parallel"),
        name="mixmem",
    )(ys, yd, x2, km, vm, *consts)


def _mlp_rows(rows, tf, x_ref, gpre_ref, wup_ref, wdn_ref, gpost_ref, o_ref):
    x = x_ref[rows, :]
    h = _rms(x, gpre_ref[...]).astype(BF16)
    acc = jnp.zeros(x.shape, F32)
    for c in range(D_FF // tf):
        u = _dot(h, wup_ref[:, c * tf:(c + 1) * tf])
        yield
        u = jnp.maximum(u, 0.0)
        acc = acc + _dot((u * u).astype(BF16), wdn_ref[c * tf:(c + 1) * tf, :])
        yield
    o_ref[rows, :] = x + _rms(acc, gpost_ref[...])


def _mlp_kernel(*refs, tf):
    tm = refs[0].shape[0]
    groups = [_mlp_rows(slice(r, r + ROW_GROUP), tf, *refs) for r in range(0, tm, ROW_GROUP)]
    for _ in itertools.zip_longest(*groups):
        pass


def _mlp(x2, gpre, wup, wdn, gpost, tm, tf):
    T = x2.shape[0]
    row = pl.BlockSpec((tm, D_MODEL), lambda i: (i, 0))
    return pl.pallas_call(
        functools.partial(_mlp_kernel, tf=tf),
        grid=(T // tm,),
        in_specs=[row, _const_spec(gpre.shape), _const_spec(wup.shape), _const_spec(wdn.shape),
                  _const_spec(gpost.shape)],
        out_specs=row,
        out_shape=jax.ShapeDtypeStruct((T, D_MODEL), F32),
        compiler_params=_params("parallel"),
        name="mlp",
    )(x2, gpre, wup, wdn, gpost)


def _rope_tables(positions):
    inv = ROPE_THETA ** (-jnp.arange(0, DIFF_HEAD_DIM, 2, dtype=F32) / DIFF_HEAD_DIM)
    ang = inv.reshape(-1, 1) * positions.astype(F32).reshape(1, -1)
    return jnp.cos(ang), jnp.sin(ang)


def _head_lane_order(wcols):
    half = DIFF_HEAD_DIM // 2
    rows = wcols.shape[0]
    return wcols.reshape(rows, DIFF_HEADS, 2, 2, half).transpose(0, 1, 3, 2, 4).reshape(rows, D_DIFF)


def kernel(x, mem, positions, norm_mix_pre, norm_mix_post, norm_mem_q, norm_mem_kv, norm_mem_post,
           norm_mlp_pre, norm_mlp_post, w_in, conv_w, conv_b, dt_bias, a_log, d_skip, ssd_norm_w,
           lambda_q1, lambda_k1, lambda_q2, lambda_k2, subln_w, w_out, w_mq, w_mk, w_mv, w_mo, w_up, w_down):
    B, S, _ = x.shape
    M = mem.shape[1]
    T = B * S
    assert norm_mix_pre.shape[0] == 1, "single-layer trunk"
    x2 = x.reshape(T, D_MODEL)
    cos, sin = _rope_tables(positions)

    w = w_in[0]
    o0 = D_SSD
    o1 = o0 + D_XBC
    o2 = o1 + SSD_HEADS
    o3 = o2 + D_DIFF
    o4 = o3 + D_DIFF
    wz = w[:, :o0].astype(BF16)
    wxbc = w[:, o0:o1].astype(BF16)
    wq = _head_lane_order(w[:, o2:o3]).astype(BF16)
    wk = _head_lane_order(w[:, o3:o4]).astype(BF16)
    head_pad = ((0, DT_ROWS - SSD_HEADS), (0, 0))
    wvdt = jnp.concatenate([w[:, o4:].T, jnp.pad(w[:, o1:o2].T, head_pad)], axis=0).astype(BF16)
    head_col = lambda p: jnp.pad(p.reshape(SSD_HEADS, 1), head_pad)

    zg, xc, dtt, q, k, vt = _inproj(x2, norm_mix_pre, cos, sin, wz, wxbc, wq, wk, wvdt,
                                    conv_w[0, :, 0, :], conv_b, head_col(dt_bias), S, tm=512)

    y_ssd = _ssd(xc, zg, dtt, head_col(a_log), jnp.repeat(d_skip, SSD_HEAD_DIM, axis=1), ssd_norm_w, B, S)

    lamv = jnp.concatenate([lambda_q1, lambda_k1, lambda_q2, lambda_k2], axis=0)
    y_diff = _diffattn(lamv, subln_w.reshape(2 * DIFF_HEAD_DIM, 1), q, k, vt, B, S, tq=512)

    wo = w_out[0].astype(BF16)
    km, vm = _memkv(mem.reshape(B * M, D_MODEL), norm_mem_kv, w_mk[0].astype(BF16), w_mv[0].astype(BF16), M)
    x2b = _mixmem(y_ssd, y_diff, x2, km, vm, wo[:D_SSD], wo[D_SSD:], norm_mix_post, norm_mem_q,
                  w_mq[0].astype(BF16), w_mo[0].astype(BF16), norm_mem_post, S, M, tm=1024)
    out = _mlp(x2b, norm_mlp_pre, w_up[0].astype(BF16), w_down[0].astype(BF16), norm_mlp_post, tm=1024, tf=1024)
    return out.reshape(B, S, D_MODEL)
```

```python
import functools
import itertools
import math

import jax
import jax.numpy as jnp
from jax import lax
from jax.experimental import pallas as pl
from jax.experimental.pallas import tpu as pltpu

F32 = jnp.float32
BF16 = jnp.bfloat16

D_MODEL = 1024
D_SSD = 512
SSD_HEAD_DIM = 64
SSD_HEADS = D_SSD // SSD_HEAD_DIM
SSD_GROUPS = 2
SSD_STATE = 128
CONV_WIDTH = 4
CHUNK = 128
D_XBC = D_SSD + 2 * SSD_GROUPS * SSD_STATE
D_DIFF = D_MODEL - D_SSD
DIFF_HEAD_DIM = 64
DIFF_HEADS = D_DIFF // (2 * DIFF_HEAD_DIM)
ROPE_THETA = 10000.0
MEM_HEADS = 4
MEM_HEAD_DIM = D_MODEL // MEM_HEADS
D_FF = 4 * D_MODEL
NORM_EPS = 1e-6
LAMBDA_INIT = 0.8 - 0.6 * math.exp(-0.3 * 0)

LOG2E = math.log2(math.e)
LANES = 128
SUBLANES = 8
ONES_ROWS = 16
DT_ROWS = 16
CONV_COLS = 256
SSD_SEQS_PER_STEP = 4
SSD_CHUNKS_PER_STEP = 4
DIFF_HEADS_PER_STEP = 2
ROW_GROUP = 256
VMEM_LIMIT = 56 * 1024 * 1024


def _dot(a, b):
    return jnp.dot(a, b, preferred_element_type=F32)


def _dot_nt(a, b):
    return lax.dot_general(a, b, (((1,), (1,)), ((), ())), preferred_element_type=F32)


def _rms(x, g):
    ms = jnp.mean(x * x, axis=-1, keepdims=True)
    return x * lax.rsqrt(ms + NORM_EPS) * g


def _silu(a):
    half = 0.5 * a
    return half + half * jnp.tanh(half)


def _const_spec(shape):
    nd = len(shape)
    return pl.BlockSpec(shape, lambda *_: (0,) * nd, pipeline_mode=pl.Buffered(1))


def _params(*sem):
    return pltpu.CompilerParams(dimension_semantics=sem, vmem_limit_bytes=VMEM_LIMIT)


def _inproj_kernel(x_ref, g_ref, cos_ref, sin_ref, wz_ref, wxbc_ref, wq_ref, wk_ref, wvdt_ref,
                   cw_ref, cb_ref, dtb_ref,
                   zg_ref, xc_ref, dtt_ref, q_ref, k_ref, vt_ref, halo_ref, *, tm, tiles_per_seq):
    @pl.when(pl.program_id(0) % tiles_per_seq == 0)
    def _():
        halo_ref[...] = jnp.zeros_like(halo_ref)

    vregs = ROW_GROUP // SUBLANES
    sub = lax.broadcasted_iota(jnp.int32, (1, SUBLANES, 1), 1)
    reps = LANES // (DIFF_HEAD_DIM // 2)
    lane = lax.broadcasted_iota(jnp.int32, (ROW_GROUP, LANES), 1)
    comp0 = (lane % DIFF_HEAD_DIM) < DIFF_HEAD_DIM // 2
    history = {}

    def row_group(rows):
        h = _rms(x_ref[rows, :], g_ref[...]).astype(BF16)

        def gate(z):
            zg_ref[rows, :] = _silu(z).astype(BF16)

        def conv(c, xbc):
            cols = slice(c * CONV_COLS, (c + 1) * CONV_COLS)
            xb = xbc.reshape(vregs, SUBLANES, CONV_COLS)
            prev = history[c] if c in history else halo_ref[:, cols]
            acc = cb_ref[:, cols] + cw_ref[CONV_WIDTH - 1:CONV_WIDTH, cols] * xb
            for k in range(1, CONV_WIDTH):
                w = cw_ref[CONV_WIDTH - 1 - k:CONV_WIDTH - k, cols]
                r = pltpu.roll(xb * w, k, 1)
                r_first = pltpu.roll(prev * w, k, 0)[None]
                acc = acc + jnp.where(sub >= k, r, jnp.concatenate([r_first, r[:vregs - 1]], axis=0))
            history[c] = xb[vregs - 1]
            xc_ref[rows, cols] = _silu(acc.reshape(ROW_GROUP, CONV_COLS)).astype(BF16)

        def values_and_steps(r):
            vt_ref[:, rows] = r[:D_DIFF].astype(BF16)
            dtt_ref[:, rows] = jax.nn.softplus(r[D_DIFF:] + dtb_ref[...])

        tables = []

        def rope(a, c):
            if not tables:
                cos = jnp.concatenate([cos_ref[:, rows]] * reps, axis=0).T
                sin = jnp.concatenate([sin_ref[:, rows]] * reps, axis=0).T
                tables.extend([cos, jnp.where(lane < LANES // 2, -sin, sin)])
            ac = a[:, c * LANES:(c + 1) * LANES]
            return ac * tables[0] + pltpu.roll(ac, LANES // 2, 1) * tables[1]

        def keys(a):
            for c in range(DIFF_HEADS):
                k_ref[rows, c * LANES:(c + 1) * LANES] = rope(a, c).astype(BF16)

        def queries(a):
            for c in range(DIFF_HEADS):
                qh = rope(a, c) * (DIFF_HEAD_DIM ** -0.5 * LOG2E)
                q_ref[rows, (2 * c) * LANES:(2 * c + 1) * LANES] = jnp.where(comp0, qh, 0.0).astype(BF16)
                q_ref[rows, (2 * c + 1) * LANES:(2 * c + 2) * LANES] = jnp.where(comp0, 0.0, qh).astype(BF16)

        light = [(lambda: _dot_nt(wvdt_ref[...], h), values_and_steps), (lambda: _dot(h, wz_ref[...]), gate),
                 (lambda: _dot(h, wk_ref[...]), keys), (lambda: _dot(h, wq_ref[...]), queries)]
        for c in range(D_XBC // CONV_COLS):
            for matmul, epilogue in ((functools.partial(lambda c: _dot(h, wxbc_ref[:, c * CONV_COLS:(c + 1) * CONV_COLS]), c),
                                      functools.partial(conv, c)), light[c]):
                result = matmul()
                yield
                epilogue(result)

    for _ in itertools.zip_longest(*[row_group(slice(r, r + ROW_GROUP)) for r in range(0, tm, ROW_GROUP)]):
        pass
    for c, last_rows in history.items():
        halo_ref[:, c * CONV_COLS:(c + 1) * CONV_COLS] = last_rows


def _inproj(x2, g, cos, sin, wz, wxbc, wq, wk, wvdt, cw, cb, dtb_col, S, tm):
    T = x2.shape[0]
    row = lambda n: pl.BlockSpec((tm, n), lambda i: (i, 0))
    col = lambda n: pl.BlockSpec((n, tm), lambda i: (0, i))
    return pl.pallas_call(
        functools.partial(_inproj_kernel, tm=tm, tiles_per_seq=S // tm),
        grid=(T // tm,),
        in_specs=[row(D_MODEL), _const_spec((1, D_MODEL)), col(cos.shape[0]), col(sin.shape[0]),
                  _const_spec(wz.shape), _const_spec(wxbc.shape),
                  _const_spec(wq.shape), _const_spec(wk.shape), _const_spec(wvdt.shape),
                  _const_spec(cw.shape), _const_spec(cb.shape), _const_spec(dtb_col.shape)],
        out_specs=[row(D_SSD), row(D_XBC), col(DT_ROWS), row(2 * D_DIFF), row(D_DIFF), col(D_DIFF)],
        out_shape=[jax.ShapeDtypeStruct((T, D_SSD), BF16),
                   jax.ShapeDtypeStruct((T, D_XBC), BF16),
                   jax.ShapeDtypeStruct((DT_ROWS, T), F32),
                   jax.ShapeDtypeStruct((T, 2 * D_DIFF), BF16),
                   jax.ShapeDtypeStruct((T, D_DIFF), BF16),
                   jax.ShapeDtypeStruct((D_DIFF, T), BF16)],
        scratch_shapes=[pltpu.VMEM((SUBLANES, D_XBC), F32)],
        compiler_params=_params("arbitrary"),
        name="inproj",
    )(x2, g, cos, sin, wz, wxbc, wq, wk, wvdt, cw, cb, dtb_col)


def _ssd_chunk(xc_ref, zg_ref, dt_t, alog_ref, dskip_ref, nw_ref, y_ref, state_ref, lane, causal, tri_t):
    da_t = dt_t * (-LOG2E * jnp.exp(alog_ref[...]))
    hi = da_t.astype(BF16)
    rem = da_t - hi.astype(F32)
    mid = rem.astype(BF16)
    lo = (rem - mid.astype(F32)).astype(BF16)
    cum_t = _dot(hi, tri_t) + _dot(mid, tri_t) + _dot(lo, tri_t)
    yield
    last = cum_t[:, CHUNK - 1:CHUNK]
    wrow_t = dt_t * jnp.exp2(last - cum_t)
    cdec = jnp.exp2(last)
    src_t = cum_t - jnp.log2(dt_t)
    cum = jnp.concatenate([cum_t, jnp.zeros((CHUNK - DT_ROWS, CHUNK), F32)], axis=0).T
    yield

    gn = SSD_GROUPS * SSD_STATE
    pairs_per_group = SSD_HEADS // SSD_GROUPS // 2
    y_pairs = []
    for g in range(SSD_GROUPS):
        bg = xc_ref[:, D_SSD + g * SSD_STATE:D_SSD + (g + 1) * SSD_STATE]
        cg = xc_ref[:, D_SSD + gn + g * SSD_STATE:D_SSD + gn + (g + 1) * SSD_STATE]
        cbm = _dot_nt(cg, bg)
        bgt = bg.astype(F32).T
        cg32 = cg.astype(F32)
        yield
        for jp in range(pairs_per_group):
            j = g * pairs_per_group + jp
            xs_pair = xc_ref[:, j * LANES:(j + 1) * LANES]
            lhs, rhs = [], []
            for half in range(2):
                hd = 2 * j + half
                colb = jnp.broadcast_to(cum[:, hd:hd + 1], (CHUNK, LANES))
                dec = jnp.where(causal, jnp.exp2(colb - src_t[hd:hd + 1, :]), 0.0)
                wp = (cbm * dec).astype(BF16)
                gg = (cg32 * jnp.exp2(colb)).astype(BF16)
                in_half = (lane >= SSD_HEAD_DIM) if half else (lane < SSD_HEAD_DIM)
                xs_m = jnp.where(in_half, xs_pair, jnp.zeros_like(xs_pair))
                st = state_ref[hd]
                lhs += [wp, gg]
                rhs += [xs_m, st.astype(BF16)]
                l2 = (bgt * wrow_t[hd:hd + 1, :]).astype(BF16)
                state_ref[hd] = st * cdec[hd:hd + 1, :] + _dot(l2, xs_m)
                yield
            y_pairs.append(_dot(jnp.concatenate(lhs, axis=1), jnp.concatenate(rhs, axis=0)))

    y = jnp.concatenate(y_pairs, axis=1) + dskip_ref[...] * xc_ref[:, :D_SSD].astype(F32)
    y = y * zg_ref[...].astype(F32)
    y_ref[...] = _rms(y, nw_ref[...]).astype(BF16)


def _ssd_kernel(xc_ref, zg_ref, *rest):
    dtt_refs = rest[:SSD_SEQS_PER_STEP]
    alog_ref, dskip_ref, nw_ref, y_ref, state_ref = rest[SSD_SEQS_PER_STEP:]

    @pl.when(pl.program_id(1) == 0)
    def _():
        state_ref[...] = jnp.zeros_like(state_ref)

    row = lax.broadcasted_iota(jnp.int32, (CHUNK, LANES), 0)
    lane = lax.broadcasted_iota(jnp.int32, (CHUNK, LANES), 1)
    causal = row >= lane
    tri_t = jnp.where(row <= lane, 1.0, 0.0).astype(BF16)
    for sc in range(SSD_CHUNKS_PER_STEP):
        t = pl.ds(sc * CHUNK, CHUNK)
        chunks = [_ssd_chunk(xc_ref.at[0, p, t], zg_ref.at[0, p, t], dtt_refs[p][:, sc * CHUNK:(sc + 1) * CHUNK],
                             alog_ref, dskip_ref, nw_ref, y_ref.at[0, p, t], state_ref.at[p], lane, causal, tri_t)
                  for p in range(SSD_SEQS_PER_STEP)]
        for _ in itertools.zip_longest(*chunks):
            pass


def _ssd(xc, zg, dtt, alog_col, dskip, nw, B, S):
    span = SSD_CHUNKS_PER_STEP * CHUNK
    nc = S // span
    nb = SSD_SEQS_PER_STEP
    seqs = lambda a: a.reshape(B // nb, nb, S, a.shape[-1])
    row = lambda n: pl.BlockSpec((1, nb, span, n), lambda b, c: (b, 0, c, 0))
    y = pl.pallas_call(
        _ssd_kernel,
        grid=(B // nb, nc),
        in_specs=[row(D_XBC), row(D_SSD)]
        + [pl.BlockSpec((DT_ROWS, span), functools.partial(lambda p, b, c: (0, (b * nb + p) * nc + c), p))
           for p in range(nb)]
        + [_const_spec(alog_col.shape), _const_spec(dskip.shape), _const_spec(nw.shape)],
        out_specs=row(D_SSD),
        out_shape=jax.ShapeDtypeStruct((B // nb, nb, S, D_SSD), BF16),
        scratch_shapes=[pltpu.VMEM((nb, SSD_HEADS, SSD_STATE, LANES), F32)],
        compiler_params=_params("parallel", "arbitrary"),
        name="ssd",
    )(seqs(xc), seqs(zg), *([dtt] * nb), alog_col, dskip, nw)
    return y.reshape(B * S, D_SSD)


def _diffattn_kernel(lam_ref, subw_ref, q_ref, k_ref, vt_ref, o_ref, sa_scr, sb_scr, acc_scr, m_scr, *,
                     tq, tk, nq, heads):
    hd = 2 * DIFF_HEAD_DIM
    ones = jnp.ones((ONES_ROWS, tk), BF16)
    krow = lax.broadcasted_iota(jnp.int32, (tk, tk), 0)
    qcol = lax.broadcasted_iota(jnp.int32, (tk, tk), 1)
    tri = krow <= qcol
    lv = lam_ref[...]
    lam = (jnp.exp(jnp.sum(lv[0:1] * lv[1:2], axis=-1, keepdims=True))
           - jnp.exp(jnp.sum(lv[2:3] * lv[3:4], axis=-1, keepdims=True)) + LAMBDA_INIT)

    steps = []
    for i in range(nq):
        steps += [(i, kb, "full") for kb in range(2 * i)] + [(i, 2 * i, "diag0"), (i, 2 * i + 1, "diag1")]

    def query_cols(kind):
        return (tk, tq) if kind == "diag1" else (0, tq)

    def head(hh):
        acc = acc_scr.at[hh]
        mx = m_scr.at[hh]
        bufs = (sa_scr.at[hh], sb_scr.at[hh])
        feat = slice(hh * hd, (hh + 1) * hd)

        def scores(step, s_ref):
            i, kb, kind = step
            lo, hi = query_cols(kind)
            kblk = k_ref[kb * tk:(kb + 1) * tk, feat]
            for c in range(2):
                s = _dot_nt(kblk, q_ref[i * tq + lo:i * tq + hi, (2 * hh + c) * LANES:(2 * hh + c + 1) * LANES])
                if kind == "diag0":
                    s = jnp.concatenate([jnp.where(tri, s[:, :tk], -jnp.inf), s[:, tk:]], axis=1)
                elif kind == "diag1":
                    s = jnp.where(tri, s, -jnp.inf)
                s_ref[c, :, lo:hi] = s

        def softmax_pv(step, s_ref):
            i, kb, kind = step
            lo, hi = query_cols(kind)
            par = i % 2
            lhs = jnp.concatenate([vt_ref[feat, kb * tk:(kb + 1) * tk], ones], axis=0)
            for c in range(2):
                m_blk = jnp.max(s_ref[c, :, lo:hi], axis=0, keepdims=True)
                if kb == 0:
                    m_new = m_blk
                    acc[par, c, :, lo:hi] = _dot(lhs, jnp.exp2(s_ref[c, :, lo:hi] - m_new).astype(BF16))
                else:
                    m_old = mx[par, c, :, lo:hi]
                    m_new = jnp.maximum(m_old, m_blk)
                    alpha = jnp.exp2(m_old - m_new)
                    p = jnp.exp2(s_ref[c, :, lo:hi] - m_new).astype(BF16)
                    acc[par, c, :, lo:hi] = alpha * acc[par, c, :, lo:hi] + _dot(lhs, p)
                mx[par, c, :, lo:hi] = m_new

        def finish(i):
            par = i % 2
            a0 = acc[par, 0]
            a1 = acc[par, 1]
            ot = a0[:hd] * (1.0 / a0[hd:hd + 1]) - a1[:hd] * (lam / a1[hd:hd + 1])
            ms = jnp.mean(ot * ot, axis=0, keepdims=True)
            ot = ot * (lax.rsqrt(ms + NORM_EPS) * (1.0 - LAMBDA_INIT)) * subw_ref[...]
            o_ref[i * tq:(i + 1) * tq, feat] = ot.T.astype(BF16)

        scores(steps[0], bufs[0])
        yield
        for n, step in enumerate(steps):
            if n + 1 < len(steps):
                scores(steps[n + 1], bufs[(n + 1) % 2])
                yield
            softmax_pv(step, bufs[n % 2])
            yield
            if step[2] == "diag1":
                finish(step[0])

    for _ in itertools.zip_longest(*[head(hh) for hh in range(heads)]):
        pass


def _diffattn(lamv, subw_col, q, k, vt, B, S, tq):
    nq = S // tq
    tk = tq // 2
    hd = 2 * DIFF_HEAD_DIM
    hp = DIFF_HEADS_PER_STEP
    return pl.pallas_call(
        functools.partial(_diffattn_kernel, tq=tq, tk=tk, nq=nq, heads=hp),
        grid=(B, DIFF_HEADS // hp),
        in_specs=[_const_spec(lamv.shape), _const_spec(subw_col.shape),
                  pl.BlockSpec((S, 2 * hd * hp), lambda b, h: (b, h)),
                  pl.BlockSpec((S, hd * hp), lambda b, h: (b, h)),
                  pl.BlockSpec((hd * hp, S), lambda b, h: (h, b))],
        out_specs=pl.BlockSpec((S, hd * hp), lambda b, h: (b, h)),
        out_shape=jax.ShapeDtypeStruct((B * S, D_DIFF), BF16),
        scratch_shapes=[pltpu.VMEM((hp, 2, tk, tq), F32),
                        pltpu.VMEM((hp, 2, tk, tq), F32),
                        pltpu.VMEM((hp, 2, 2, hd + ONES_ROWS, tq), F32),
                        pltpu.VMEM((hp, 2, 2, 1, tq), F32)],
        compiler_params=_params("parallel", "parallel"),
        name="diffattn",
    )(lamv, subw_col, q, k, vt)


def _memkv_kernel(mem_ref, g_ref, wk_ref, wv_ref, k_ref, v_ref):
    h = _rms(mem_ref[...], g_ref[...]).astype(BF16)
    k_ref[...] = _dot(h, wk_ref[...]).astype(BF16)
    v_ref[...] = _dot(h, wv_ref[...]).astype(BF16)


def _memkv(mem2, g, wk, wv, M):
    R = mem2.shape[0]
    row = pl.BlockSpec((M, D_MODEL), lambda i: (i, 0))
    return pl.pallas_call(
        _memkv_kernel,
        grid=(R // M,),
        in_specs=[row, _const_spec(g.shape), _const_spec(wk.shape), _const_spec(wv.shape)],
        out_specs=[row, row],
        out_shape=[jax.ShapeDtypeStruct((R, D_MODEL), BF16)] * 2,
        compiler_params=_params("parallel"),
        name="memkv",
    )(mem2, g, wk, wv)


def _mixmem_rows(rows, ys_ref, yd_ref, x_ref, km_ref, vm_ref, wo_s_ref, wo_d_ref, gmix_ref, gq_ref, wmq_ref,
                 wmo_ref, gmem_ref, x2_ref):
    mixed = _dot(ys_ref[rows, :], wo_s_ref[...]) + _dot(yd_ref[rows, :], wo_d_ref[...])
    yield
    x1 = x_ref[rows, :] + _rms(mixed, gmix_ref[...])
    qm = _dot(_rms(x1, gq_ref[...]).astype(BF16), wmq_ref[...])
    yield
    qm = (qm * (MEM_HEAD_DIM ** -0.5)).astype(BF16)
    outs = []
    for hd in range(MEM_HEADS):
        sl = slice(hd * MEM_HEAD_DIM, (hd + 1) * MEM_HEAD_DIM)
        s = _dot_nt(qm[:, sl], km_ref[:, sl])
        yield
        e = jnp.exp(s - jnp.max(s, axis=-1, keepdims=True))
        p = e / jnp.sum(e, axis=-1, keepdims=True)
        outs.append(_dot(p.astype(BF16), vm_ref[:, sl]).astype(BF16))
        yield
    c = _dot(jnp.concatenate(outs, axis=1), wmo_ref[...])
    yield
    x2_ref[rows, :] = x1 + _rms(c, gmem_ref[...])


def _mixmem_kernel(*refs):
    tm = refs[0].shape[0]
    groups = [_mixmem_rows(slice(r, r + ROW_GROUP), *refs) for r in range(0, tm, ROW_GROUP)]
    for _ in itertools.zip_longest(*groups):
        pass


def _mixmem(ys, yd, x2, km, vm, wo_s, wo_d, gmix, gq, wmq, wmo, gmem, S, M, tm):
    T = x2.shape[0]
    per_b = S // tm
    row = lambda n: pl.BlockSpec((tm, n), lambda i: (i, 0))
    mem = pl.BlockSpec((M, D_MODEL), lambda i: (i // per_b, 0))
    consts = (wo_s, wo_d, gmix, gq, wmq, wmo, gmem)
    return pl.pallas_call(
        _mixmem_kernel,
        grid=(T // tm,),
        in_specs=[row(D_SSD), row(D_DIFF), row(D_MODEL), mem, mem] + [_const_spec(c.shape) for c in consts],
        out_specs=row(D_MODEL),
        out_shape=jax.ShapeDtypeStruct((T, D_MODEL), F32),
        compiler_params=_params("parallel"),
        name="mixmem",
    )(ys, yd, x2, km, vm, *consts)


def _mlp_rows(rows, tf, x_ref, gpre_ref, wup_ref, wdn_ref, gpost_ref, o_ref):
    x = x_ref[rows, :]
    h = _rms(x, gpre_ref[...]).astype(BF16)
    acc = jnp.zeros(x.shape, F32)
    for c in range(D_FF // tf):
        u = _dot(h, wup_ref[:, c * tf:(c + 1) * tf])
        yield
        u = jnp.maximum(u, 0.0)
        acc = acc + _dot((u * u).astype(BF16), wdn_ref[c * tf:(c + 1) * tf, :])
        yield
    o_ref[rows, :] = x + _rms(acc, gpost_ref[...])


def _mlp_kernel(*refs, tf):
    tm = refs[0].shape[0]
    groups = [_mlp_rows(slice(r, r + ROW_GROUP), tf, *refs) for r in range(0, tm, ROW_GROUP)]
    for _ in itertools.zip_longest(*groups):
        pass


def _mlp(x2, gpre, wup, wdn, gpost, tm, tf):
    T = x2.shape[0]
    row = pl.BlockSpec((tm, D_MODEL), lambda i: (i, 0))
    return pl.pallas_call(
        functools.partial(_mlp_kernel, tf=tf),
        grid=(T // tm,),
        in_specs=[row, _const_spec(gpre.shape), _const_spec(wup.shape), _const_spec(wdn.shape),
                  _const_spec(gpost.shape)],
        out_specs=row,
        out_shape=jax.ShapeDtypeStruct((T, D_MODEL), F32),
        compiler_params=_params("parallel"),
        name="mlp",
    )(x2, gpre, wup, wdn, gpost)


def _rope_tables(positions):
    inv = ROPE_THETA ** (-jnp.arange(0, DIFF_HEAD_DIM, 2, dtype=F32) / DIFF_HEAD_DIM)
    ang = inv.reshape(-1, 1) * positions.astype(F32).reshape(1, -1)
    return jnp.cos(ang), jnp.sin(ang)


def _head_lane_order(wcols):
    half = DIFF_HEAD_DIM // 2
    rows = wcols.shape[0]
    return wcols.reshape(rows, DIFF_HEADS, 2, 2, half).transpose(0, 1, 3, 2, 4).reshape(rows, D_DIFF)


def kernel(x, mem, positions, norm_mix_pre, norm_mix_post, norm_mem_q, norm_mem_kv, norm_mem_post,
           norm_mlp_pre, norm_mlp_post, w_in, conv_w, conv_b, dt_bias, a_log, d_skip, ssd_norm_w,
           lambda_q1, lambda_k1, lambda_q2, lambda_k2, subln_w, w_out, w_mq, w_mk, w_mv, w_mo, w_up, w_down):
    B, S, _ = x.shape
    M = mem.shape[1]
    T = B * S
    assert norm_mix_pre.shape[0] == 1, "single-layer trunk"
    x2 = x.reshape(T, D_MODEL)
    cos, sin = _rope_tables(positions)

    w = w_in[0]
    o0 = D_SSD
    o1 = o0 + D_XBC
    o2 = o1 + SSD_HEADS
    o3 = o2 + D_DIFF
    o4 = o3 + D_DIFF
    wz = w[:, :o0].astype(BF16)
    wxbc = w[:, o0:o1].astype(BF16)
    wq = _head_lane_order(w[:, o2:o3]).astype(BF16)
    wk = _head_lane_order(w[:, o3:o4]).astype(BF16)
    head_pad = ((0, DT_ROWS - SSD_HEADS), (0, 0))
    wvdt = jnp.concatenate([w[:, o4:].T, jnp.pad(w[:, o1:o2].T, head_pad)], axis=0).astype(BF16)
    head_col = lambda p: jnp.pad(p.reshape(SSD_HEADS, 1), head_pad)

    zg, xc, dtt, q, k, vt = _inproj(x2, norm_mix_pre, cos, sin, wz, wxbc, wq, wk, wvdt,
                                    conv_w[0, :, 0, :], conv_b, head_col(dt_bias), S, tm=512)

    y_ssd = _ssd(xc, zg, dtt, head_col(a_log), jnp.repeat(d_skip, SSD_HEAD_DIM, axis=1), ssd_norm_w, B, S)

    lamv = jnp.concatenate([lambda_q1, lambda_k1, lambda_q2, lambda_k2], axis=0)
    y_diff = _diffattn(lamv, subln_w.reshape(2 * DIFF_HEAD_DIM, 1), q, k, vt, B, S, tq=512)

    wo = w_out[0].astype(BF16)
    km, vm = _memkv(mem.reshape(B * M, D_MODEL), norm_mem_kv, w_mk[0].astype(BF16), w_mv[0].astype(BF16), M)
    x2b = _mixmem(y_ssd, y_diff, x2, km, vm, wo[:D_SSD], wo[D_SSD:], norm_mix_post, norm_mem_q,
                  w_mq[0].astype(BF16), w_mo[0].astype(BF16), norm_mem_post, S, M, tm=1024)
    out = _mlp(x2b, norm_mlp_pre, w_up[0].astype(BF16), w_down[0].astype(BF16), norm_mlp_post, tm=1024, tf=1024)
    return out.reshape(B, S, D_MODEL)
```

```python
import functools
import itertools
import math

import jax
import jax.numpy as jnp
from jax import lax
from jax.experimental import pallas as pl
from jax.experimental.pallas import tpu as pltpu

F32 = jnp.float32
BF16 = jnp.bfloat16

D_MODEL = 1024
D_SSD = 512
SSD_HEAD_DIM = 64
SSD_HEADS = D_SSD // SSD_HEAD_DIM
SSD_GROUPS = 2
SSD_STATE = 128
CONV_WIDTH = 4
CHUNK = 128
D_XBC = D_SSD + 2 * SSD_GROUPS * SSD_STATE
D_DIFF = D_MODEL - D_SSD
DIFF_HEAD_DIM = 64
DIFF_HEADS = D_DIFF // (2 * DIFF_HEAD_DIM)
ROPE_THETA = 10000.0
MEM_HEADS = 4
MEM_HEAD_DIM = D_MODEL // MEM_HEADS
D_FF = 4 * D_MODEL
NORM_EPS = 1e-6
LAMBDA_INIT = 0.8 - 0.6 * math.exp(-0.3 * 0)

LOG2E = math.log2(math.e)
LANES = 128
SUBLANES = 8
ONES_ROWS = 16
DT_ROWS = 16
CONV_COLS = 256
SSD_SEQS_PER_STEP = 4
SSD_CHUNKS_PER_STEP = 4
DIFF_HEADS_PER_STEP = 2
ROW_GROUP = 256
VMEM_LIMIT = 56 * 1024 * 1024


def _dot(a, b):
    return jnp.dot(a, b, preferred_element_type=F32)


def _dot_nt(a, b):
    return lax.dot_general(a, b, (((1,), (1,)), ((), ())), preferred_element_type=F32)


def _rms(x, g):
    ms = jnp.mean(x * x, axis=-1, keepdims=True)
    return x * lax.rsqrt(ms + NORM_EPS) * g


def _silu_of_half(half):
    return half + half * jnp.tanh(half)


def _const_spec(shape):
    nd = len(shape)
    return pl.BlockSpec(shape, lambda *_: (0,) * nd, pipeline_mode=pl.Buffered(1))


def _params(*sem):
    return pltpu.CompilerParams(dimension_semantics=sem, vmem_limit_bytes=VMEM_LIMIT)


def _inproj_kernel(x_ref, g_ref, cos_ref, sin_ref, wz_ref, wxbc_ref, wq_ref, wk_ref, wvdt_ref,
                   cw_ref, cb_ref, dtb_ref,
                   zg_ref, xc_ref, dtt_ref, q_ref, k_ref, vt_ref, halo_ref, *, tm, tiles_per_seq):
    @pl.when(pl.program_id(0) % tiles_per_seq == 0)
    def _():
        halo_ref[...] = jnp.zeros_like(halo_ref)

    vregs = ROW_GROUP // SUBLANES
    sub = lax.broadcasted_iota(jnp.int32, (1, SUBLANES, 1), 1)
    reps = LANES // (DIFF_HEAD_DIM // 2)
    lane = lax.broadcasted_iota(jnp.int32, (ROW_GROUP, LANES), 1)
    history = {}

    def row_group(rows):
        h = _rms(x_ref[rows, :], g_ref[...]).astype(BF16)

        def gate(z_half):
            zg_ref[rows, :] = _silu_of_half(z_half).astype(BF16)

        def conv(c, xbc):
            cols = slice(c * CONV_COLS, (c + 1) * CONV_COLS)
            xb = xbc.reshape(vregs, SUBLANES, CONV_COLS)
            prev = history[c] if c in history else halo_ref[:, cols]
            w0, w1, w2, w3 = [0.5 * cw_ref[j:j + 1, cols] for j in range(CONV_WIDTH)]

            def shift(a, a_prev, k):
                r = pltpu.roll(a, k, 1)
                r_first = pltpu.roll(a_prev, k, 0)[None]
                return jnp.where(sub >= k, r, jnp.concatenate([r_first, r[:vregs - 1]], axis=0))

            x1 = shift(xb, prev, 1)
            pair = w1 * xb + w0 * x1
            pair_prev = w1 * prev + w0 * pltpu.roll(prev, 1, 0)
            acc = 0.5 * cb_ref[:, cols] + w3 * xb + w2 * x1 + shift(pair, pair_prev, 2)
            history[c] = xb[vregs - 1]
            xc_ref[rows, cols] = _silu_of_half(acc.reshape(ROW_GROUP, CONV_COLS)).astype(BF16)

        def values_and_steps(r):
            vt_ref[:, rows] = r[:D_DIFF].astype(BF16)
            dtt_ref[:, rows] = jax.nn.softplus(r[D_DIFF:] + dtb_ref[...])

        tables = []

        def rope(a, c):
            if not tables:
                cos = jnp.concatenate([cos_ref[:, rows]] * reps, axis=0).T
                sin = jnp.concatenate([sin_ref[:, rows]] * reps, axis=0).T
                tables.extend([cos, jnp.where(lane < LANES // 2, -sin, sin)])
            ac = a[:, c * LANES:(c + 1) * LANES]
            return ac * tables[0] + pltpu.roll(ac, LANES // 2, 1) * tables[1]

        def keys(a):
            for c in range(DIFF_HEADS):
                k_ref[rows, c * LANES:(c + 1) * LANES] = rope(a, c).astype(BF16)

        def queries(a):
            for c in range(DIFF_HEADS):
                q_ref[rows, c * LANES:(c + 1) * LANES] = (rope(a, c) * (DIFF_HEAD_DIM ** -0.5 * LOG2E)).astype(BF16)

        light = [(lambda: _dot_nt(wvdt_ref[...], h), values_and_steps), (lambda: _dot(h, wz_ref[...]), gate),
                 (lambda: _dot(h, wk_ref[...]), keys), (lambda: _dot(h, wq_ref[...]), queries)]
        for c in range(D_XBC // CONV_COLS):
            for matmul, epilogue in ((functools.partial(lambda c: _dot(h, wxbc_ref[:, c * CONV_COLS:(c + 1) * CONV_COLS]), c),
                                      functools.partial(conv, c)), light[c]):
                result = matmul()
                yield
                epilogue(result)

    for _ in itertools.zip_longest(*[row_group(slice(r, r + ROW_GROUP)) for r in range(0, tm, ROW_GROUP)]):
        pass
    for c, last_rows in history.items():
        halo_ref[:, c * CONV_COLS:(c + 1) * CONV_COLS] = last_rows


def _inproj(x2, g, cos, sin, wz, wxbc, wq, wk, wvdt, cw, cb, dtb_col, S, tm):
    T = x2.shape[0]
    row = lambda n: pl.BlockSpec((tm, n), lambda i: (i, 0))
    col = lambda n: pl.BlockSpec((n, tm), lambda i: (0, i))
    return pl.pallas_call(
        functools.partial(_inproj_kernel, tm=tm, tiles_per_seq=S // tm),
        grid=(T // tm,),
        in_specs=[row(D_MODEL), _const_spec((1, D_MODEL)), col(cos.shape[0]), col(sin.shape[0]),
                  _const_spec(wz.shape), _const_spec(wxbc.shape),
                  _const_spec(wq.shape), _const_spec(wk.shape), _const_spec(wvdt.shape),
                  _const_spec(cw.shape), _const_spec(cb.shape), _const_spec(dtb_col.shape)],
        out_specs=[row(D_SSD), row(D_XBC), col(DT_ROWS), row(D_DIFF), row(D_DIFF), col(D_DIFF)],
        out_shape=[jax.ShapeDtypeStruct((T, D_SSD), BF16),
                   jax.ShapeDtypeStruct((T, D_XBC), BF16),
                   jax.ShapeDtypeStruct((DT_ROWS, T), F32),
                   jax.ShapeDtypeStruct((T, D_DIFF), BF16),
                   jax.ShapeDtypeStruct((T, D_DIFF), BF16),
                   jax.ShapeDtypeStruct((D_DIFF, T), BF16)],
        scratch_shapes=[pltpu.VMEM((SUBLANES, D_XBC), F32)],
        compiler_params=_params("arbitrary"),
        name="inproj",
    )(x2, g, cos, sin, wz, wxbc, wq, wk, wvdt, cw, cb, dtb_col)


def _ssd_chunk(xc_ref, zg_ref, dt_t, alog_ref, dskip_ref, nw_ref, y_ref, state_ref, lane, causal, tri_t):
    da_t = dt_t * (-LOG2E * jnp.exp(alog_ref[...]))
    hi = da_t.astype(BF16)
    rem = da_t - hi.astype(F32)
    mid = rem.astype(BF16)
    lo = (rem - mid.astype(F32)).astype(BF16)
    cum_t = _dot(hi, tri_t) + _dot(mid, tri_t) + _dot(lo, tri_t)
    yield
    last = cum_t[:, CHUNK - 1:CHUNK]
    wrow_t = dt_t * jnp.exp2(last - cum_t)
    cdec = jnp.exp2(last)
    src_t = cum_t - jnp.log2(dt_t)
    cum = jnp.concatenate([cum_t, jnp.zeros((CHUNK - DT_ROWS, CHUNK), F32)], axis=0).T
    yield

    gn = SSD_GROUPS * SSD_STATE
    pairs_per_group = SSD_HEADS // SSD_GROUPS // 2
    y_pairs = []
    for g in range(SSD_GROUPS):
        bg = xc_ref[:, D_SSD + g * SSD_STATE:D_SSD + (g + 1) * SSD_STATE]
        cg = xc_ref[:, D_SSD + gn + g * SSD_STATE:D_SSD + gn + (g + 1) * SSD_STATE]
        cbm = _dot_nt(cg, bg)
        bgt = bg.astype(F32).T
        cg32 = cg.astype(F32)
        yield
        for jp in range(pairs_per_group):
            j = g * pairs_per_group + jp
            xs_pair = xc_ref[:, j * LANES:(j + 1) * LANES]
            lhs, rhs = [], []
            for half in range(2):
                hd = 2 * j + half
                colb = jnp.broadcast_to(cum[:, hd:hd + 1], (CHUNK, LANES))
                dec = jnp.where(causal, jnp.exp2(colb - src_t[hd:hd + 1, :]), 0.0)
                wp = (cbm * dec).astype(BF16)
                gg = (cg32 * jnp.exp2(colb)).astype(BF16)
                in_half = (lane >= SSD_HEAD_DIM) if half else (lane < SSD_HEAD_DIM)
                xs_m = jnp.where(in_half, xs_pair, jnp.zeros_like(xs_pair))
                st = state_ref[hd]
                lhs += [wp, gg]
                rhs += [xs_m, st.astype(BF16)]
                l2 = (bgt * wrow_t[hd:hd + 1, :]).astype(BF16)
                state_ref[hd] = st * cdec[hd:hd + 1, :] + _dot(l2, xs_m)
                yield
            y_pairs.append(_dot(jnp.concatenate(lhs, axis=1), jnp.concatenate(rhs, axis=0)))

    y = jnp.concatenate(y_pairs, axis=1) + dskip_ref[...] * xc_ref[:, :D_SSD].astype(F32)
    y = y * zg_ref[...].astype(F32)
    y_ref[...] = _rms(y, nw_ref[...]).astype(BF16)


def _ssd_kernel(xc_ref, zg_ref, *rest):
    dtt_refs = rest[:SSD_SEQS_PER_STEP]
    alog_ref, dskip_ref, nw_ref, y_ref, state_ref = rest[SSD_SEQS_PER_STEP:]

    @pl.when(pl.program_id(1) == 0)
    def _():
        state_ref[...] = jnp.zeros_like(state_ref)

    row = lax.broadcasted_iota(jnp.int32, (CHUNK, LANES), 0)
    lane = lax.broadcasted_iota(jnp.int32, (CHUNK, LANES), 1)
    causal = row >= lane
    tri_t = jnp.where(row <= lane, 1.0, 0.0).astype(BF16)
    for sc in range(SSD_CHUNKS_PER_STEP):
        t = pl.ds(sc * CHUNK, CHUNK)
        chunks = [_ssd_chunk(xc_ref.at[0, p, t], zg_ref.at[0, p, t], dtt_refs[p][:, sc * CHUNK:(sc + 1) * CHUNK],
                             alog_ref, dskip_ref, nw_ref, y_ref.at[0, p, t], state_ref.at[p], lane, causal, tri_t)
                  for p in range(SSD_SEQS_PER_STEP)]
        for _ in itertools.zip_longest(*chunks):
            pass


def _ssd(xc, zg, dtt, alog_col, dskip, nw, B, S):
    span = SSD_CHUNKS_PER_STEP * CHUNK
    nc = S // span
    nb = SSD_SEQS_PER_STEP
    seqs = lambda a: a.reshape(B // nb, nb, S, a.shape[-1])
    row = lambda n: pl.BlockSpec((1, nb, span, n), lambda b, c: (b, 0, c, 0))
    y = pl.pallas_call(
        _ssd_kernel,
        grid=(B // nb, nc),
        in_specs=[row(D_XBC), row(D_SSD)]
        + [pl.BlockSpec((DT_ROWS, span), functools.partial(lambda p, b, c: (0, (b * nb + p) * nc + c), p))
           for p in range(nb)]
        + [_const_spec(alog_col.shape), _const_spec(dskip.shape), _const_spec(nw.shape)],
        out_specs=row(D_SSD),
        out_shape=jax.ShapeDtypeStruct((B // nb, nb, S, D_SSD), BF16),
        scratch_shapes=[pltpu.VMEM((nb, SSD_HEADS, SSD_STATE, LANES), F32)],
        compiler_params=_params("parallel", "arbitrary"),
        name="ssd",
    )(seqs(xc), seqs(zg), *([dtt] * nb), alog_col, dskip, nw)
    return y.reshape(B * S, D_SSD)


def _diffattn_kernel(lam_ref, subw_ref, q_ref, k_ref, vt_ref, o_ref, sa_scr, sb_scr, acc_scr, m_scr, qc_scr, *,
                     tq, tk, nq, heads):
    hd = 2 * DIFF_HEAD_DIM
    ones = jnp.ones((ONES_ROWS, tk), BF16)
    krow = lax.broadcasted_iota(jnp.int32, (tk, tk), 0)
    qcol = lax.broadcasted_iota(jnp.int32, (tk, tk), 1)
    tri = krow <= qcol
    lane = lax.broadcasted_iota(jnp.int32, (tk, hd), 1)
    comp0 = (lane % DIFF_HEAD_DIM) < DIFF_HEAD_DIM // 2
    lv = lam_ref[...]
    lam = (jnp.exp(jnp.sum(lv[0:1] * lv[1:2], axis=-1, keepdims=True))
           - jnp.exp(jnp.sum(lv[2:3] * lv[3:4], axis=-1, keepdims=True)) + LAMBDA_INIT)

    steps = []
    for i in range(nq):
        steps += [(i, kb, "full") for kb in range(2 * i)] + [(i, 2 * i, "diag0"), (i, 2 * i + 1, "diag1")]

    def query_cols(kind):
        return (tk, tq) if kind == "diag1" else (0, tq)

    def head(hh):
        acc = acc_scr.at[hh]
        mx = m_scr.at[hh]
        bufs = (sa_scr.at[hh], sb_scr.at[hh])
        feat = slice(hh * hd, (hh + 1) * hd)
        qc = qc_scr.at[hh]
        for r in range(0, nq * tq, tk):
            qblk = q_ref[r:r + tk, feat]
            qc[0, r:r + tk, :] = jnp.where(comp0, qblk, jnp.zeros_like(qblk))
            qc[1, r:r + tk, :] = jnp.where(comp0, jnp.zeros_like(qblk), qblk)

        def scores(step, s_ref):
            i, kb, kind = step
            lo, hi = query_cols(kind)
            kblk = k_ref[kb * tk:(kb + 1) * tk, feat]
            for c in range(2):
                s = _dot_nt(kblk, qc[c, i * tq + lo:i * tq + hi, :])
                if kind == "diag0":
                    s = jnp.concatenate([jnp.where(tri, s[:, :tk], -jnp.inf), s[:, tk:]], axis=1)
                elif kind == "diag1":
                    s = jnp.where(tri, s, -jnp.inf)
                s_ref[c, :, lo:hi] = s

        def softmax_pv(step, s_ref):
            i, kb, kind = step
            lo, hi = query_cols(kind)
            par = i % 2
            lhs = jnp.concatenate([vt_ref[feat, kb * tk:(kb + 1) * tk], ones], axis=0)
            for c in range(2):
                m_blk = jnp.max(s_ref[c, :, lo:hi], axis=0, keepdims=True)
                if kb == 0:
                    m_new = m_blk
                    acc[par, c, :, lo:hi] = _dot(lhs, jnp.exp2(s_ref[c, :, lo:hi] - m_new).astype(BF16))
                else:
                    m_old = mx[par, c, :, lo:hi]
                    m_new = jnp.maximum(m_old, m_blk)
                    alpha = jnp.exp2(m_old - m_new)
                    p = jnp.exp2(s_ref[c, :, lo:hi] - m_new).astype(BF16)
                    acc[par, c, :, lo:hi] = alpha * acc[par, c, :, lo:hi] + _dot(lhs, p)
                mx[par, c, :, lo:hi] = m_new

        def finish(i):
            par = i % 2
            a0 = acc[par, 0]
            a1 = acc[par, 1]
            ot = a0[:hd] * (1.0 / a0[hd:hd + 1]) - a1[:hd] * (lam / a1[hd:hd + 1])
            ms = jnp.mean(ot * ot, axis=0, keepdims=True)
            ot = ot * (lax.rsqrt(ms + NORM_EPS) * (1.0 - LAMBDA_INIT)) * subw_ref[...]
            o_ref[i * tq:(i + 1) * tq, feat] = ot.T.astype(BF16)

        scores(steps[0], bufs[0])
        yield
        for n, step in enumerate(steps):
            if n + 1 < len(steps):
                scores(steps[n + 1], bufs[(n + 1) % 2])
                yield
            softmax_pv(step, bufs[n % 2])
            yield
            if step[2] == "diag1":
                finish(step[0])

    for _ in itertools.zip_longest(*[head(hh) for hh in range(heads)]):
        pass


def _diffattn(lamv, subw_col, q, k, vt, B, S, tq):
    nq = S // tq
    tk = tq // 2
    hd = 2 * DIFF_HEAD_DIM
    hp = DIFF_HEADS_PER_STEP
    return pl.pallas_call(
        functools.partial(_diffattn_kernel, tq=tq, tk=tk, nq=nq, heads=hp),
        grid=(B, DIFF_HEADS // hp),
        in_specs=[_const_spec(lamv.shape), _const_spec(subw_col.shape),
                  pl.BlockSpec((S, hd * hp), lambda b, h: (b, h)),
                  pl.BlockSpec((S, hd * hp), lambda b, h: (b, h)),
                  pl.BlockSpec((hd * hp, S), lambda b, h: (h, b))],
        out_specs=pl.BlockSpec((S, hd * hp), lambda b, h: (b, h)),
        out_shape=jax.ShapeDtypeStruct((B * S, D_DIFF), BF16),
        scratch_shapes=[pltpu.VMEM((hp, 2, tk, tq), F32),
                        pltpu.VMEM((hp, 2, tk, tq), F32),
                        pltpu.VMEM((hp, 2, 2, hd + ONES_ROWS, tq), F32),
                        pltpu.VMEM((hp, 2, 2, 1, tq), F32),
                        pltpu.VMEM((hp, 2, S, hd), BF16)],
        compiler_params=_params("parallel", "parallel"),
        name="diffattn",
    )(lamv, subw_col, q, k, vt)


def _memkv_kernel(mem_ref, g_ref, wk_ref, wv_ref, k_ref, v_ref):
    h = _rms(mem_ref[...], g_ref[...]).astype(BF16)
    k_ref[...] = _dot(h, wk_ref[...]).astype(BF16)
    v_ref[...] = _dot(h, wv_ref[...]).astype(BF16)


def _memkv(mem2, g, wk, wv, M):
    R = mem2.shape[0]
    row = pl.BlockSpec((M, D_MODEL), lambda i: (i, 0))
    return pl.pallas_call(
        _memkv_kernel,
        grid=(R // M,),
        in_specs=[row, _const_spec(g.shape), _const_spec(wk.shape), _const_spec(wv.shape)],
        out_specs=[row, row],
        out_shape=[jax.ShapeDtypeStruct((R, D_MODEL), BF16)] * 2,
        compiler_params=_params("parallel"),
        name="memkv",
    )(mem2, g, wk, wv)


def _mixmem_rows(rows, ys_ref, yd_ref, x_ref, km_ref, vm_ref, wo_s_ref, wo_d_ref, gmix_ref, gq_ref, wmq_ref,
                 wmo_ref, gmem_ref, x2_ref):
    mixed = _dot(ys_ref[rows, :], wo_s_ref[...]) + _dot(yd_ref[rows, :], wo_d_ref[...])
    yield
    x1 = x_ref[rows, :] + _rms(mixed, gmix_ref[...])
    qm = _dot(_rms(x1, gq_ref[...]).astype(BF16), wmq_ref[...])
    yield
    qm = (qm * (MEM_HEAD_DIM ** -0.5)).astype(BF16)
    outs = []
    for hd in range(MEM_HEADS):
        sl = slice(hd * MEM_HEAD_DIM, (hd + 1) * MEM_HEAD_DIM)
        s = _dot_nt(qm[:, sl], km_ref[:, sl])
        yield
        e = jnp.exp(s - jnp.max(s, axis=-1, keepdims=True))
        p = e / jnp.sum(e, axis=-1, keepdims=True)
        outs.append(_dot(p.astype(BF16), vm_ref[:, sl]).astype(BF16))
        yield
    c = _dot(jnp.concatenate(outs, axis=1), wmo_ref[...])
    yield
    x2_ref[rows, :] = x1 + _rms(c, gmem_ref[...])


def _mixmem_kernel(*refs):
    tm = refs[0].shape[0]
    groups = [_mixmem_rows(slice(r, r + ROW_GROUP), *refs) for r in range(0, tm, ROW_GROUP)]
    for _ in itertools.zip_longest(*groups):
        pass


def _mixmem(ys, yd, x2, km, vm, wo_s, wo_d, gmix, gq, wmq, wmo, gmem, S, M, tm):
    T = x2.shape[0]
    per_b = S // tm
    row = lambda n: pl.BlockSpec((tm, n), lambda i: (i, 0))
    mem = pl.BlockSpec((M, D_MODEL), lambda i: (i // per_b, 0))
    consts = (wo_s, wo_d, gmix, gq, wmq, wmo, gmem)
    return pl.pallas_call(
        _mixmem_kernel,
        grid=(T // tm,),
        in_specs=[row(D_SSD), row(D_DIFF), row(D_MODEL), mem, mem] + [_const_spec(c.shape) for c in consts],
        out_specs=row(D_MODEL),
        out_shape=jax.ShapeDtypeStruct((T, D_MODEL), F32),
        compiler_params=_params("parallel"),
        name="mixmem",
    )(ys, yd, x2, km, vm, *consts)


def _mlp_rows(rows, tf, x_ref, gpre_ref, wup_ref, wdn_ref, gpost_ref, o_ref):
    x = x_ref[rows, :]
    h = _rms(x, gpre_ref[...]).astype(BF16)
    acc = jnp.zeros(x.shape, F32)
    for c in range(D_FF // tf):
        u = _dot(h, wup_ref[:, c * tf:(c + 1) * tf])
        yield
        u = jnp.maximum(u, 0.0)
        acc = acc + _dot((u * u).astype(BF16), wdn_ref[c * tf:(c + 1) * tf, :])
        yield
    o_ref[rows, :] = x + _rms(acc, gpost_ref[...])


def _mlp_kernel(*refs, tf):
    tm = refs[0].shape[0]
    groups = [_mlp_rows(slice(r, r + ROW_GROUP), tf, *refs) for r in range(0, tm, ROW_GROUP)]
    for _ in itertools.zip_longest(*groups):
        pass


def _mlp(x2, gpre, wup, wdn, gpost, tm, tf):
    T = x2.shape[0]
    row = pl.BlockSpec((tm, D_MODEL), lambda i: (i, 0))
    return pl.pallas_call(
        functools.partial(_mlp_kernel, tf=tf),
        grid=(T // tm,),
        in_specs=[row, _const_spec(gpre.shape), _const_spec(wup.shape), _const_spec(wdn.shape),
                  _const_spec(gpost.shape)],
        out_specs=row,
        out_shape=jax.ShapeDtypeStruct((T, D_MODEL), F32),
        compiler_params=_params("parallel"),
        name="mlp",
    )(x2, gpre, wup, wdn, gpost)


def _rope_tables(positions):
    inv = ROPE_THETA ** (-jnp.arange(0, DIFF_HEAD_DIM, 2, dtype=F32) / DIFF_HEAD_DIM)
    ang = inv.reshape(-1, 1) * positions.astype(F32).reshape(1, -1)
    return jnp.cos(ang), jnp.sin(ang)


def _head_lane_order(wcols):
    half = DIFF_HEAD_DIM // 2
    rows = wcols.shape[0]
    return wcols.reshape(rows, DIFF_HEADS, 2, 2, half).transpose(0, 1, 3, 2, 4).reshape(rows, D_DIFF)


def kernel(x, mem, positions, norm_mix_pre, norm_mix_post, norm_mem_q, norm_mem_kv, norm_mem_post,
           norm_mlp_pre, norm_mlp_post, w_in, conv_w, conv_b, dt_bias, a_log, d_skip, ssd_norm_w,
           lambda_q1, lambda_k1, lambda_q2, lambda_k2, subln_w, w_out, w_mq, w_mk, w_mv, w_mo, w_up, w_down):
    B, S, _ = x.shape
    M = mem.shape[1]
    T = B * S
    assert norm_mix_pre.shape[0] == 1, "single-layer trunk"
    x2 = x.reshape(T, D_MODEL)
    cos, sin = _rope_tables(positions)

    w = w_in[0]
    o0 = D_SSD
    o1 = o0 + D_XBC
    o2 = o1 + SSD_HEADS
    o3 = o2 + D_DIFF
    o4 = o3 + D_DIFF
    wz = (0.5 * w[:, :o0]).astype(BF16)
    wxbc = w[:, o0:o1].astype(BF16)
    wq = _head_lane_order(w[:, o2:o3]).astype(BF16)
    wk = _head_lane_order(w[:, o3:o4]).astype(BF16)
    head_pad = ((0, DT_ROWS - SSD_HEADS), (0, 0))
    wvdt = jnp.concatenate([w[:, o4:].T, jnp.pad(w[:, o1:o2].T, head_pad)], axis=0).astype(BF16)
    head_col = lambda p: jnp.pad(p.reshape(SSD_HEADS, 1), head_pad)

    zg, xc, dtt, q, k, vt = _inproj(x2, norm_mix_pre, cos, sin, wz, wxbc, wq, wk, wvdt,
                                    conv_w[0, :, 0, :], conv_b, head_col(dt_bias), S, tm=512)

    y_ssd = _ssd(xc, zg, dtt, head_col(a_log), jnp.repeat(d_skip, SSD_HEAD_DIM, axis=1), ssd_norm_w, B, S)

    lamv = jnp.concatenate([lambda_q1, lambda_k1, lambda_q2, lambda_k2], axis=0)
    y_diff = _diffattn(lamv, subln_w.reshape(2 * DIFF_HEAD_DIM, 1), q, k, vt, B, S, tq=512)

    wo = w_out[0].astype(BF16)
    km, vm = _memkv(mem.reshape(B * M, D_MODEL), norm_mem_kv, w_mk[0].astype(BF16), w_mv[0].astype(BF16), M)
    x2b = _mixmem(y_ssd, y_diff, x2, km, vm, wo[:D_SSD], wo[D_SSD:], norm_mix_post, norm_mem_q,
                  w_mq[0].astype(BF16), w_mo[0].astype(BF16), norm_mem_post, S, M, tm=1024)
    out = _mlp(x2b, norm_mlp_pre, w_up[0].astype(BF16), w_down[0].astype(BF16), norm_mlp_post, tm=1024, tf=1024)
    return out.reshape(B, S, D_MODEL)
```

```python
import functools
import itertools
import math

import jax
import jax.numpy as jnp
from jax import lax
from jax.experimental import pallas as pl
from jax.experimental.pallas import tpu as pltpu

F32 = jnp.float32
BF16 = jnp.bfloat16

D_MODEL = 1024
D_SSD = 512
SSD_HEAD_DIM = 64
SSD_HEADS = D_SSD // SSD_HEAD_DIM
SSD_GROUPS = 2
SSD_STATE = 128
CONV_WIDTH = 4
CHUNK = 128
D_XBC = D_SSD + 2 * SSD_GROUPS * SSD_STATE
D_DIFF = D_MODEL - D_SSD
DIFF_HEAD_DIM = 64
DIFF_HEADS = D_DIFF // (2 * DIFF_HEAD_DIM)
ROPE_THETA = 10000.0
MEM_HEADS = 4
MEM_HEAD_DIM = D_MODEL // MEM_HEADS
D_FF = 4 * D_MODEL
NORM_EPS = 1e-6
LAMBDA_INIT = 0.8 - 0.6 * math.exp(-0.3 * 0)

LOG2E = math.log2(math.e)
LANES = 128
SUBLANES = 8
ONES_ROWS = 16
DT_ROWS = 16
CONV_COLS = 256
SSD_SEQS_PER_STEP = 4
SSD_CHUNKS_PER_STEP = 4
DIFF_HEADS_PER_STEP = 2
ROW_GROUP = 256
VMEM_LIMIT = 56 * 1024 * 1024


def _dot(a, b):
    return jnp.dot(a, b, preferred_element_type=F32)


def _dot_nt(a, b):
    return lax.dot_general(a, b, (((1,), (1,)), ((), ())), preferred_element_type=F32)


def _rms(x, g):
    ms = jnp.mean(x * x, axis=-1, keepdims=True)
    return x * lax.rsqrt(ms + NORM_EPS) * g


def _silu_of_half(half):
    return half + half * jnp.tanh(half)


def _const_spec(shape):
    nd = len(shape)
    return pl.BlockSpec(shape, lambda *_: (0,) * nd, pipeline_mode=pl.Buffered(1))


def _params(*sem):
    return pltpu.CompilerParams(dimension_semantics=sem, vmem_limit_bytes=VMEM_LIMIT)


def _inproj_kernel(x_ref, g_ref, cos_ref, sin_ref, wz_ref, wxbc_ref, wq_ref, wk_ref, wvdt_ref,
                   cw_ref, cb_ref, dtb_ref,
                   zg_ref, xc_ref, dtt_ref, q_ref, k_ref, vt_ref, halo_ref, *, tm, tiles_per_seq):
    @pl.when(pl.program_id(0) % tiles_per_seq == 0)
    def _():
        halo_ref[...] = jnp.zeros_like(halo_ref)

    vregs = ROW_GROUP // SUBLANES
    sub = lax.broadcasted_iota(jnp.int32, (1, SUBLANES, 1), 1)
    reps = LANES // (DIFF_HEAD_DIM // 2)
    lane = lax.broadcasted_iota(jnp.int32, (ROW_GROUP, LANES), 1)
    history = {}

    def row_group(rows):
        h = _rms(x_ref[rows, :], g_ref[...]).astype(BF16)

        def gate(z_half):
            zg_ref[rows, :] = _silu_of_half(z_half).astype(BF16)

        def conv(c, xbc):
            cols = slice(c * CONV_COLS, (c + 1) * CONV_COLS)
            xb = xbc.reshape(vregs, SUBLANES, CONV_COLS)
            prev = history[c] if c in history else halo_ref[:, cols]
            w0, w1, w2, w3 = [0.5 * cw_ref[j:j + 1, cols] for j in range(CONV_WIDTH)]

            def shift(a, a_prev, k):
                r = pltpu.roll(a, k, 1)
                r_first = pltpu.roll(a_prev, k, 0)[None]
                return jnp.where(sub >= k, r, jnp.concatenate([r_first, r[:vregs - 1]], axis=0))

            x1 = shift(xb, prev, 1)
            pair = w1 * xb + w0 * x1
            pair_prev = w1 * prev + w0 * pltpu.roll(prev, 1, 0)
            acc = 0.5 * cb_ref[:, cols] + w3 * xb + w2 * x1 + shift(pair, pair_prev, 2)
            history[c] = xb[vregs - 1]
            xc_ref[rows, cols] = _silu_of_half(acc.reshape(ROW_GROUP, CONV_COLS)).astype(BF16)

        def values_and_steps(r):
            vt_ref[:, rows] = r[:D_DIFF].astype(BF16)
            dtt_ref[:, rows] = jax.nn.softplus(r[D_DIFF:] + dtb_ref[...])

        tables = []

        def rope(a, c):
            if not tables:
                cos = jnp.concatenate([cos_ref[:, rows]] * reps, axis=0).T
                sin = jnp.concatenate([sin_ref[:, rows]] * reps, axis=0).T
                tables.extend([cos, jnp.where(lane < LANES // 2, -sin, sin)])
            ac = a[:, c * LANES:(c + 1) * LANES]
            return ac * tables[0] + pltpu.roll(ac, LANES // 2, 1) * tables[1]

        def keys(a):
            for c in range(DIFF_HEADS):
                k_ref[rows, c * LANES:(c + 1) * LANES] = rope(a, c).astype(BF16)

        def queries(a):
            for c in range(DIFF_HEADS):
                q_ref[rows, c * LANES:(c + 1) * LANES] = (rope(a, c) * (DIFF_HEAD_DIM ** -0.5 * LOG2E)).astype(BF16)

        light = [(lambda: _dot_nt(wvdt_ref[...], h), values_and_steps), (lambda: _dot(h, wz_ref[...]), gate),
                 (lambda: _dot(h, wk_ref[...]), keys), (lambda: _dot(h, wq_ref[...]), queries)]
        for c in range(D_XBC // CONV_COLS):
            for matmul, epilogue in ((functools.partial(lambda c: _dot(h, wxbc_ref[:, c * CONV_COLS:(c + 1) * CONV_COLS]), c),
                                      functools.partial(conv, c)), light[c]):
                result = matmul()
                yield
                epilogue(result)

    for _ in itertools.zip_longest(*[row_group(slice(r, r + ROW_GROUP)) for r in range(0, tm, ROW_GROUP)]):
        pass
    for c, last_rows in history.items():
        halo_ref[:, c * CONV_COLS:(c + 1) * CONV_COLS] = last_rows


def _inproj(x2, g, cos, sin, wz, wxbc, wq, wk, wvdt, cw, cb, dtb_col, S, tm):
    T = x2.shape[0]
    row = lambda n: pl.BlockSpec((tm, n), lambda i: (i, 0))
    col = lambda n: pl.BlockSpec((n, tm), lambda i: (0, i))
    return pl.pallas_call(
        functools.partial(_inproj_kernel, tm=tm, tiles_per_seq=S // tm),
        grid=(T // tm,),
        in_specs=[row(D_MODEL), _const_spec((1, D_MODEL)), col(cos.shape[0]), col(sin.shape[0]),
                  _const_spec(wz.shape), _const_spec(wxbc.shape),
                  _const_spec(wq.shape), _const_spec(wk.shape), _const_spec(wvdt.shape),
                  _const_spec(cw.shape), _const_spec(cb.shape), _const_spec(dtb_col.shape)],
        out_specs=[row(D_SSD), row(D_XBC), col(DT_ROWS), row(D_DIFF), row(D_DIFF), col(D_DIFF)],
        out_shape=[jax.ShapeDtypeStruct((T, D_SSD), BF16),
                   jax.ShapeDtypeStruct((T, D_XBC), BF16),
                   jax.ShapeDtypeStruct((DT_ROWS, T), F32),
                   jax.ShapeDtypeStruct((T, D_DIFF), BF16),
                   jax.ShapeDtypeStruct((T, D_DIFF), BF16),
                   jax.ShapeDtypeStruct((D_DIFF, T), BF16)],
        scratch_shapes=[pltpu.VMEM((SUBLANES, D_XBC), F32)],
        compiler_params=_params("arbitrary"),
        name="inproj",
    )(x2, g, cos, sin, wz, wxbc, wq, wk, wvdt, cw, cb, dtb_col)


def _ssd_chunk(xc_ref, zg_ref, dt_t, alog_ref, dskip_ref, nw_ref, y_ref, state_ref, lane, causal, tri_t):
    da_t = dt_t * (-LOG2E * jnp.exp(alog_ref[...]))
    hi = da_t.astype(BF16)
    rem = da_t - hi.astype(F32)
    mid = rem.astype(BF16)
    lo = (rem - mid.astype(F32)).astype(BF16)
    cum_t = _dot(hi, tri_t) + _dot(mid, tri_t) + _dot(lo, tri_t)
    yield
    last = cum_t[:, CHUNK - 1:CHUNK]
    wrow_t = dt_t * jnp.exp2(last - cum_t)
    cdec = jnp.exp2(last)
    src_t = cum_t - jnp.log2(dt_t)
    cum = jnp.concatenate([cum_t, jnp.zeros((CHUNK - DT_ROWS, CHUNK), F32)], axis=0).T
    yield

    gn = SSD_GROUPS * SSD_STATE
    pairs_per_group = SSD_HEADS // SSD_GROUPS // 2
    y_pairs = []
    for g in range(SSD_GROUPS):
        bg = xc_ref[:, D_SSD + g * SSD_STATE:D_SSD + (g + 1) * SSD_STATE]
        cg = xc_ref[:, D_SSD + gn + g * SSD_STATE:D_SSD + gn + (g + 1) * SSD_STATE]
        cbm = _dot_nt(cg, bg)
        bgt = bg.astype(F32).T
        cg32 = cg.astype(F32)
        yield
        for jp in range(pairs_per_group):
            j = g * pairs_per_group + jp
            xs_pair = xc_ref[:, j * LANES:(j + 1) * LANES]
            lhs, rhs = [], []
            for half in range(2):
                hd = 2 * j + half
                colb = jnp.broadcast_to(cum[:, hd:hd + 1], (CHUNK, LANES))
                dec = jnp.where(causal, jnp.exp2(colb - src_t[hd:hd + 1, :]), 0.0)
                wp = (cbm * dec).astype(BF16)
                gg = (cg32 * jnp.exp2(colb)).astype(BF16)
                in_half = (lane >= SSD_HEAD_DIM) if half else (lane < SSD_HEAD_DIM)
                xs_m = jnp.where(in_half, xs_pair, jnp.zeros_like(xs_pair))
                st = state_ref[hd]
                lhs += [wp, gg]
                rhs += [xs_m, st.astype(BF16)]
                l2 = (bgt * wrow_t[hd:hd + 1, :]).astype(BF16)
                state_ref[hd] = st * cdec[hd:hd + 1, :] + _dot(l2, xs_m)
                yield
            y_pairs.append(_dot(jnp.concatenate(lhs, axis=1), jnp.concatenate(rhs, axis=0)))

    y = jnp.concatenate(y_pairs, axis=1) + dskip_ref[...] * xc_ref[:, :D_SSD].astype(F32)
    y = y * zg_ref[...].astype(F32)
    y_ref[...] = _rms(y, nw_ref[...]).astype(BF16)


def _ssd_kernel(xc_ref, zg_ref, *rest):
    dtt_refs = rest[:SSD_SEQS_PER_STEP]
    alog_ref, dskip_ref, nw_ref, y_ref, state_ref = rest[SSD_SEQS_PER_STEP:]

    @pl.when(pl.program_id(1) == 0)
    def _():
        state_ref[...] = jnp.zeros_like(state_ref)

    row = lax.broadcasted_iota(jnp.int32, (CHUNK, LANES), 0)
    lane = lax.broadcasted_iota(jnp.int32, (CHUNK, LANES), 1)
    causal = row >= lane
    tri_t = jnp.where(row <= lane, 1.0, 0.0).astype(BF16)
    for sc in range(SSD_CHUNKS_PER_STEP):
        t = pl.ds(sc * CHUNK, CHUNK)
        chunks = [_ssd_chunk(xc_ref.at[0, p, t], zg_ref.at[0, p, t], dtt_refs[p][:, sc * CHUNK:(sc + 1) * CHUNK],
                             alog_ref, dskip_ref, nw_ref, y_ref.at[0, p, t], state_ref.at[p], lane, causal, tri_t)
                  for p in range(SSD_SEQS_PER_STEP)]
        for _ in itertools.zip_longest(*chunks):
            pass


def _ssd(xc, zg, dtt, alog_col, dskip, nw, B, S):
    span = SSD_CHUNKS_PER_STEP * CHUNK
    nc = S // span
    nb = SSD_SEQS_PER_STEP
    seqs = lambda a: a.reshape(B // nb, nb, S, a.shape[-1])
    row = lambda n: pl.BlockSpec((1, nb, span, n), lambda b, c: (b, 0, c, 0))
    y = pl.pallas_call(
        _ssd_kernel,
        grid=(B // nb, nc),
        in_specs=[row(D_XBC), row(D_SSD)]
        + [pl.BlockSpec((DT_ROWS, span), functools.partial(lambda p, b, c: (0, (b * nb + p) * nc + c), p))
           for p in range(nb)]
        + [_const_spec(alog_col.shape), _const_spec(dskip.shape), _const_spec(nw.shape)],
        out_specs=row(D_SSD),
        out_shape=jax.ShapeDtypeStruct((B // nb, nb, S, D_SSD), BF16),
        scratch_shapes=[pltpu.VMEM((nb, SSD_HEADS, SSD_STATE, LANES), F32)],
        compiler_params=_params("parallel", "arbitrary"),
        name="ssd",
    )(seqs(xc), seqs(zg), *([dtt] * nb), alog_col, dskip, nw)
    return y.reshape(B * S, D_SSD)


def _diffattn_kernel(lam_ref, subw_ref, q_ref, k_ref, vt_ref, o_ref, sa_scr, sb_scr, acc_scr, m_scr, qc_scr, *,
                     tq, tk, nq, heads):
    hd = 2 * DIFF_HEAD_DIM
    ones = jnp.ones((ONES_ROWS, tk), BF16)
    krow = lax.broadcasted_iota(jnp.int32, (tk, tk), 0)
    qcol = lax.broadcasted_iota(jnp.int32, (tk, tk), 1)
    tri = krow <= qcol
    lane = lax.broadcasted_iota(jnp.int32, (tk, hd), 1)
    comp0 = (lane % DIFF_HEAD_DIM) < DIFF_HEAD_DIM // 2
    lv = lam_ref[...]
    lam = (jnp.exp(jnp.sum(lv[0:1] * lv[1:2], axis=-1, keepdims=True))
           - jnp.exp(jnp.sum(lv[2:3] * lv[3:4], axis=-1, keepdims=True)) + LAMBDA_INIT)

    steps = []
    for i in range(nq):
        steps += [(i, kb, "full") for kb in range(2 * i)] + [(i, 2 * i, "diag0"), (i, 2 * i + 1, "diag1")]

    def query_cols(kind):
        return (tk, tq) if kind == "diag1" else (0, tq)

    def head(hh):
        acc = acc_scr.at[hh]
        mx = m_scr.at[hh]
        bufs = (sa_scr.at[hh], sb_scr.at[hh])
        feat = slice(hh * hd, (hh + 1) * hd)
        qc = qc_scr.at[hh]
        for r in range(0, nq * tq, tk):
            qblk = q_ref[r:r + tk, feat]
            qc[0, r:r + tk, :] = jnp.where(comp0, qblk, jnp.zeros_like(qblk))
            qc[1, r:r + tk, :] = jnp.where(comp0, jnp.zeros_like(qblk), qblk)

        def scores(step, s_ref):
            i, kb, kind = step
            lo, hi = query_cols(kind)
            kblk = k_ref[kb * tk:(kb + 1) * tk, feat]
            for c in range(2):
                s = _dot_nt(kblk, qc[c, i * tq + lo:i * tq + hi, :])
                if kind == "diag0":
                    s = jnp.concatenate([jnp.where(tri, s[:, :tk], -jnp.inf), s[:, tk:]], axis=1)
                elif kind == "diag1":
                    s = jnp.where(tri, s, -jnp.inf)
                s_ref[c, :, lo:hi] = s

        def softmax_pv(step, s_ref):
            i, kb, kind = step
            lo, hi = query_cols(kind)
            par = i % 2
            lhs = jnp.concatenate([vt_ref[feat, kb * tk:(kb + 1) * tk], ones], axis=0)
            for c in range(2):
                m_blk = jnp.max(s_ref[c, :, lo:hi], axis=0, keepdims=True)
                if kb == 0:
                    m_new = m_blk
                    acc[par, c, :, lo:hi] = _dot(lhs, jnp.exp2(s_ref[c, :, lo:hi] - m_new).astype(BF16))
                else:
                    m_old = mx[par, c, :, lo:hi]
                    m_new = jnp.maximum(m_old, m_blk)
                    alpha = jnp.exp2(m_old - m_new)
                    p = jnp.exp2(s_ref[c, :, lo:hi] - m_new).astype(BF16)
                    acc[par, c, :, lo:hi] = alpha * acc[par, c, :, lo:hi] + _dot(lhs, p)
                mx[par, c, :, lo:hi] = m_new

        def finish(i):
            par = i % 2
            a0 = acc[par, 0]
            a1 = acc[par, 1]
            ot = a0[:hd] * (1.0 / a0[hd:hd + 1]) - a1[:hd] * (lam / a1[hd:hd + 1])
            ms = jnp.mean(ot * ot, axis=0, keepdims=True)
            ot = ot * (lax.rsqrt(ms + NORM_EPS) * (1.0 - LAMBDA_INIT)) * subw_ref[...]
            o_ref[i * tq:(i + 1) * tq, feat] = ot.T.astype(BF16)

        scores(steps[0], bufs[0])
        yield
        for n, step in enumerate(steps):
            if n + 1 < len(steps):
                scores(steps[n + 1], bufs[(n + 1) % 2])
                yield
            softmax_pv(step, bufs[n % 2])
            yield
            if step[2] == "diag1":
                finish(step[0])

    for _ in itertools.zip_longest(*[head(hh) for hh in range(heads)]):
        pass


def _diffattn(lamv, subw_col, q, k, vt, B, S, tq):
    nq = S // tq
    tk = tq // 2
    hd = 2 * DIFF_HEAD_DIM
    hp = DIFF_HEADS_PER_STEP
    return pl.pallas_call(
        functools.partial(_diffattn_kernel, tq=tq, tk=tk, nq=nq, heads=hp),
        grid=(B, DIFF_HEADS // hp),
        in_specs=[_const_spec(lamv.shape), _const_spec(subw_col.shape),
                  pl.BlockSpec((S, hd * hp), lambda b, h: (b, h)),
                  pl.BlockSpec((S, hd * hp), lambda b, h: (b, h)),
                  pl.BlockSpec((hd * hp, S), lambda b, h: (h, b))],
        out_specs=pl.BlockSpec((S, hd * hp), lambda b, h: (b, h)),
        out_shape=jax.ShapeDtypeStruct((B * S, D_DIFF), BF16),
        scratch_shapes=[pltpu.VMEM((hp, 2, tk, tq), F32),
                        pltpu.VMEM((hp, 2, tk, tq), F32),
                        pltpu.VMEM((hp, 2, 2, hd + ONES_ROWS, tq), F32),
                        pltpu.VMEM((hp, 2, 2, 1, tq), F32),
                        pltpu.VMEM((hp, 2, S, hd), BF16)],
        compiler_params=_params("parallel", "parallel"),
        name="diffattn",
    )(lamv, subw_col, q, k, vt)


def _memkv_kernel(mem_ref, g_ref, wk_ref, wv_ref, k_ref, v_ref):
    h = _rms(mem_ref[...], g_ref[...]).astype(BF16)
    k_ref[...] = _dot(h, wk_ref[...]).astype(BF16)
    v_ref[...] = _dot(h, wv_ref[...]).astype(BF16)


def _memkv(mem2, g, wk, wv, M):
    R = mem2.shape[0]
    row = pl.BlockSpec((M, D_MODEL), lambda i: (i, 0))
    return pl.pallas_call(
        _memkv_kernel,
        grid=(R // M,),
        in_specs=[row, _const_spec(g.shape), _const_spec(wk.shape), _const_spec(wv.shape)],
        out_specs=[row, row],
        out_shape=[jax.ShapeDtypeStruct((R, D_MODEL), BF16)] * 2,
        compiler_params=_params("parallel"),
        name="memkv",
    )(mem2, g, wk, wv)


def _mixmem_rows(rows, ys_ref, yd_ref, x_ref, km_ref, vm_ref, wo_s_ref, wo_d_ref, gmix_ref, gq_ref, wmq_ref,
                 wmo_ref, gmem_ref, x2_ref):
    mixed = _dot(ys_ref[rows, :], wo_s_ref[...]) + _dot(yd_ref[rows, :], wo_d_ref[...])
    yield
    x1 = x_ref[rows, :] + _rms(mixed, gmix_ref[...])
    qm = _dot(_rms(x1, gq_ref[...]).astype(BF16), wmq_ref[...])
    yield
    qm = (qm * (MEM_HEAD_DIM ** -0.5)).astype(BF16)
    outs = []
    for hd in range(MEM_HEADS):
        sl = slice(hd * MEM_HEAD_DIM, (hd + 1) * MEM_HEAD_DIM)
        s = _dot_nt(qm[:, sl], km_ref[:, sl])
        yield
        e = jnp.exp(s - jnp.max(s, axis=-1, keepdims=True))
        p = e / jnp.sum(e, axis=-1, keepdims=True)
        outs.append(_dot(p.astype(BF16), vm_ref[:, sl]).astype(BF16))
        yield
    c = _dot(jnp.concatenate(outs, axis=1), wmo_ref[...])
    yield
    x2_ref[rows, :] = x1 + _rms(c, gmem_ref[...])


def _mixmem_kernel(*refs):
    tm = refs[0].shape[0]
    groups = [_mixmem_rows(slice(r, r + ROW_GROUP), *refs) for r in range(0, tm, ROW_GROUP)]
    for _ in itertools.zip_longest(*groups):
        pass


def _mixmem(ys, yd, x2, km, vm, wo_s, wo_d, gmix, gq, wmq, wmo, gmem, S, M, tm):
    T = x2.shape[0]
    per_b = S // tm
    row = lambda n: pl.BlockSpec((tm, n), lambda i: (i, 0))
    mem = pl.BlockSpec((M, D_MODEL), lambda i: (i // per_b, 0))
    consts = (wo_s, wo_d, gmix, gq, wmq, wmo, gmem)
    return pl.pallas_call(
        _mixmem_kernel,
        grid=(T // tm,),
        in_specs=[row(D_SSD), row(D_DIFF), row(D_MODEL), mem, mem] + [_const_spec(c.shape) for c in consts],
        out_specs=row(D_MODEL),
        out_shape=jax.ShapeDtypeStruct((T, D_MODEL), F32),
        compiler_params=_params("parallel"),
        name="mixmem",
    )(ys, yd, x2, km, vm, *consts)


def _mlp_rows(rows, tf, x_ref, gpre_ref, wup_ref, wdn_ref, gpost_ref, o_ref):
    x = x_ref[rows, :]
    h = _rms(x, gpre_ref[...]).astype(BF16)
    acc = jnp.zeros(x.shape, F32)
    for c in range(D_FF // tf):
        u = _dot(h, wup_ref[:, c * tf:(c + 1) * tf])
        yield
        u = jnp.maximum(u, 0.0)
        acc = acc + _dot((u * u).astype(BF16), wdn_ref[c * tf:(c + 1) * tf, :])
        yield
    o_ref[rows, :] = x + _rms(acc, gpost_ref[...])


def _mlp_kernel(*refs, tf):
    tm = refs[0].shape[0]
    groups = [_mlp_rows(slice(r, r + ROW_GROUP), tf, *refs) for r in range(0, tm, ROW_GROUP)]
    for _ in itertools.zip_longest(*groups):
        pass


def _mlp(x2, gpre, wup, wdn, gpost, tm, tf):
    T = x2.shape[0]
    row = pl.BlockSpec((tm, D_MODEL), lambda i: (i, 0))
    return pl.pallas_call(
        functools.partial(_mlp_kernel, tf=tf),
        grid=(T // tm,),
        in_specs=[row, _const_spec(gpre.shape), _const_spec(wup.shape), _const_spec(wdn.shape),
                  _const_spec(gpost.shape)],
        out_specs=row,
        out_shape=jax.ShapeDtypeStruct((T, D_MODEL), F32),
        compiler_params=_params("parallel"),
        name="mlp",
    )(x2, gpre, wup, wdn, gpost)


def _rope_tables(positions):
    inv = ROPE_THETA ** (-jnp.arange(0, DIFF_HEAD_DIM, 2, dtype=F32) / DIFF_HEAD_DIM)
    ang = inv.reshape(-1, 1) * positions.astype(F32).reshape(1, -1)
    return jnp.cos(ang), jnp.sin(ang)


def _head_lane_order(wcols):
    half = DIFF_HEAD_DIM // 2
    rows = wcols.shape[0]
    return wcols.reshape(rows, DIFF_HEADS, 2, 2, half).transpose(0, 1, 3, 2, 4).reshape(rows, D_DIFF)


def kernel(x, mem, positions, norm_mix_pre, norm_mix_post, norm_mem_q, norm_mem_kv, norm_mem_post,
           norm_mlp_pre, norm_mlp_post, w_in, conv_w, conv_b, dt_bias, a_log, d_skip, ssd_norm_w,
           lambda_q1, lambda_k1, lambda_q2, lambda_k2, subln_w, w_out, w_mq, w_mk, w_mv, w_mo, w_up, w_down):
    B, S, _ = x.shape
    M = mem.shape[1]
    T = B * S
    assert norm_mix_pre.shape[0] == 1, "single-layer trunk"
    x2 = x.reshape(T, D_MODEL)
    cos, sin = _rope_tables(positions)

    o0 = D_SSD
    o1 = o0 + D_XBC
    o2 = o1 + SSD_HEADS
    o3 = o2 + D_DIFF
    o4 = o3 + D_DIFF
    w = w_in[0] * jnp.where(jnp.arange(w_in.shape[2]) < o0, 0.5, 1.0).astype(F32)
    wz = w[:, :o0].astype(BF16)
    wxbc = w[:, o0:o1].astype(BF16)
    wq = _head_lane_order(w[:, o2:o3]).astype(BF16)
    wk = _head_lane_order(w[:, o3:o4]).astype(BF16)
    head_pad = ((0, DT_ROWS - SSD_HEADS), (0, 0))
    wvdt = jnp.concatenate([w[:, o4:].T, jnp.pad(w[:, o1:o2].T, head_pad)], axis=0).astype(BF16)
    head_col = lambda p: jnp.pad(p.reshape(SSD_HEADS, 1), head_pad)

    zg, xc, dtt, q, k, vt = _inproj(x2, norm_mix_pre, cos, sin, wz, wxbc, wq, wk, wvdt,
                                    conv_w[0, :, 0, :], conv_b, head_col(dt_bias), S, tm=512)

    y_ssd = _ssd(xc, zg, dtt, head_col(a_log), jnp.repeat(d_skip, SSD_HEAD_DIM, axis=1), ssd_norm_w, B, S)

    lamv = jnp.concatenate([lambda_q1, lambda_k1, lambda_q2, lambda_k2], axis=0)
    y_diff = _diffattn(lamv, subln_w.reshape(2 * DIFF_HEAD_DIM, 1), q, k, vt, B, S, tq=512)

    wo = w_out[0].astype(BF16)
    km, vm = _memkv(mem.reshape(B * M, D_MODEL), norm_mem_kv, w_mk[0].astype(BF16), w_mv[0].astype(BF16), M)
    x2b = _mixmem(y_ssd, y_diff, x2, km, vm, wo[:D_SSD], wo[D_SSD:], norm_mix_post, norm_mem_q,
                  w_mq[0].astype(BF16), w_mo[0].astype(BF16), norm_mem_post, S, M, tm=1024)
    out = _mlp(x2b, norm_mlp_pre, w_up[0].astype(BF16), w_down[0].astype(BF16), norm_mlp_post, tm=1024, tf=1024)
    return out.reshape(B, S, D_MODEL)
```

```python
import functools
import itertools
import math

import jax
import jax.numpy as jnp
from jax import lax
from jax.experimental import pallas as pl
from jax.experimental.pallas import tpu as pltpu

F32 = jnp.float32
BF16 = jnp.bfloat16

D_MODEL = 1024
D_SSD = 512
SSD_HEAD_DIM = 64
SSD_HEADS = D_SSD // SSD_HEAD_DIM
SSD_GROUPS = 2
SSD_STATE = 128
CONV_WIDTH = 4
CHUNK = 128
D_XBC = D_SSD + 2 * SSD_GROUPS * SSD_STATE
D_DIFF = D_MODEL - D_SSD
DIFF_HEAD_DIM = 64
DIFF_HEADS = D_DIFF // (2 * DIFF_HEAD_DIM)
ROPE_THETA = 10000.0
MEM_HEADS = 4
MEM_HEAD_DIM = D_MODEL // MEM_HEADS
D_FF = 4 * D_MODEL
NORM_EPS = 1e-6
LAMBDA_INIT = 0.8 - 0.6 * math.exp(-0.3 * 0)

LOG2E = math.log2(math.e)
LANES = 128
SUBLANES = 8
ONES_ROWS = 16
DT_ROWS = 16
CONV_COLS = 256
SSD_SEQS_PER_STEP = 4
SSD_CHUNKS_PER_STEP = 4
DIFF_HEADS_PER_STEP = 2
ROW_GROUP = 256
VMEM_LIMIT = 56 * 1024 * 1024


def _dot(a, b):
    return jnp.dot(a, b, preferred_element_type=F32)


def _dot_nt(a, b):
    return lax.dot_general(a, b, (((1,), (1,)), ((), ())), preferred_element_type=F32)


def _rms(x, g):
    ms = jnp.mean(x * x, axis=-1, keepdims=True)
    return x * lax.rsqrt(ms + NORM_EPS) * g


def _silu_of_half(half):
    return half + half * jnp.tanh(half)


def _const_spec(shape):
    nd = len(shape)
    return pl.BlockSpec(shape, lambda *_: (0,) * nd, pipeline_mode=pl.Buffered(1))


def _params(*sem):
    return pltpu.CompilerParams(dimension_semantics=sem, vmem_limit_bytes=VMEM_LIMIT)


def _inproj_kernel(x_ref, g_ref, cos_ref, sin_ref, wz_ref, wxbc_ref, wq_ref, wk_ref, wvdt_ref,
                   cw_ref, cb_ref, dtb_ref,
                   zg_ref, xc_ref, dtt_ref, q_ref, k_ref, vt_ref, halo_ref, *, tm, tiles_per_seq):
    @pl.when(pl.program_id(0) % tiles_per_seq == 0)
    def _():
        halo_ref[...] = jnp.zeros_like(halo_ref)

    vregs = ROW_GROUP // SUBLANES
    sub = lax.broadcasted_iota(jnp.int32, (1, SUBLANES, 1), 1)
    reps = LANES // (DIFF_HEAD_DIM // 2)
    lane = lax.broadcasted_iota(jnp.int32, (ROW_GROUP, LANES), 1)
    history = {}

    def row_group(rows):
        h = _rms(x_ref[rows, :], g_ref[...]).astype(BF16)

        def gate(z):
            zg_ref[rows, :] = _silu_of_half(0.5 * z).astype(BF16)

        def conv(c, xbc):
            cols = slice(c * CONV_COLS, (c + 1) * CONV_COLS)
            xb = xbc.reshape(vregs, SUBLANES, CONV_COLS)
            prev = history[c] if c in history else halo_ref[:, cols]
            w0, w1, w2, w3 = [0.5 * cw_ref[j:j + 1, cols] for j in range(CONV_WIDTH)]

            def shift(a, a_prev, k):
                r = pltpu.roll(a, k, 1)
                r_first = pltpu.roll(a_prev, k, 0)[None]
                return jnp.where(sub >= k, r, jnp.concatenate([r_first, r[:vregs - 1]], axis=0))

            x1 = shift(xb, prev, 1)
            pair = w1 * xb + w0 * x1
            pair_prev = w1 * prev + w0 * pltpu.roll(prev, 1, 0)
            acc = 0.5 * cb_ref[:, cols] + w3 * xb + w2 * x1 + shift(pair, pair_prev, 2)
            history[c] = xb[vregs - 1]
            xc_ref[rows, cols] = _silu_of_half(acc.reshape(ROW_GROUP, CONV_COLS)).astype(BF16)

        def values_and_steps(r):
            vt_ref[:, rows] = r[:D_DIFF].astype(BF16)
            dtt_ref[:, rows] = jax.nn.softplus(r[D_DIFF:] + dtb_ref[...])

        tables = []

        def rope(a, c):
            if not tables:
                cos = jnp.concatenate([cos_ref[:, rows]] * reps, axis=0).T
                sin = jnp.concatenate([sin_ref[:, rows]] * reps, axis=0).T
                tables.extend([cos, jnp.where(lane < LANES // 2, -sin, sin)])
            ac = a[:, c * LANES:(c + 1) * LANES]
            return ac * tables[0] + pltpu.roll(ac, LANES // 2, 1) * tables[1]

        def keys(a):
            for c in range(DIFF_HEADS):
                k_ref[rows, c * LANES:(c + 1) * LANES] = rope(a, c).astype(BF16)

        def queries(a):
            for c in range(DIFF_HEADS):
                q_ref[rows, c * LANES:(c + 1) * LANES] = (rope(a, c) * (DIFF_HEAD_DIM ** -0.5 * LOG2E)).astype(BF16)

        light = [(lambda: _dot_nt(wvdt_ref[...], h), values_and_steps), (lambda: _dot(h, wz_ref[...]), gate),
                 (lambda: _dot(h, wk_ref[...]), keys), (lambda: _dot(h, wq_ref[...]), queries)]
        for c in range(D_XBC // CONV_COLS):
            for matmul, epilogue in ((functools.partial(lambda c: _dot(h, wxbc_ref[:, c * CONV_COLS:(c + 1) * CONV_COLS]), c),
                                      functools.partial(conv, c)), light[c]):
                result = matmul()
                yield
                epilogue(result)

    for _ in itertools.zip_longest(*[row_group(slice(r, r + ROW_GROUP)) for r in range(0, tm, ROW_GROUP)]):
        pass
    for c, last_rows in history.items():
        halo_ref[:, c * CONV_COLS:(c + 1) * CONV_COLS] = last_rows


def _inproj(x2, g, cos, sin, wz, wxbc, wq, wk, wvdt, cw, cb, dtb_col, S, tm):
    T = x2.shape[0]
    row = lambda n: pl.BlockSpec((tm, n), lambda i: (i, 0))
    col = lambda n: pl.BlockSpec((n, tm), lambda i: (0, i))
    return pl.pallas_call(
        functools.partial(_inproj_kernel, tm=tm, tiles_per_seq=S // tm),
        grid=(T // tm,),
        in_specs=[row(D_MODEL), _const_spec((1, D_MODEL)), col(cos.shape[0]), col(sin.shape[0]),
                  _const_spec(wz.shape), _const_spec(wxbc.shape),
                  _const_spec(wq.shape), _const_spec(wk.shape), _const_spec(wvdt.shape),
                  _const_spec(cw.shape), _const_spec(cb.shape), _const_spec(dtb_col.shape)],
        out_specs=[row(D_SSD), row(D_XBC), col(DT_ROWS), row(D_DIFF), row(D_DIFF), col(D_DIFF)],
        out_shape=[jax.ShapeDtypeStruct((T, D_SSD), BF16),
                   jax.ShapeDtypeStruct((T, D_XBC), BF16),
                   jax.ShapeDtypeStruct((DT_ROWS, T), F32),
                   jax.ShapeDtypeStruct((T, D_DIFF), BF16),
                   jax.ShapeDtypeStruct((T, D_DIFF), BF16),
                   jax.ShapeDtypeStruct((D_DIFF, T), BF16)],
        scratch_shapes=[pltpu.VMEM((SUBLANES, D_XBC), F32)],
        compiler_params=_params("arbitrary"),
        name="inproj",
    )(x2, g, cos, sin, wz, wxbc, wq, wk, wvdt, cw, cb, dtb_col)


def _ssd_chunk(xc_ref, zg_ref, dt_t, alog_ref, dskip_ref, nw_ref, y_ref, state_ref, lane, causal, tri_t):
    da_t = dt_t * (-LOG2E * jnp.exp(alog_ref[...]))
    hi = da_t.astype(BF16)
    rem = da_t - hi.astype(F32)
    mid = rem.astype(BF16)
    lo = (rem - mid.astype(F32)).astype(BF16)
    cum_t = _dot(hi, tri_t) + _dot(mid, tri_t) + _dot(lo, tri_t)
    yield
    last = cum_t[:, CHUNK - 1:CHUNK]
    wrow_t = dt_t * jnp.exp2(last - cum_t)
    cdec = jnp.exp2(last)
    src_t = cum_t - jnp.log2(dt_t)
    cum = jnp.concatenate([cum_t, jnp.zeros((CHUNK - DT_ROWS, CHUNK), F32)], axis=0).T
    yield

    gn = SSD_GROUPS * SSD_STATE
    pairs_per_group = SSD_HEADS // SSD_GROUPS // 2
    y_pairs = []
    for g in range(SSD_GROUPS):
        bg = xc_ref[:, D_SSD + g * SSD_STATE:D_SSD + (g + 1) * SSD_STATE]
        cg = xc_ref[:, D_SSD + gn + g * SSD_STATE:D_SSD + gn + (g + 1) * SSD_STATE]
        cbm = _dot_nt(cg, bg)
        bgt = bg.astype(F32).T
        cg32 = cg.astype(F32)
        yield
        for jp in range(pairs_per_group):
            j = g * pairs_per_group + jp
            xs_pair = xc_ref[:, j * LANES:(j + 1) * LANES]
            lhs, rhs = [], []
            for half in range(2):
                hd = 2 * j + half
                colb = jnp.broadcast_to(cum[:, hd:hd + 1], (CHUNK, LANES))
                dec = jnp.where(causal, jnp.exp2(colb - src_t[hd:hd + 1, :]), 0.0)
                wp = (cbm * dec).astype(BF16)
                gg = (cg32 * jnp.exp2(colb)).astype(BF16)
                in_half = (lane >= SSD_HEAD_DIM) if half else (lane < SSD_HEAD_DIM)
                xs_m = jnp.where(in_half, xs_pair, jnp.zeros_like(xs_pair))
                st = state_ref[hd]
                lhs += [wp, gg]
                rhs += [xs_m, st.astype(BF16)]
                l2 = (bgt * wrow_t[hd:hd + 1, :]).astype(BF16)
                state_ref[hd] = st * cdec[hd:hd + 1, :] + _dot(l2, xs_m)
                yield
            y_pairs.append(_dot(jnp.concatenate(lhs, axis=1), jnp.concatenate(rhs, axis=0)))

    y = jnp.concatenate(y_pairs, axis=1) + dskip_ref[...] * xc_ref[:, :D_SSD].astype(F32)
    y = y * zg_ref[...].astype(F32)
    y_ref[...] = _rms(y, nw_ref[...]).astype(BF16)


def _ssd_kernel(xc_ref, zg_ref, *rest):
    dtt_refs = rest[:SSD_SEQS_PER_STEP]
    alog_ref, dskip_ref, nw_ref, y_ref, state_ref = rest[SSD_SEQS_PER_STEP:]

    @pl.when(pl.program_id(1) == 0)
    def _():
        state_ref[...] = jnp.zeros_like(state_ref)

    row = lax.broadcasted_iota(jnp.int32, (CHUNK, LANES), 0)
    lane = lax.broadcasted_iota(jnp.int32, (CHUNK, LANES), 1)
    causal = row >= lane
    tri_t = jnp.where(row <= lane, 1.0, 0.0).astype(BF16)
    for sc in range(SSD_CHUNKS_PER_STEP):
        t = pl.ds(sc * CHUNK, CHUNK)
        chunks = [_ssd_chunk(xc_ref.at[0, p, t], zg_ref.at[0, p, t], dtt_refs[p][:, sc * CHUNK:(sc + 1) * CHUNK],
                             alog_ref, dskip_ref, nw_ref, y_ref.at[0, p, t], state_ref.at[p], lane, causal, tri_t)
                  for p in range(SSD_SEQS_PER_STEP)]
        for _ in itertools.zip_longest(*chunks):
            pass


def _ssd(xc, zg, dtt, alog_col, dskip, nw, B, S):
    span = SSD_CHUNKS_PER_STEP * CHUNK
    nc = S // span
    nb = SSD_SEQS_PER_STEP
    seqs = lambda a: a.reshape(B // nb, nb, S, a.shape[-1])
    row = lambda n: pl.BlockSpec((1, nb, span, n), lambda b, c: (b, 0, c, 0))
    y = pl.pallas_call(
        _ssd_kernel,
        grid=(B // nb, nc),
        in_specs=[row(D_XBC), row(D_SSD)]
        + [pl.BlockSpec((DT_ROWS, span), functools.partial(lambda p, b, c: (0, (b * nb + p) * nc + c), p))
           for p in range(nb)]
        + [_const_spec(alog_col.shape), _const_spec(dskip.shape), _const_spec(nw.shape)],
        out_specs=row(D_SSD),
        out_shape=jax.ShapeDtypeStruct((B // nb, nb, S, D_SSD), BF16),
        scratch_shapes=[pltpu.VMEM((nb, SSD_HEADS, SSD_STATE, LANES), F32)],
        compiler_params=_params("parallel", "arbitrary"),
        name="ssd",
    )(seqs(xc), seqs(zg), *([dtt] * nb), alog_col, dskip, nw)
    return y.reshape(B * S, D_SSD)


def _diffattn_kernel(lam_ref, subw_ref, q_ref, k_ref, vt_ref, o_ref, sa_scr, sb_scr, acc_scr, m_scr, qc_scr, *,
                     tq, tk, nq, heads):
    hd = 2 * DIFF_HEAD_DIM
    ones = jnp.ones((ONES_ROWS, tk), BF16)
    krow = lax.broadcasted_iota(jnp.int32, (tk, tk), 0)
    qcol = lax.broadcasted_iota(jnp.int32, (tk, tk), 1)
    tri = krow <= qcol
    lane = lax.broadcasted_iota(jnp.int32, (tk, hd), 1)
    comp0 = (lane % DIFF_HEAD_DIM) < DIFF_HEAD_DIM // 2
    lv = lam_ref[...]
    lam = (jnp.exp(jnp.sum(lv[0:1] * lv[1:2], axis=-1, keepdims=True))
           - jnp.exp(jnp.sum(lv[2:3] * lv[3:4], axis=-1, keepdims=True)) + LAMBDA_INIT)

    steps = []
    for i in range(nq):
        steps += [(i, kb, "full") for kb in range(2 * i)] + [(i, 2 * i, "diag0"), (i, 2 * i + 1, "diag1")]

    def query_cols(kind):
        return (tk, tq) if kind == "diag1" else (0, tq)

    def head(hh):
        acc = acc_scr.at[hh]
        mx = m_scr.at[hh]
        bufs = (sa_scr.at[hh], sb_scr.at[hh])
        feat = slice(hh * hd, (hh + 1) * hd)
        qc = qc_scr.at[hh]
        for r in range(0, nq * tq, tk):
            qblk = q_ref[r:r + tk, feat]
            qc[0, r:r + tk, :] = jnp.where(comp0, qblk, jnp.zeros_like(qblk))
            qc[1, r:r + tk, :] = jnp.where(comp0, jnp.zeros_like(qblk), qblk)

        def scores(step, s_ref):
            i, kb, kind = step
            lo, hi = query_cols(kind)
            kblk = k_ref[kb * tk:(kb + 1) * tk, feat]
            for c in range(2):
                s = _dot_nt(kblk, qc[c, i * tq + lo:i * tq + hi, :])
                if kind == "diag0":
                    s = jnp.concatenate([jnp.where(tri, s[:, :tk], -jnp.inf), s[:, tk:]], axis=1)
                elif kind == "diag1":
                    s = jnp.where(tri, s, -jnp.inf)
                s_ref[c, :, lo:hi] = s

        def softmax_pv(step, s_ref):
            i, kb, kind = step
            lo, hi = query_cols(kind)
            par = i % 2
            lhs = jnp.concatenate([vt_ref[feat, kb * tk:(kb + 1) * tk], ones], axis=0)
            for c in range(2):
                m_blk = jnp.max(s_ref[c, :, lo:hi], axis=0, keepdims=True)
                if kb == 0:
                    m_new = m_blk
                    acc[par, c, :, lo:hi] = _dot(lhs, jnp.exp2(s_ref[c, :, lo:hi] - m_new).astype(BF16))
                else:
                    m_old = mx[par, c, :, lo:hi]
                    m_new = jnp.maximum(m_old, m_blk)
                    alpha = jnp.exp2(m_old - m_new)
                    p = jnp.exp2(s_ref[c, :, lo:hi] - m_new).astype(BF16)
                    acc[par, c, :, lo:hi] = alpha * acc[par, c, :, lo:hi] + _dot(lhs, p)
                mx[par, c, :, lo:hi] = m_new

        def finish(i):
            par = i % 2
            a0 = acc[par, 0]
            a1 = acc[par, 1]
            ot = a0[:hd] * (1.0 / a0[hd:hd + 1]) - a1[:hd] * (lam / a1[hd:hd + 1])
            ms = jnp.mean(ot * ot, axis=0, keepdims=True)
            ot = ot * (lax.rsqrt(ms + NORM_EPS) * (1.0 - LAMBDA_INIT)) * subw_ref[...]
            o_ref[i * tq:(i + 1) * tq, feat] = ot.T.astype(BF16)

        scores(steps[0], bufs[0])
        yield
        for n, step in enumerate(steps):
            if n + 1 < len(steps):
                scores(steps[n + 1], bufs[(n + 1) % 2])
                yield
            softmax_pv(step, bufs[n % 2])
            yield
            if step[2] == "diag1":
                finish(step[0])

    for _ in itertools.zip_longest(*[head(hh) for hh in range(heads)]):
        pass


def _diffattn(lamv, subw_col, q, k, vt, B, S, tq):
    nq = S // tq
    tk = tq // 2
    hd = 2 * DIFF_HEAD_DIM
    hp = DIFF_HEADS_PER_STEP
    return pl.pallas_call(
        functools.partial(_diffattn_kernel, tq=tq, tk=tk, nq=nq, heads=hp),
        grid=(B, DIFF_HEADS // hp),
        in_specs=[_const_spec(lamv.shape), _const_spec(subw_col.shape),
                  pl.BlockSpec((S, hd * hp), lambda b, h: (b, h)),
                  pl.BlockSpec((S, hd * hp), lambda b, h: (b, h)),
                  pl.BlockSpec((hd * hp, S), lambda b, h: (h, b))],
        out_specs=pl.BlockSpec((S, hd * hp), lambda b, h: (b, h)),
        out_shape=jax.ShapeDtypeStruct((B * S, D_DIFF), BF16),
        scratch_shapes=[pltpu.VMEM((hp, 2, tk, tq), F32),
                        pltpu.VMEM((hp, 2, tk, tq), F32),
                        pltpu.VMEM((hp, 2, 2, hd + ONES_ROWS, tq), F32),
                        pltpu.VMEM((hp, 2, 2, 1, tq), F32),
                        pltpu.VMEM((hp, 2, S, hd), BF16)],
        compiler_params=_params("parallel", "parallel"),
        name="diffattn",
    )(lamv, subw_col, q, k, vt)


def _memkv_kernel(mem_ref, g_ref, wk_ref, wv_ref, k_ref, v_ref):
    h = _rms(mem_ref[...], g_ref[...]).astype(BF16)
    k_ref[...] = _dot(h, wk_ref[...]).astype(BF16)
    v_ref[...] = _dot(h, wv_ref[...]).astype(BF16)


def _memkv(mem2, g, wk, wv, M):
    R = mem2.shape[0]
    row = pl.BlockSpec((M, D_MODEL), lambda i: (i, 0))
    return pl.pallas_call(
        _memkv_kernel,
        grid=(R // M,),
        in_specs=[row, _const_spec(g.shape), _const_spec(wk.shape), _const_spec(wv.shape)],
        out_specs=[row, row],
        out_shape=[jax.ShapeDtypeStruct((R, D_MODEL), BF16)] * 2,
        compiler_params=_params("parallel"),
        name="memkv",
    )(mem2, g, wk, wv)


def _mixmem_rows(rows, ys_ref, yd_ref, x_ref, km_ref, vm_ref, wo_s_ref, wo_d_ref, gmix_ref, gq_ref, wmq_ref,
                 wmo_ref, gmem_ref, x2_ref):
    mixed = _dot(ys_ref[rows, :], wo_s_ref[...]) + _dot(yd_ref[rows, :], wo_d_ref[...])
    yield
    x1 = x_ref[rows, :] + _rms(mixed, gmix_ref[...])
    qm = _dot(_rms(x1, gq_ref[...]).astype(BF16), wmq_ref[...])
    yield
    qm = (qm * (MEM_HEAD_DIM ** -0.5)).astype(BF16)
    outs = []
    for hd in range(MEM_HEADS):
        sl = slice(hd * MEM_HEAD_DIM, (hd + 1) * MEM_HEAD_DIM)
        s = _dot_nt(qm[:, sl], km_ref[:, sl])
        yield
        e = jnp.exp(s - jnp.max(s, axis=-1, keepdims=True))
        p = e / jnp.sum(e, axis=-1, keepdims=True)
        outs.append(_dot(p.astype(BF16), vm_ref[:, sl]).astype(BF16))
        yield
    c = _dot(jnp.concatenate(outs, axis=1), wmo_ref[...])
    yield
    x2_ref[rows, :] = x1 + _rms(c, gmem_ref[...])


def _mixmem_kernel(*refs):
    tm = refs[0].shape[0]
    groups = [_mixmem_rows(slice(r, r + ROW_GROUP), *refs) for r in range(0, tm, ROW_GROUP)]
    for _ in itertools.zip_longest(*groups):
        pass


def _mixmem(ys, yd, x2, km, vm, wo_s, wo_d, gmix, gq, wmq, wmo, gmem, S, M, tm):
    T = x2.shape[0]
    per_b = S // tm
    row = lambda n: pl.BlockSpec((tm, n), lambda i: (i, 0))
    mem = pl.BlockSpec((M, D_MODEL), lambda i: (i // per_b, 0))
    consts = (wo_s, wo_d, gmix, gq, wmq, wmo, gmem)
    return pl.pallas_call(
        _mixmem_kernel,
        grid=(T // tm,),
        in_specs=[row(D_SSD), row(D_DIFF), row(D_MODEL), mem, mem] + [_const_spec(c.shape) for c in consts],
        out_specs=row(D_MODEL),
        out_shape=jax.ShapeDtypeStruct((T, D_MODEL), F32),
        compiler_params=_params("parallel"),
        name="mixmem",
    )(ys, yd, x2, km, vm, *consts)


def _mlp_rows(rows, tf, x_ref, gpre_ref, wup_ref, wdn_ref, gpost_ref, o_ref):
    x = x_ref[rows, :]
    h = _rms(x, gpre_ref[...]).astype(BF16)
    acc = jnp.zeros(x.shape, F32)
    for c in range(D_FF // tf):
        u = _dot(h, wup_ref[:, c * tf:(c + 1) * tf])
        yield
        u = jnp.maximum(u, 0.0)
        acc = acc + _dot((u * u).astype(BF16), wdn_ref[c * tf:(c + 1) * tf, :])
        yield
    o_ref[rows, :] = x + _rms(acc, gpost_ref[...])


def _mlp_kernel(*refs, tf):
    tm = refs[0].shape[0]
    groups = [_mlp_rows(slice(r, r + ROW_GROUP), tf, *refs) for r in range(0, tm, ROW_GROUP)]
    for _ in itertools.zip_longest(*groups):
        pass


def _mlp(x2, gpre, wup, wdn, gpost, tm, tf):
    T = x2.shape[0]
    row = pl.BlockSpec((tm, D_MODEL), lambda i: (i, 0))
    return pl.pallas_call(
        functools.partial(_mlp_kernel, tf=tf),
        grid=(T // tm,),
        in_specs=[row, _const_spec(gpre.shape), _const_spec(wup.shape), _const_spec(wdn.shape),
                  _const_spec(gpost.shape)],
        out_specs=row,
        out_shape=jax.ShapeDtypeStruct((T, D_MODEL), F32),
        compiler_params=_params("parallel"),
        name="mlp",
    )(x2, gpre, wup, wdn, gpost)


def _rope_tables(positions):
    inv = ROPE_THETA ** (-jnp.arange(0, DIFF_HEAD_DIM, 2, dtype=F32) / DIFF_HEAD_DIM)
    ang = inv.reshape(-1, 1) * positions.astype(F32).reshape(1, -1)
    return jnp.cos(ang), jnp.sin(ang)


def _head_lane_order(wcols):
    half = DIFF_HEAD_DIM // 2
    rows = wcols.shape[0]
    return wcols.reshape(rows, DIFF_HEADS, 2, 2, half).transpose(0, 1, 3, 2, 4).reshape(rows, D_DIFF)


def kernel(x, mem, positions, norm_mix_pre, norm_mix_post, norm_mem_q, norm_mem_kv, norm_mem_post,
           norm_mlp_pre, norm_mlp_post, w_in, conv_w, conv_b, dt_bias, a_log, d_skip, ssd_norm_w,
           lambda_q1, lambda_k1, lambda_q2, lambda_k2, subln_w, w_out, w_mq, w_mk, w_mv, w_mo, w_up, w_down):
    B, S, _ = x.shape
    M = mem.shape[1]
    T = B * S
    assert norm_mix_pre.shape[0] == 1, "single-layer trunk"
    x2 = x.reshape(T, D_MODEL)
    cos, sin = _rope_tables(positions)

    o0 = D_SSD
    o1 = o0 + D_XBC
    o2 = o1 + SSD_HEADS
    o3 = o2 + D_DIFF
    o4 = o3 + D_DIFF
    w = w_in[0]
    wz = w[:, :o0].astype(BF16)
    wxbc = w[:, o0:o1].astype(BF16)
    wq = _head_lane_order(w[:, o2:o3]).astype(BF16)
    wk = _head_lane_order(w[:, o3:o4]).astype(BF16)
    head_pad = ((0, DT_ROWS - SSD_HEADS), (0, 0))
    wvdt = jnp.concatenate([w[:, o4:].T, jnp.pad(w[:, o1:o2].T, head_pad)], axis=0).astype(BF16)
    head_col = lambda p: jnp.pad(p.reshape(SSD_HEADS, 1), head_pad)

    zg, xc, dtt, q, k, vt = _inproj(x2, norm_mix_pre, cos, sin, wz, wxbc, wq, wk, wvdt,
                                    conv_w[0, :, 0, :], conv_b, head_col(dt_bias), S, tm=512)

    y_ssd = _ssd(xc, zg, dtt, head_col(a_log), jnp.repeat(d_skip, SSD_HEAD_DIM, axis=1), ssd_norm_w, B, S)

    lamv = jnp.concatenate([lambda_q1, lambda_k1, lambda_q2, lambda_k2], axis=0)
    y_diff = _diffattn(lamv, subln_w.reshape(2 * DIFF_HEAD_DIM, 1), q, k, vt, B, S, tq=512)

    wo = w_out[0].astype(BF16)
    km, vm = _memkv(mem.reshape(B * M, D_MODEL), norm_mem_kv, w_mk[0].astype(BF16), w_mv[0].astype(BF16), M)
    x2b = _mixmem(y_ssd, y_diff, x2, km, vm, wo[:D_SSD], wo[D_SSD:], norm_mix_post, norm_mem_q,
                  w_mq[0].astype(BF16), w_mo[0].astype(BF16), norm_mem_post, S, M, tm=1024)
    out = _mlp(x2b, norm_mlp_pre, w_up[0].astype(BF16), w_down[0].astype(BF16), norm_mlp_post, tm=1024, tf=1024)
    return out.reshape(B, S, D_MODEL)
```

```python
import functools
import itertools
import math

import jax
import jax.numpy as jnp
from jax import lax
from jax.experimental import pallas as pl
from jax.experimental.pallas import tpu as pltpu

F32 = jnp.float32
BF16 = jnp.bfloat16

D_MODEL = 1024
D_SSD = 512
SSD_HEAD_DIM = 64
SSD_HEADS = D_SSD // SSD_HEAD_DIM
SSD_GROUPS = 2
SSD_STATE = 128
CONV_WIDTH = 4
CHUNK = 128
D_XBC = D_SSD + 2 * SSD_GROUPS * SSD_STATE
D_DIFF = D_MODEL - D_SSD
DIFF_HEAD_DIM = 64
DIFF_HEADS = D_DIFF // (2 * DIFF_HEAD_DIM)
ROPE_THETA = 10000.0
MEM_HEADS = 4
MEM_HEAD_DIM = D_MODEL // MEM_HEADS
D_FF = 4 * D_MODEL
NORM_EPS = 1e-6
LAMBDA_INIT = 0.8 - 0.6 * math.exp(-0.3 * 0)

LOG2E = math.log2(math.e)
LANES = 128
SUBLANES = 8
ONES_ROWS = 16
DT_ROWS = 16
CONV_COLS = 256
SSD_SEQS_PER_STEP = 4
SSD_CHUNKS_PER_STEP = 4
DIFF_HEADS_PER_STEP = 2
ROW_GROUP = 256
INPROJ_ROWS = 512
ROW_TILE = 1024
DIFF_Q_BLOCK = 512
MLP_FF_CHUNK = 1024
VMEM_LIMIT = 56 * 1024 * 1024


def _dot(a, b):
    return jnp.dot(a, b, preferred_element_type=F32)


def _dot_nt(a, b):
    return lax.dot_general(a, b, (((1,), (1,)), ((), ())), preferred_element_type=F32)


def _rms(x, g):
    ms = jnp.mean(x * x, axis=-1, keepdims=True)
    return x * lax.rsqrt(ms + NORM_EPS) * g


def _silu_of_half(half):
    return half + half * jnp.tanh(half)


def _const_spec(shape):
    nd = len(shape)
    return pl.BlockSpec(shape, lambda *_: (0,) * nd, pipeline_mode=pl.Buffered(1))


def _params(*sem):
    return pltpu.CompilerParams(dimension_semantics=sem, vmem_limit_bytes=VMEM_LIMIT)


def _inproj_kernel(x_ref, g_ref, cos_ref, sin_ref, wz_ref, wxbc_ref, wq_ref, wk_ref, wvdt_ref,
                   cw_ref, cb_ref, dtb_ref,
                   zg_ref, xc_ref, dtt_ref, q_ref, k_ref, vt_ref, halo_ref, *, tm, tiles_per_seq):
    @pl.when(pl.program_id(0) % tiles_per_seq == 0)
    def _():
        halo_ref[...] = jnp.zeros_like(halo_ref)

    vregs = ROW_GROUP // SUBLANES
    sub = lax.broadcasted_iota(jnp.int32, (1, SUBLANES, 1), 1)
    reps = LANES // (DIFF_HEAD_DIM // 2)
    lane = lax.broadcasted_iota(jnp.int32, (ROW_GROUP, LANES), 1)
    history = {}

    def row_group(rows):
        h = _rms(x_ref[rows, :], g_ref[...]).astype(BF16)

        def gate(z):
            zg_ref[rows, :] = _silu_of_half(0.5 * z).astype(BF16)

        def conv(c, xbc):
            cols = slice(c * CONV_COLS, (c + 1) * CONV_COLS)
            xb = xbc.reshape(vregs, SUBLANES, CONV_COLS)
            prev = history[c] if c in history else halo_ref[:, cols]
            w0, w1, w2, w3 = [0.5 * cw_ref[j:j + 1, cols] for j in range(CONV_WIDTH)]

            def shift(a, a_prev, k):
                r = pltpu.roll(a, k, 1)
                r_first = pltpu.roll(a_prev, k, 0)[None]
                return jnp.where(sub >= k, r, jnp.concatenate([r_first, r[:vregs - 1]], axis=0))

            x1 = shift(xb, prev, 1)
            pair = w1 * xb + w0 * x1
            pair_prev = w1 * prev + w0 * pltpu.roll(prev, 1, 0)
            acc = 0.5 * cb_ref[:, cols] + w3 * xb + w2 * x1 + shift(pair, pair_prev, 2)
            history[c] = xb[vregs - 1]
            xc_ref[rows, cols] = _silu_of_half(acc.reshape(ROW_GROUP, CONV_COLS)).astype(BF16)

        def values_and_steps(r):
            vt_ref[:, rows] = r[:D_DIFF].astype(BF16)
            dtt_ref[:, rows] = jax.nn.softplus(r[D_DIFF:] + dtb_ref[...])

        tables = []

        def rope(a, c):
            if not tables:
                cos = jnp.concatenate([cos_ref[:, rows]] * reps, axis=0).T
                sin = jnp.concatenate([sin_ref[:, rows]] * reps, axis=0).T
                tables.extend([cos, jnp.where(lane < LANES // 2, -sin, sin)])
            ac = a[:, c * LANES:(c + 1) * LANES]
            return ac * tables[0] + pltpu.roll(ac, LANES // 2, 1) * tables[1]

        def keys(a):
            for c in range(DIFF_HEADS):
                k_ref[rows, c * LANES:(c + 1) * LANES] = rope(a, c).astype(BF16)

        def queries(a):
            for c in range(DIFF_HEADS):
                q_ref[rows, c * LANES:(c + 1) * LANES] = (rope(a, c) * (DIFF_HEAD_DIM ** -0.5 * LOG2E)).astype(BF16)

        light = [(lambda: _dot_nt(wvdt_ref[...], h), values_and_steps), (lambda: _dot(h, wz_ref[...]), gate),
                 (lambda: _dot(h, wk_ref[...]), keys), (lambda: _dot(h, wq_ref[...]), queries)]
        for c in range(D_XBC // CONV_COLS):
            for matmul, epilogue in ((functools.partial(lambda c: _dot(h, wxbc_ref[:, c * CONV_COLS:(c + 1) * CONV_COLS]), c),
                                      functools.partial(conv, c)), light[c]):
                result = matmul()
                yield
                epilogue(result)

    for _ in itertools.zip_longest(*[row_group(slice(r, r + ROW_GROUP)) for r in range(0, tm, ROW_GROUP)]):
        pass
    for c, last_rows in history.items():
        halo_ref[:, c * CONV_COLS:(c + 1) * CONV_COLS] = last_rows


def _inproj(x2, g, cos, sin, wz, wxbc, wq, wk, wvdt, cw, cb, dtb_col, S, tm):
    T = x2.shape[0]
    row = lambda n: pl.BlockSpec((tm, n), lambda i: (i, 0))
    col = lambda n: pl.BlockSpec((n, tm), lambda i: (0, i))
    return pl.pallas_call(
        functools.partial(_inproj_kernel, tm=tm, tiles_per_seq=S // tm),
        grid=(T // tm,),
        in_specs=[row(D_MODEL), _const_spec((1, D_MODEL)), col(cos.shape[0]), col(sin.shape[0]),
                  _const_spec(wz.shape), _const_spec(wxbc.shape),
                  _const_spec(wq.shape), _const_spec(wk.shape), _const_spec(wvdt.shape),
                  _const_spec(cw.shape), _const_spec(cb.shape), _const_spec(dtb_col.shape)],
        out_specs=[row(D_SSD), row(D_XBC), col(DT_ROWS), row(D_DIFF), row(D_DIFF), col(D_DIFF)],
        out_shape=[jax.ShapeDtypeStruct((T, D_SSD), BF16),
                   jax.ShapeDtypeStruct((T, D_XBC), BF16),
                   jax.ShapeDtypeStruct((DT_ROWS, T), F32),
                   jax.ShapeDtypeStruct((T, D_DIFF), BF16),
                   jax.ShapeDtypeStruct((T, D_DIFF), BF16),
                   jax.ShapeDtypeStruct((D_DIFF, T), BF16)],
        scratch_shapes=[pltpu.VMEM((SUBLANES, D_XBC), F32)],
        compiler_params=_params("arbitrary"),
        name="inproj",
    )(x2, g, cos, sin, wz, wxbc, wq, wk, wvdt, cw, cb, dtb_col)


def _ssd_chunk(xc_ref, zg_ref, dt_t, alog_ref, dskip_ref, nw_ref, y_ref, state_ref, lane, causal, tri_t):
    da_t = dt_t * (-LOG2E * jnp.exp(alog_ref[...]))
    hi = da_t.astype(BF16)
    rem = da_t - hi.astype(F32)
    mid = rem.astype(BF16)
    lo = (rem - mid.astype(F32)).astype(BF16)
    cum_t = _dot(hi, tri_t) + _dot(mid, tri_t) + _dot(lo, tri_t)
    yield
    last = cum_t[:, CHUNK - 1:CHUNK]
    wrow_t = dt_t * jnp.exp2(last - cum_t)
    cdec = jnp.exp2(last)
    src_t = cum_t - jnp.log2(dt_t)
    cum = jnp.concatenate([cum_t, jnp.zeros((CHUNK - DT_ROWS, CHUNK), F32)], axis=0).T
    yield

    gn = SSD_GROUPS * SSD_STATE
    pairs_per_group = SSD_HEADS // SSD_GROUPS // 2
    y_pairs = []
    for g in range(SSD_GROUPS):
        bg = xc_ref[:, D_SSD + g * SSD_STATE:D_SSD + (g + 1) * SSD_STATE]
        cg = xc_ref[:, D_SSD + gn + g * SSD_STATE:D_SSD + gn + (g + 1) * SSD_STATE]
        cbm = _dot_nt(cg, bg)
        bgt = bg.astype(F32).T
        cg32 = cg.astype(F32)
        yield
        for jp in range(pairs_per_group):
            j = g * pairs_per_group + jp
            xs_pair = xc_ref[:, j * LANES:(j + 1) * LANES]
            lhs, rhs = [], []
            for half in range(2):
                hd = 2 * j + half
                colb = jnp.broadcast_to(cum[:, hd:hd + 1], (CHUNK, LANES))
                dec = jnp.where(causal, jnp.exp2(colb - src_t[hd:hd + 1, :]), 0.0)
                wp = (cbm * dec).astype(BF16)
                gg = (cg32 * jnp.exp2(colb)).astype(BF16)
                in_half = (lane >= SSD_HEAD_DIM) if half else (lane < SSD_HEAD_DIM)
                xs_m = jnp.where(in_half, xs_pair, jnp.zeros_like(xs_pair))
                st = state_ref[hd]
                lhs += [wp, gg]
                rhs += [xs_m, st.astype(BF16)]
                l2 = (bgt * wrow_t[hd:hd + 1, :]).astype(BF16)
                state_ref[hd] = st * cdec[hd:hd + 1, :] + _dot(l2, xs_m)
                yield
            y_pairs.append(_dot(jnp.concatenate(lhs, axis=1), jnp.concatenate(rhs, axis=0)))

    y = jnp.concatenate(y_pairs, axis=1) + dskip_ref[...] * xc_ref[:, :D_SSD].astype(F32)
    y = y * zg_ref[...].astype(F32)
    y_ref[...] = _rms(y, nw_ref[...]).astype(BF16)


def _ssd_kernel(xc_ref, zg_ref, *rest):
    dtt_refs = rest[:SSD_SEQS_PER_STEP]
    alog_ref, dskip_ref, nw_ref, y_ref, state_ref = rest[SSD_SEQS_PER_STEP:]

    @pl.when(pl.program_id(1) == 0)
    def _():
        state_ref[...] = jnp.zeros_like(state_ref)

    row = lax.broadcasted_iota(jnp.int32, (CHUNK, LANES), 0)
    lane = lax.broadcasted_iota(jnp.int32, (CHUNK, LANES), 1)
    causal = row >= lane
    tri_t = jnp.where(row <= lane, 1.0, 0.0).astype(BF16)
    for sc in range(SSD_CHUNKS_PER_STEP):
        t = pl.ds(sc * CHUNK, CHUNK)
        chunks = [_ssd_chunk(xc_ref.at[0, p, t], zg_ref.at[0, p, t], dtt_refs[p][:, sc * CHUNK:(sc + 1) * CHUNK],
                             alog_ref, dskip_ref, nw_ref, y_ref.at[0, p, t], state_ref.at[p], lane, causal, tri_t)
                  for p in range(SSD_SEQS_PER_STEP)]
        for _ in itertools.zip_longest(*chunks):
            pass


def _ssd(xc, zg, dtt, alog_col, dskip, nw, B, S):
    span = SSD_CHUNKS_PER_STEP * CHUNK
    nc = S // span
    nb = SSD_SEQS_PER_STEP
    seqs = lambda a: a.reshape(B // nb, nb, S, a.shape[-1])
    row = lambda n: pl.BlockSpec((1, nb, span, n), lambda b, c: (b, 0, c, 0))
    y = pl.pallas_call(
        _ssd_kernel,
        grid=(B // nb, nc),
        in_specs=[row(D_XBC), row(D_SSD)]
        + [pl.BlockSpec((DT_ROWS, span), functools.partial(lambda p, b, c: (0, (b * nb + p) * nc + c), p))
           for p in range(nb)]
        + [_const_spec(alog_col.shape), _const_spec(dskip.shape), _const_spec(nw.shape)],
        out_specs=row(D_SSD),
        out_shape=jax.ShapeDtypeStruct((B // nb, nb, S, D_SSD), BF16),
        scratch_shapes=[pltpu.VMEM((nb, SSD_HEADS, SSD_STATE, LANES), F32)],
        compiler_params=_params("parallel", "arbitrary"),
        name="ssd",
    )(seqs(xc), seqs(zg), *([dtt] * nb), alog_col, dskip, nw)
    return y.reshape(B * S, D_SSD)


def _diffattn_kernel(lam_ref, subw_ref, q_ref, k_ref, vt_ref, o_ref, sa_scr, sb_scr, acc_scr, m_scr, qc_scr, *,
                     tq, tk, nq, heads):
    hd = 2 * DIFF_HEAD_DIM
    ones = jnp.ones((ONES_ROWS, tk), BF16)
    krow = lax.broadcasted_iota(jnp.int32, (tk, tk), 0)
    qcol = lax.broadcasted_iota(jnp.int32, (tk, tk), 1)
    tri = krow <= qcol
    lane = lax.broadcasted_iota(jnp.int32, (tk, hd), 1)
    comp0 = (lane % DIFF_HEAD_DIM) < DIFF_HEAD_DIM // 2
    lv = lam_ref[...]
    lam = (jnp.exp(jnp.sum(lv[0:1] * lv[1:2], axis=-1, keepdims=True))
           - jnp.exp(jnp.sum(lv[2:3] * lv[3:4], axis=-1, keepdims=True)) + LAMBDA_INIT)

    steps = []
    for i in range(nq):
        steps += [(i, kb, "full") for kb in range(2 * i)] + [(i, 2 * i, "diag0"), (i, 2 * i + 1, "diag1")]

    def query_cols(kind):
        return (tk, tq) if kind == "diag1" else (0, tq)

    def head(hh):
        acc = acc_scr.at[hh]
        mx = m_scr.at[hh]
        bufs = (sa_scr.at[hh], sb_scr.at[hh])
        feat = slice(hh * hd, (hh + 1) * hd)
        qc = qc_scr.at[hh]
        for r in range(0, nq * tq, tk):
            qblk = q_ref[r:r + tk, feat]
            qc[0, r:r + tk, :] = jnp.where(comp0, qblk, jnp.zeros_like(qblk))
            qc[1, r:r + tk, :] = jnp.where(comp0, jnp.zeros_like(qblk), qblk)

        def scores(step, s_ref):
            i, kb, kind = step
            lo, hi = query_cols(kind)
            kblk = k_ref[kb * tk:(kb + 1) * tk, feat]
            for c in range(2):
                s = _dot_nt(kblk, qc[c, i * tq + lo:i * tq + hi, :])
                if kind == "diag0":
                    s = jnp.concatenate([jnp.where(tri, s[:, :tk], -jnp.inf), s[:, tk:]], axis=1)
                elif kind == "diag1":
                    s = jnp.where(tri, s, -jnp.inf)
                s_ref[c, :, lo:hi] = s

        def softmax_pv(step, s_ref):
            i, kb, kind = step
            lo, hi = query_cols(kind)
            par = i % 2
            lhs = jnp.concatenate([vt_ref[feat, kb * tk:(kb + 1) * tk], ones], axis=0)
            for c in range(2):
                m_blk = jnp.max(s_ref[c, :, lo:hi], axis=0, keepdims=True)
                if kb == 0:
                    m_new = m_blk
                    acc[par, c, :, lo:hi] = _dot(lhs, jnp.exp2(s_ref[c, :, lo:hi] - m_new).astype(BF16))
                else:
                    m_old = mx[par, c, :, lo:hi]
                    m_new = jnp.maximum(m_old, m_blk)
                    alpha = jnp.exp2(m_old - m_new)
                    p = jnp.exp2(s_ref[c, :, lo:hi] - m_new).astype(BF16)
                    acc[par, c, :, lo:hi] = alpha * acc[par, c, :, lo:hi] + _dot(lhs, p)
                mx[par, c, :, lo:hi] = m_new

        def finish(i):
            par = i % 2
            a0 = acc[par, 0]
            a1 = acc[par, 1]
            ot = a0[:hd] * (1.0 / a0[hd:hd + 1]) - a1[:hd] * (lam / a1[hd:hd + 1])
            ms = jnp.mean(ot * ot, axis=0, keepdims=True)
            ot = ot * (lax.rsqrt(ms + NORM_EPS) * (1.0 - LAMBDA_INIT)) * subw_ref[...]
            o_ref[i * tq:(i + 1) * tq, feat] = ot.T.astype(BF16)

        scores(steps[0], bufs[0])
        yield
        for n, step in enumerate(steps):
            if n + 1 < len(steps):
                scores(steps[n + 1], bufs[(n + 1) % 2])
                yield
            softmax_pv(step, bufs[n % 2])
            yield
            if step[2] == "diag1":
                finish(step[0])

    for _ in itertools.zip_longest(*[head(hh) for hh in range(heads)]):
        pass


def _diffattn(lamv, subw_col, q, k, vt, B, S, tq):
    nq = S // tq
    tk = tq // 2
    hd = 2 * DIFF_HEAD_DIM
    hp = DIFF_HEADS_PER_STEP
    return pl.pallas_call(
        functools.partial(_diffattn_kernel, tq=tq, tk=tk, nq=nq, heads=hp),
        grid=(B, DIFF_HEADS // hp),
        in_specs=[_const_spec(lamv.shape), _const_spec(subw_col.shape),
                  pl.BlockSpec((S, hd * hp), lambda b, h: (b, h)),
                  pl.BlockSpec((S, hd * hp), lambda b, h: (b, h)),
                  pl.BlockSpec((hd * hp, S), lambda b, h: (h, b))],
        out_specs=pl.BlockSpec((S, hd * hp), lambda b, h: (b, h)),
        out_shape=jax.ShapeDtypeStruct((B * S, D_DIFF), BF16),
        scratch_shapes=[pltpu.VMEM((hp, 2, tk, tq), F32),
                        pltpu.VMEM((hp, 2, tk, tq), F32),
                        pltpu.VMEM((hp, 2, 2, hd + ONES_ROWS, tq), F32),
                        pltpu.VMEM((hp, 2, 2, 1, tq), F32),
                        pltpu.VMEM((hp, 2, S, hd), BF16)],
        compiler_params=_params("parallel", "parallel"),
        name="diffattn",
    )(lamv, subw_col, q, k, vt)


def _memkv_rows(rows, mem_ref, g_ref, wk_ref, wv_ref, k_ref, v_ref):
    h = _rms(mem_ref[rows, :], g_ref[...]).astype(BF16)
    k = _dot(h, wk_ref[...])
    yield
    k_ref[rows, :] = k.astype(BF16)
    v = _dot(h, wv_ref[...])
    yield
    v_ref[rows, :] = v.astype(BF16)


def _memkv_kernel(*refs):
    tm = refs[0].shape[0]
    groups = [_memkv_rows(slice(r, r + ROW_GROUP), *refs) for r in range(0, tm, ROW_GROUP)]
    for _ in itertools.zip_longest(*groups):
        pass


def _memkv(mem2, g, wk, wv, tm):
    R = mem2.shape[0]
    row = pl.BlockSpec((tm, D_MODEL), lambda i: (i, 0))
    return pl.pallas_call(
        _memkv_kernel,
        grid=(R // tm,),
        in_specs=[row, _const_spec(g.shape), _const_spec(wk.shape), _const_spec(wv.shape)],
        out_specs=[row, row],
        out_shape=[jax.ShapeDtypeStruct((R, D_MODEL), BF16)] * 2,
        compiler_params=_params("parallel"),
        name="memkv",
    )(mem2, g, wk, wv)


def _mixmem_rows(rows, ys_ref, yd_ref, x_ref, km_ref, vm_ref, wo_s_ref, wo_d_ref, gmix_ref, gq_ref, wmq_ref,
                 wmo_ref, gmem_ref, x2_ref):
    mixed = _dot(ys_ref[rows, :], wo_s_ref[...]) + _dot(yd_ref[rows, :], wo_d_ref[...])
    yield
    x1 = x_ref[rows, :] + _rms(mixed, gmix_ref[...])
    qm = _dot(_rms(x1, gq_ref[...]).astype(BF16), wmq_ref[...])
    yield
    qm = (qm * (MEM_HEAD_DIM ** -0.5)).astype(BF16)
    outs = []
    for hd in range(MEM_HEADS):
        sl = slice(hd * MEM_HEAD_DIM, (hd + 1) * MEM_HEAD_DIM)
        s = _dot_nt(qm[:, sl], km_ref[:, sl])
        yield
        e = jnp.exp(s - jnp.max(s, axis=-1, keepdims=True))
        p = e / jnp.sum(e, axis=-1, keepdims=True)
        outs.append(_dot(p.astype(BF16), vm_ref[:, sl]).astype(BF16))
        yield
    c = _dot(jnp.concatenate(outs, axis=1), wmo_ref[...])
    yield
    x2_ref[rows, :] = x1 + _rms(c, gmem_ref[...])


def _mixmem_kernel(*refs):
    tm = refs[0].shape[0]
    groups = [_mixmem_rows(slice(r, r + ROW_GROUP), *refs) for r in range(0, tm, ROW_GROUP)]
    for _ in itertools.zip_longest(*groups):
        pass


def _mixmem(ys, yd, x2, km, vm, wo_s, wo_d, gmix, gq, wmq, wmo, gmem, S, M, tm):
    T = x2.shape[0]
    per_b = S // tm
    row = lambda n: pl.BlockSpec((tm, n), lambda i: (i, 0))
    mem = pl.BlockSpec((M, D_MODEL), lambda i: (i // per_b, 0))
    consts = (wo_s, wo_d, gmix, gq, wmq, wmo, gmem)
    return pl.pallas_call(
        _mixmem_kernel,
        grid=(T // tm,),
        in_specs=[row(D_SSD), row(D_DIFF), row(D_MODEL), mem, mem] + [_const_spec(c.shape) for c in consts],
        out_specs=row(D_MODEL),
        out_shape=jax.ShapeDtypeStruct((T, D_MODEL), F32),
        compiler_params=_params("parallel"),
        name="mixmem",
    )(ys, yd, x2, km, vm, *consts)


def _mlp_rows(rows, tf, x_ref, gpre_ref, wup_ref, wdn_ref, gpost_ref, o_ref):
    x = x_ref[rows, :]
    h = _rms(x, gpre_ref[...]).astype(BF16)
    acc = jnp.zeros(x.shape, F32)
    for c in range(D_FF // tf):
        u = _dot(h, wup_ref[:, c * tf:(c + 1) * tf])
        yield
        u = jnp.maximum(u, 0.0)
        acc = acc + _dot((u * u).astype(BF16), wdn_ref[c * tf:(c + 1) * tf, :])
        yield
    o_ref[rows, :] = x + _rms(acc, gpost_ref[...])


def _mlp_kernel(*refs, tf):
    tm = refs[0].shape[0]
    groups = [_mlp_rows(slice(r, r + ROW_GROUP), tf, *refs) for r in range(0, tm, ROW_GROUP)]
    for _ in itertools.zip_longest(*groups):
        pass


def _mlp(x2, gpre, wup, wdn, gpost, tm, tf):
    T = x2.shape[0]
    row = pl.BlockSpec((tm, D_MODEL), lambda i: (i, 0))
    return pl.pallas_call(
        functools.partial(_mlp_kernel, tf=tf),
        grid=(T // tm,),
        in_specs=[row, _const_spec(gpre.shape), _const_spec(wup.shape), _const_spec(wdn.shape),
                  _const_spec(gpost.shape)],
        out_specs=row,
        out_shape=jax.ShapeDtypeStruct((T, D_MODEL), F32),
        compiler_params=_params("parallel"),
        name="mlp",
    )(x2, gpre, wup, wdn, gpost)


def _rope_tables(positions):
    inv = ROPE_THETA ** (-jnp.arange(0, DIFF_HEAD_DIM, 2, dtype=F32) / DIFF_HEAD_DIM)
    ang = inv.reshape(-1, 1) * positions.astype(F32).reshape(1, -1)
    return jnp.cos(ang), jnp.sin(ang)


def _head_lane_order(wcols):
    half = DIFF_HEAD_DIM // 2
    rows = wcols.shape[0]
    return wcols.reshape(rows, DIFF_HEADS, 2, 2, half).transpose(0, 1, 3, 2, 4).reshape(rows, D_DIFF)


def kernel(x, mem, positions, norm_mix_pre, norm_mix_post, norm_mem_q, norm_mem_kv, norm_mem_post,
           norm_mlp_pre, norm_mlp_post, w_in, conv_w, conv_b, dt_bias, a_log, d_skip, ssd_norm_w,
           lambda_q1, lambda_k1, lambda_q2, lambda_k2, subln_w, w_out, w_mq, w_mk, w_mv, w_mo, w_up, w_down):
    B, S, _ = x.shape
    M = mem.shape[1]
    T = B * S
    assert norm_mix_pre.shape[0] == 1, "single-layer trunk"
    x2 = x.reshape(T, D_MODEL)
    cos, sin = _rope_tables(positions)

    o0 = D_SSD
    o1 = o0 + D_XBC
    o2 = o1 + SSD_HEADS
    o3 = o2 + D_DIFF
    o4 = o3 + D_DIFF
    w = w_in[0]
    wz = w[:, :o0].astype(BF16)
    wxbc = w[:, o0:o1].astype(BF16)
    wq = _head_lane_order(w[:, o2:o3]).astype(BF16)
    wk = _head_lane_order(w[:, o3:o4]).astype(BF16)
    head_pad = ((0, DT_ROWS - SSD_HEADS), (0, 0))
    wvdt = jnp.concatenate([w[:, o4:].T, jnp.pad(w[:, o1:o2].T, head_pad)], axis=0).astype(BF16)
    head_col = lambda p: jnp.pad(p.reshape(SSD_HEADS, 1), head_pad)

    zg, xc, dtt, q, k, vt = _inproj(x2, norm_mix_pre, cos, sin, wz, wxbc, wq, wk, wvdt,
                                    conv_w[0, :, 0, :], conv_b, head_col(dt_bias), S, tm=INPROJ_ROWS)

    y_ssd = _ssd(xc, zg, dtt, head_col(a_log), jnp.repeat(d_skip, SSD_HEAD_DIM, axis=1), ssd_norm_w, B, S)

    lamv = jnp.concatenate([lambda_q1, lambda_k1, lambda_q2, lambda_k2], axis=0)
    y_diff = _diffattn(lamv, subln_w.reshape(2 * DIFF_HEAD_DIM, 1), q, k, vt, B, S, tq=DIFF_Q_BLOCK)

    wo = w_out[0].astype(BF16)
    km, vm = _memkv(mem.reshape(B * M, D_MODEL), norm_mem_kv, w_mk[0].astype(BF16), w_mv[0].astype(BF16),
                    tm=min(ROW_TILE, B * M))
    x2b = _mixmem(y_ssd, y_diff, x2, km, vm, wo[:D_SSD], wo[D_SSD:], norm_mix_post, norm_mem_q,
                  w_mq[0].astype(BF16), w_mo[0].astype(BF16), norm_mem_post, S, M, tm=ROW_TILE)
    out = _mlp(x2b, norm_mlp_pre, w_up[0].astype(BF16), w_down[0].astype(BF16), norm_mlp_post,
               tm=ROW_TILE, tf=MLP_FF_CHUNK)
    return out.reshape(B, S, D_MODEL)
```

```python
import functools
import itertools
import math

import jax
import jax.numpy as jnp
from jax import lax
from jax.experimental import pallas as pl
from jax.experimental.pallas import tpu as pltpu

F32 = jnp.float32
BF16 = jnp.bfloat16

D_MODEL = 1024
D_SSD = 512
SSD_HEAD_DIM = 64
SSD_HEADS = D_SSD // SSD_HEAD_DIM
SSD_GROUPS = 2
SSD_STATE = 128
CONV_WIDTH = 4
CHUNK = 128
D_XBC = D_SSD + 2 * SSD_GROUPS * SSD_STATE
D_DIFF = D_MODEL - D_SSD
DIFF_HEAD_DIM = 64
DIFF_HEADS = D_DIFF // (2 * DIFF_HEAD_DIM)
ROPE_THETA = 10000.0
MEM_HEADS = 4
MEM_HEAD_DIM = D_MODEL // MEM_HEADS
D_FF = 4 * D_MODEL
NORM_EPS = 1e-6
LAMBDA_INIT = 0.8 - 0.6 * math.exp(-0.3 * 0)

LOG2E = math.log2(math.e)
LANES = 128
SUBLANES = 8
ONES_ROWS = 16
DT_ROWS = 16
CONV_COLS = 256
SSD_SEQS_PER_STEP = 4
SSD_CHUNKS_PER_STEP = 4
DIFF_HEADS_PER_STEP = 2
ROW_GROUP = 256
INPROJ_ROWS = 512
ROW_TILE = 1024
DIFF_Q_BLOCK = 512
MLP_FF_CHUNK = 1024
WEIGHT_CAST_ROWS = 128
VMEM_LIMIT = 56 * 1024 * 1024


def _dot(a, b):
    return jnp.dot(a, b, preferred_element_type=F32)


def _dot_nt(a, b):
    return lax.dot_general(a, b, (((1,), (1,)), ((), ())), preferred_element_type=F32)


def _rms(x, g):
    ms = jnp.mean(x * x, axis=-1, keepdims=True)
    return x * lax.rsqrt(ms + NORM_EPS) * g


def _silu_of_half(half):
    return half + half * jnp.tanh(half)


def _const_spec(shape):
    nd = len(shape)
    return pl.BlockSpec(shape, lambda *_: (0,) * nd, pipeline_mode=pl.Buffered(1))


def _params(*sem):
    return pltpu.CompilerParams(dimension_semantics=sem, vmem_limit_bytes=VMEM_LIMIT)


def _inproj_kernel(x_ref, g_ref, cos_ref, sin_ref, wz_ref, wxbc_ref, wq_ref, wk_ref, wvdt_ref,
                   cw_ref, cb_ref, dtb_ref,
                   zg_ref, xc_ref, dtt_ref, q_ref, k_ref, vt_ref, halo_ref, *, tm, tiles_per_seq):
    @pl.when(pl.program_id(0) % tiles_per_seq == 0)
    def _():
        halo_ref[...] = jnp.zeros_like(halo_ref)

    vregs = ROW_GROUP // SUBLANES
    sub = lax.broadcasted_iota(jnp.int32, (1, SUBLANES, 1), 1)
    reps = LANES // (DIFF_HEAD_DIM // 2)
    lane = lax.broadcasted_iota(jnp.int32, (ROW_GROUP, LANES), 1)
    history = {}

    def row_group(rows):
        h = _rms(x_ref[rows, :], g_ref[...]).astype(BF16)

        def gate(z):
            zg_ref[rows, :] = _silu_of_half(0.5 * z).astype(BF16)

        def conv(c, xbc):
            cols = slice(c * CONV_COLS, (c + 1) * CONV_COLS)
            xb = xbc.reshape(vregs, SUBLANES, CONV_COLS)
            prev = history[c] if c in history else halo_ref[:, cols]
            w0, w1, w2, w3 = [0.5 * cw_ref[j:j + 1, cols] for j in range(CONV_WIDTH)]

            def shift(a, a_prev, k):
                r = pltpu.roll(a, k, 1)
                r_first = pltpu.roll(a_prev, k, 0)[None]
                return jnp.where(sub >= k, r, jnp.concatenate([r_first, r[:vregs - 1]], axis=0))

            x1 = shift(xb, prev, 1)
            pair = w1 * xb + w0 * x1
            pair_prev = w1 * prev + w0 * pltpu.roll(prev, 1, 0)
            acc = 0.5 * cb_ref[:, cols] + w3 * xb + w2 * x1 + shift(pair, pair_prev, 2)
            history[c] = xb[vregs - 1]
            xc_ref[rows, cols] = _silu_of_half(acc.reshape(ROW_GROUP, CONV_COLS)).astype(BF16)

        def values_and_steps(r):
            vt_ref[:, rows] = r[:D_DIFF].astype(BF16)
            dtt_ref[:, rows] = jax.nn.softplus(r[D_DIFF:] + dtb_ref[...])

        tables = []

        def rope(a, c):
            if not tables:
                cos = jnp.concatenate([cos_ref[:, rows]] * reps, axis=0).T
                sin = jnp.concatenate([sin_ref[:, rows]] * reps, axis=0).T
                tables.extend([cos, jnp.where(lane < LANES // 2, -sin, sin)])
            ac = a[:, c * LANES:(c + 1) * LANES]
            return ac * tables[0] + pltpu.roll(ac, LANES // 2, 1) * tables[1]

        def keys(a):
            for c in range(DIFF_HEADS):
                k_ref[rows, c * LANES:(c + 1) * LANES] = rope(a, c).astype(BF16)

        def queries(a):
            for c in range(DIFF_HEADS):
                q_ref[rows, c * LANES:(c + 1) * LANES] = (rope(a, c) * (DIFF_HEAD_DIM ** -0.5 * LOG2E)).astype(BF16)

        light = [(lambda: _dot_nt(wvdt_ref[...], h), values_and_steps), (lambda: _dot(h, wz_ref[...]), gate),
                 (lambda: _dot(h, wk_ref[...]), keys), (lambda: _dot(h, wq_ref[...]), queries)]
        for c in range(D_XBC // CONV_COLS):
            for matmul, epilogue in ((functools.partial(lambda c: _dot(h, wxbc_ref[:, c * CONV_COLS:(c + 1) * CONV_COLS]), c),
                                      functools.partial(conv, c)), light[c]):
                result = matmul()
                yield
                epilogue(result)

    for _ in itertools.zip_longest(*[row_group(slice(r, r + ROW_GROUP)) for r in range(0, tm, ROW_GROUP)]):
        pass
    for c, last_rows in history.items():
        halo_ref[:, c * CONV_COLS:(c + 1) * CONV_COLS] = last_rows


def _inproj(x2, g, cos, sin, wz, wxbc, wq, wk, wvdt, cw, cb, dtb_col, S, tm):
    T = x2.shape[0]
    row = lambda n: pl.BlockSpec((tm, n), lambda i: (i, 0))
    col = lambda n: pl.BlockSpec((n, tm), lambda i: (0, i))
    return pl.pallas_call(
        functools.partial(_inproj_kernel, tm=tm, tiles_per_seq=S // tm),
        grid=(T // tm,),
        in_specs=[row(D_MODEL), _const_spec((1, D_MODEL)), col(cos.shape[0]), col(sin.shape[0]),
                  _const_spec(wz.shape), _const_spec(wxbc.shape),
                  _const_spec(wq.shape), _const_spec(wk.shape), _const_spec(wvdt.shape),
                  _const_spec(cw.shape), _const_spec(cb.shape), _const_spec(dtb_col.shape)],
        out_specs=[row(D_SSD), row(D_XBC), col(DT_ROWS), row(D_DIFF), row(D_DIFF), col(D_DIFF)],
        out_shape=[jax.ShapeDtypeStruct((T, D_SSD), BF16),
                   jax.ShapeDtypeStruct((T, D_XBC), BF16),
                   jax.ShapeDtypeStruct((DT_ROWS, T), F32),
                   jax.ShapeDtypeStruct((T, D_DIFF), BF16),
                   jax.ShapeDtypeStruct((T, D_DIFF), BF16),
                   jax.ShapeDtypeStruct((D_DIFF, T), BF16)],
        scratch_shapes=[pltpu.VMEM((SUBLANES, D_XBC), F32)],
        compiler_params=_params("arbitrary"),
        name="inproj",
    )(x2, g, cos, sin, wz, wxbc, wq, wk, wvdt, cw, cb, dtb_col)


def _ssd_chunk(xc_ref, zg_ref, dt_t, alog_ref, dskip_ref, nw_ref, y_ref, state_ref, lane, causal, tri_t):
    da_t = dt_t * (-LOG2E * jnp.exp(alog_ref[...]))
    hi = da_t.astype(BF16)
    rem = da_t - hi.astype(F32)
    mid = rem.astype(BF16)
    lo = (rem - mid.astype(F32)).astype(BF16)
    cum_t = _dot(hi, tri_t) + _dot(mid, tri_t) + _dot(lo, tri_t)
    yield
    last = cum_t[:, CHUNK - 1:CHUNK]
    wrow_t = dt_t * jnp.exp2(last - cum_t)
    cdec = jnp.exp2(last)
    src_t = cum_t - jnp.log2(dt_t)
    cum = jnp.concatenate([cum_t, jnp.zeros((CHUNK - DT_ROWS, CHUNK), F32)], axis=0).T
    yield

    gn = SSD_GROUPS * SSD_STATE
    pairs_per_group = SSD_HEADS // SSD_GROUPS // 2
    y_pairs = []
    for g in range(SSD_GROUPS):
        bg = xc_ref[:, D_SSD + g * SSD_STATE:D_SSD + (g + 1) * SSD_STATE]
        cg = xc_ref[:, D_SSD + gn + g * SSD_STATE:D_SSD + gn + (g + 1) * SSD_STATE]
        cbm = _dot_nt(cg, bg)
        bgt = bg.astype(F32).T
        cg32 = cg.astype(F32)
        yield
        for jp in range(pairs_per_group):
            j = g * pairs_per_group + jp
            xs_pair = xc_ref[:, j * LANES:(j + 1) * LANES]
            lhs, rhs = [], []
            for half in range(2):
                hd = 2 * j + half
                colb = jnp.broadcast_to(cum[:, hd:hd + 1], (CHUNK, LANES))
                dec = jnp.where(causal, jnp.exp2(colb - src_t[hd:hd + 1, :]), 0.0)
                wp = (cbm * dec).astype(BF16)
                gg = (cg32 * jnp.exp2(colb)).astype(BF16)
                in_half = (lane >= SSD_HEAD_DIM) if half else (lane < SSD_HEAD_DIM)
                xs_m = jnp.where(in_half, xs_pair, jnp.zeros_like(xs_pair))
                st = state_ref[hd]
                lhs += [wp, gg]
                rhs += [xs_m, st.astype(BF16)]
                l2 = (bgt * wrow_t[hd:hd + 1, :]).astype(BF16)
                state_ref[hd] = st * cdec[hd:hd + 1, :] + _dot(l2, xs_m)
                yield
            y_pairs.append(_dot(jnp.concatenate(lhs, axis=1), jnp.concatenate(rhs, axis=0)))

    y = jnp.concatenate(y_pairs, axis=1) + dskip_ref[...] * xc_ref[:, :D_SSD].astype(F32)
    y = y * zg_ref[...].astype(F32)
    y_ref[...] = _rms(y, nw_ref[...]).astype(BF16)


def _ssd_kernel(xc_ref, zg_ref, *rest):
    dtt_refs = rest[:SSD_SEQS_PER_STEP]
    alog_ref, dskip_ref, nw_ref, y_ref, state_ref = rest[SSD_SEQS_PER_STEP:]

    @pl.when(pl.program_id(1) == 0)
    def _():
        state_ref[...] = jnp.zeros_like(state_ref)

    row = lax.broadcasted_iota(jnp.int32, (CHUNK, LANES), 0)
    lane = lax.broadcasted_iota(jnp.int32, (CHUNK, LANES), 1)
    causal = row >= lane
    tri_t = jnp.where(row <= lane, 1.0, 0.0).astype(BF16)
    for sc in range(SSD_CHUNKS_PER_STEP):
        t = pl.ds(sc * CHUNK, CHUNK)
        chunks = [_ssd_chunk(xc_ref.at[0, p, t], zg_ref.at[0, p, t], dtt_refs[p][:, sc * CHUNK:(sc + 1) * CHUNK],
                             alog_ref, dskip_ref, nw_ref, y_ref.at[0, p, t], state_ref.at[p], lane, causal, tri_t)
                  for p in range(SSD_SEQS_PER_STEP)]
        for _ in itertools.zip_longest(*chunks):
            pass


def _ssd(xc, zg, dtt, alog_col, dskip, nw, B, S):
    span = SSD_CHUNKS_PER_STEP * CHUNK
    nc = S // span
    nb = SSD_SEQS_PER_STEP
    seqs = lambda a: a.reshape(B // nb, nb, S, a.shape[-1])
    row = lambda n: pl.BlockSpec((1, nb, span, n), lambda b, c: (b, 0, c, 0))
    y = pl.pallas_call(
        _ssd_kernel,
        grid=(B // nb, nc),
        in_specs=[row(D_XBC), row(D_SSD)]
        + [pl.BlockSpec((DT_ROWS, span), functools.partial(lambda p, b, c: (0, (b * nb + p) * nc + c), p))
           for p in range(nb)]
        + [_const_spec(alog_col.shape), _const_spec(dskip.shape), _const_spec(nw.shape)],
        out_specs=row(D_SSD),
        out_shape=jax.ShapeDtypeStruct((B // nb, nb, S, D_SSD), BF16),
        scratch_shapes=[pltpu.VMEM((nb, SSD_HEADS, SSD_STATE, LANES), F32)],
        compiler_params=_params("parallel", "arbitrary"),
        name="ssd",
    )(seqs(xc), seqs(zg), *([dtt] * nb), alog_col, dskip, nw)
    return y.reshape(B * S, D_SSD)


def _diffattn_kernel(lam_ref, subw_ref, q_ref, k_ref, vt_ref, o_ref, sa_scr, sb_scr, acc_scr, m_scr, qc_scr, *,
                     tq, tk, nq, heads):
    hd = 2 * DIFF_HEAD_DIM
    ones = jnp.ones((ONES_ROWS, tk), BF16)
    krow = lax.broadcasted_iota(jnp.int32, (tk, tk), 0)
    qcol = lax.broadcasted_iota(jnp.int32, (tk, tk), 1)
    tri = krow <= qcol
    lane = lax.broadcasted_iota(jnp.int32, (tk, hd), 1)
    comp0 = (lane % DIFF_HEAD_DIM) < DIFF_HEAD_DIM // 2
    lv = lam_ref[...]
    lam = (jnp.exp(jnp.sum(lv[0:1] * lv[1:2], axis=-1, keepdims=True))
           - jnp.exp(jnp.sum(lv[2:3] * lv[3:4], axis=-1, keepdims=True)) + LAMBDA_INIT)

    steps = []
    for i in range(nq):
        steps += [(i, kb, "full") for kb in range(2 * i)] + [(i, 2 * i, "diag0"), (i, 2 * i + 1, "diag1")]

    def query_cols(kind):
        return (tk, tq) if kind == "diag1" else (0, tq)

    def head(hh):
        acc = acc_scr.at[hh]
        mx = m_scr.at[hh]
        bufs = (sa_scr.at[hh], sb_scr.at[hh])
        feat = slice(hh * hd, (hh + 1) * hd)
        qc = qc_scr.at[hh]
        for r in range(0, nq * tq, tk):
            qblk = q_ref[r:r + tk, feat]
            qc[0, r:r + tk, :] = jnp.where(comp0, qblk, jnp.zeros_like(qblk))
            qc[1, r:r + tk, :] = jnp.where(comp0, jnp.zeros_like(qblk), qblk)

        def scores(step, s_ref):
            i, kb, kind = step
            lo, hi = query_cols(kind)
            kblk = k_ref[kb * tk:(kb + 1) * tk, feat]
            for c in range(2):
                s = _dot_nt(kblk, qc[c, i * tq + lo:i * tq + hi, :])
                if kind == "diag0":
                    s = jnp.concatenate([jnp.where(tri, s[:, :tk], -jnp.inf), s[:, tk:]], axis=1)
                elif kind == "diag1":
                    s = jnp.where(tri, s, -jnp.inf)
                s_ref[c, :, lo:hi] = s

        def softmax_pv(step, s_ref):
            i, kb, kind = step
            lo, hi = query_cols(kind)
            par = i % 2
            lhs = jnp.concatenate([vt_ref[feat, kb * tk:(kb + 1) * tk], ones], axis=0)
            for c in range(2):
                m_blk = jnp.max(s_ref[c, :, lo:hi], axis=0, keepdims=True)
                if kb == 0:
                    m_new = m_blk
                    acc[par, c, :, lo:hi] = _dot(lhs, jnp.exp2(s_ref[c, :, lo:hi] - m_new).astype(BF16))
                else:
                    m_old = mx[par, c, :, lo:hi]
                    m_new = jnp.maximum(m_old, m_blk)
                    alpha = jnp.exp2(m_old - m_new)
                    p = jnp.exp2(s_ref[c, :, lo:hi] - m_new).astype(BF16)
                    acc[par, c, :, lo:hi] = alpha * acc[par, c, :, lo:hi] + _dot(lhs, p)
                mx[par, c, :, lo:hi] = m_new

        def finish(i):
            par = i % 2
            a0 = acc[par, 0]
            a1 = acc[par, 1]
            ot = a0[:hd] * (1.0 / a0[hd:hd + 1]) - a1[:hd] * (lam / a1[hd:hd + 1])
            ms = jnp.mean(ot * ot, axis=0, keepdims=True)
            ot = ot * (lax.rsqrt(ms + NORM_EPS) * (1.0 - LAMBDA_INIT)) * subw_ref[...]
            o_ref[i * tq:(i + 1) * tq, feat] = ot.T.astype(BF16)

        scores(steps[0], bufs[0])
        yield
        for n, step in enumerate(steps):
            if n + 1 < len(steps):
                scores(steps[n + 1], bufs[(n + 1) % 2])
                yield
            softmax_pv(step, bufs[n % 2])
            yield
            if step[2] == "diag1":
                finish(step[0])

    for _ in itertools.zip_longest(*[head(hh) for hh in range(heads)]):
        pass


def _diffattn(lamv, subw_col, q, k, vt, B, S, tq):
    nq = S // tq
    tk = tq // 2
    hd = 2 * DIFF_HEAD_DIM
    hp = DIFF_HEADS_PER_STEP
    return pl.pallas_call(
        functools.partial(_diffattn_kernel, tq=tq, tk=tk, nq=nq, heads=hp),
        grid=(B, DIFF_HEADS // hp),
        in_specs=[_const_spec(lamv.shape), _const_spec(subw_col.shape),
                  pl.BlockSpec((S, hd * hp), lambda b, h: (b, h)),
                  pl.BlockSpec((S, hd * hp), lambda b, h: (b, h)),
                  pl.BlockSpec((hd * hp, S), lambda b, h: (h, b))],
        out_specs=pl.BlockSpec((S, hd * hp), lambda b, h: (b, h)),
        out_shape=jax.ShapeDtypeStruct((B * S, D_DIFF), BF16),
        scratch_shapes=[pltpu.VMEM((hp, 2, tk, tq), F32),
                        pltpu.VMEM((hp, 2, tk, tq), F32),
                        pltpu.VMEM((hp, 2, 2, hd + ONES_ROWS, tq), F32),
                        pltpu.VMEM((hp, 2, 2, 1, tq), F32),
                        pltpu.VMEM((hp, 2, S, hd), BF16)],
        compiler_params=_params("parallel", "parallel"),
        name="diffattn",
    )(lamv, subw_col, q, k, vt)


def _memkv_rows(rows, mem_ref, g_ref, wk_ref, wv_ref, k_ref, v_ref):
    h = _rms(mem_ref[rows, :], g_ref[...]).astype(BF16)
    k = _dot(h, wk_ref[...])
    yield
    k_ref[rows, :] = k.astype(BF16)
    v = _dot(h, wv_ref[...])
    yield
    v_ref[rows, :] = v.astype(BF16)


def _round_weights_once(pairs):
    @pl.when(pl.program_id(0) == 0)
    def _():
        for src, dst in pairs:
            for r in range(0, src.shape[0], WEIGHT_CAST_ROWS):
                dst[r:r + WEIGHT_CAST_ROWS, :] = src[r:r + WEIGHT_CAST_ROWS, :].astype(BF16)


def _memkv_kernel(mem_ref, g_ref, wk32_ref, wv32_ref, k_ref, v_ref, wk_ref, wv_ref):
    _round_weights_once([(wk32_ref, wk_ref), (wv32_ref, wv_ref)])
    tm = mem_ref.shape[0]
    groups = [_memkv_rows(slice(r, r + ROW_GROUP), mem_ref, g_ref, wk_ref, wv_ref, k_ref, v_ref)
              for r in range(0, tm, ROW_GROUP)]
    for _ in itertools.zip_longest(*groups):
        pass


def _memkv(mem2, g, wk, wv, tm):
    R = mem2.shape[0]
    row = pl.BlockSpec((tm, D_MODEL), lambda i: (i, 0))
    return pl.pallas_call(
        _memkv_kernel,
        grid=(R // tm,),
        in_specs=[row, _const_spec(g.shape), _const_spec(wk.shape), _const_spec(wv.shape)],
        out_specs=[row, row],
        out_shape=[jax.ShapeDtypeStruct((R, D_MODEL), BF16)] * 2,
        scratch_shapes=[pltpu.VMEM(wk.shape, BF16), pltpu.VMEM(wv.shape, BF16)],
        compiler_params=_params("arbitrary"),
        name="memkv",
    )(mem2, g, wk, wv)


def _mixmem_rows(rows, ys_ref, yd_ref, x_ref, km_ref, vm_ref, wo_ref, gmix_ref, gq_ref, wmq_ref,
                 wmo_ref, gmem_ref, x2_ref):
    mixed = _dot(ys_ref[rows, :], wo_ref[:D_SSD, :]) + _dot(yd_ref[rows, :], wo_ref[D_SSD:, :])
    yield
    x1 = x_ref[rows, :] + _rms(mixed, gmix_ref[...])
    qm = _dot(_rms(x1, gq_ref[...]).astype(BF16), wmq_ref[...])
    yield
    qm = (qm * (MEM_HEAD_DIM ** -0.5)).astype(BF16)
    outs = []
    for hd in range(MEM_HEADS):
        sl = slice(hd * MEM_HEAD_DIM, (hd + 1) * MEM_HEAD_DIM)
        s = _dot_nt(qm[:, sl], km_ref[:, sl])
        yield
        e = jnp.exp(s - jnp.max(s, axis=-1, keepdims=True))
        p = e / jnp.sum(e, axis=-1, keepdims=True)
        outs.append(_dot(p.astype(BF16), vm_ref[:, sl]).astype(BF16))
        yield
    c = _dot(jnp.concatenate(outs, axis=1), wmo_ref[...])
    yield
    x2_ref[rows, :] = x1 + _rms(c, gmem_ref[...])


def _mixmem_kernel(ys_ref, yd_ref, x_ref, km_ref, vm_ref, wo32_ref, gmix_ref, gq_ref, wmq32_ref, wmo32_ref,
                   gmem_ref, x2_ref, wo_ref, wmq_ref, wmo_ref):
    _round_weights_once([(wo32_ref, wo_ref), (wmq32_ref, wmq_ref), (wmo32_ref, wmo_ref)])
    tm = x_ref.shape[0]
    groups = [_mixmem_rows(slice(r, r + ROW_GROUP), ys_ref, yd_ref, x_ref, km_ref, vm_ref, wo_ref, gmix_ref,
                           gq_ref, wmq_ref, wmo_ref, gmem_ref, x2_ref) for r in range(0, tm, ROW_GROUP)]
    for _ in itertools.zip_longest(*groups):
        pass


def _mixmem(ys, yd, x2, km, vm, wo, gmix, gq, wmq, wmo, gmem, S, M, tm):
    T = x2.shape[0]
    per_b = S // tm
    row = lambda n: pl.BlockSpec((tm, n), lambda i: (i, 0))
    mem = pl.BlockSpec((M, D_MODEL), lambda i: (i // per_b, 0))
    consts = (wo, gmix, gq, wmq, wmo, gmem)
    return pl.pallas_call(
        _mixmem_kernel,
        grid=(T // tm,),
        in_specs=[row(D_SSD), row(D_DIFF), row(D_MODEL), mem, mem] + [_const_spec(c.shape) for c in consts],
        out_specs=row(D_MODEL),
        out_shape=jax.ShapeDtypeStruct((T, D_MODEL), F32),
        scratch_shapes=[pltpu.VMEM(w.shape, BF16) for w in (wo, wmq, wmo)],
        compiler_params=_params("arbitrary"),
        name="mixmem",
    )(ys, yd, x2, km, vm, *consts)


def _mlp_rows(rows, tf, x_ref, gpre_ref, wup_ref, wdn_ref, gpost_ref, o_ref):
    x = x_ref[rows, :]
    h = _rms(x, gpre_ref[...]).astype(BF16)
    acc = jnp.zeros(x.shape, F32)
    for c in range(D_FF // tf):
        u = _dot(h, wup_ref[:, c * tf:(c + 1) * tf])
        yield
        u = jnp.maximum(u, 0.0)
        acc = acc + _dot((u * u).astype(BF16), wdn_ref[c * tf:(c + 1) * tf, :])
        yield
    o_ref[rows, :] = x + _rms(acc, gpost_ref[...])


def _mlp_kernel(*refs, tf):
    tm = refs[0].shape[0]
    groups = [_mlp_rows(slice(r, r + ROW_GROUP), tf, *refs) for r in range(0, tm, ROW_GROUP)]
    for _ in itertools.zip_longest(*groups):
        pass


def _mlp(x2, gpre, wup, wdn, gpost, tm, tf):
    T = x2.shape[0]
    row = pl.BlockSpec((tm, D_MODEL), lambda i: (i, 0))
    return pl.pallas_call(
        functools.partial(_mlp_kernel, tf=tf),
        grid=(T // tm,),
        in_specs=[row, _const_spec(gpre.shape), _const_spec(wup.shape), _const_spec(wdn.shape),
                  _const_spec(gpost.shape)],
        out_specs=row,
        out_shape=jax.ShapeDtypeStruct((T, D_MODEL), F32),
        compiler_params=_params("parallel"),
        name="mlp",
    )(x2, gpre, wup, wdn, gpost)


def _rope_tables(positions):
    inv = ROPE_THETA ** (-jnp.arange(0, DIFF_HEAD_DIM, 2, dtype=F32) / DIFF_HEAD_DIM)
    ang = inv.reshape(-1, 1) * positions.astype(F32).reshape(1, -1)
    return jnp.cos(ang), jnp.sin(ang)


def _head_lane_order(wcols):
    half = DIFF_HEAD_DIM // 2
    rows = wcols.shape[0]
    return wcols.reshape(rows, DIFF_HEADS, 2, 2, half).transpose(0, 1, 3, 2, 4).reshape(rows, D_DIFF)


def kernel(x, mem, positions, norm_mix_pre, norm_mix_post, norm_mem_q, norm_mem_kv, norm_mem_post,
           norm_mlp_pre, norm_mlp_post, w_in, conv_w, conv_b, dt_bias, a_log, d_skip, ssd_norm_w,
           lambda_q1, lambda_k1, lambda_q2, lambda_k2, subln_w, w_out, w_mq, w_mk, w_mv, w_mo, w_up, w_down):
    B, S, _ = x.shape
    M = mem.shape[1]
    T = B * S
    assert norm_mix_pre.shape[0] == 1, "single-layer trunk"
    x2 = x.reshape(T, D_MODEL)
    cos, sin = _rope_tables(positions)

    o0 = D_SSD
    o1 = o0 + D_XBC
    o2 = o1 + SSD_HEADS
    o3 = o2 + D_DIFF
    o4 = o3 + D_DIFF
    w = w_in[0]
    wz = w[:, :o0].astype(BF16)
    wxbc = w[:, o0:o1].astype(BF16)
    wq = _head_lane_order(w[:, o2:o3]).astype(BF16)
    wk = _head_lane_order(w[:, o3:o4]).astype(BF16)
    head_pad = ((0, DT_ROWS - SSD_HEADS), (0, 0))
    wvdt = jnp.concatenate([w[:, o4:].T, jnp.pad(w[:, o1:o2].T, head_pad)], axis=0).astype(BF16)
    head_col = lambda p: jnp.pad(p.reshape(SSD_HEADS, 1), head_pad)

    zg, xc, dtt, q, k, vt = _inproj(x2, norm_mix_pre, cos, sin, wz, wxbc, wq, wk, wvdt,
                                    conv_w[0, :, 0, :], conv_b, head_col(dt_bias), S, tm=INPROJ_ROWS)

    y_ssd = _ssd(xc, zg, dtt, head_col(a_log), jnp.repeat(d_skip, SSD_HEAD_DIM, axis=1), ssd_norm_w, B, S)

    lamv = jnp.concatenate([lambda_q1, lambda_k1, lambda_q2, lambda_k2], axis=0)
    y_diff = _diffattn(lamv, subln_w.reshape(2 * DIFF_HEAD_DIM, 1), q, k, vt, B, S, tq=DIFF_Q_BLOCK)

    km, vm = _memkv(mem.reshape(B * M, D_MODEL), norm_mem_kv, w_mk[0], w_mv[0], tm=min(ROW_TILE, B * M))
    x2b = _mixmem(y_ssd, y_diff, x2, km, vm, w_out[0], norm_mix_post, norm_mem_q, w_mq[0], w_mo[0],
                  norm_mem_post, S, M, tm=ROW_TILE)
    out = _mlp(x2b, norm_mlp_pre, w_up[0].astype(BF16), w_down[0].astype(BF16), norm_mlp_post,
               tm=ROW_TILE, tf=MLP_FF_CHUNK)
    return out.reshape(B, S, D_MODEL)
```

```python
import functools
import itertools
import math

import jax
import jax.numpy as jnp
from jax import lax
from jax.experimental import pallas as pl
from jax.experimental.pallas import tpu as pltpu

F32 = jnp.float32
BF16 = jnp.bfloat16

D_MODEL = 1024
D_SSD = 512
SSD_HEAD_DIM = 64
SSD_HEADS = D_SSD // SSD_HEAD_DIM
SSD_GROUPS = 2
SSD_STATE = 128
CONV_WIDTH = 4
CHUNK = 128
D_XBC = D_SSD + 2 * SSD_GROUPS * SSD_STATE
D_DIFF = D_MODEL - D_SSD
DIFF_HEAD_DIM = 64
DIFF_HEADS = D_DIFF // (2 * DIFF_HEAD_DIM)
ROPE_THETA = 10000.0
MEM_HEADS = 4
MEM_HEAD_DIM = D_MODEL // MEM_HEADS
D_FF = 4 * D_MODEL
NORM_EPS = 1e-6
LAMBDA_INIT = 0.8 - 0.6 * math.exp(-0.3 * 0)

LOG2E = math.log2(math.e)
LANES = 128
SUBLANES = 8
ONES_ROWS = 16
DT_ROWS = 16
CONV_COLS = 256
SSD_SEQS_PER_STEP = 4
SSD_CHUNKS_PER_STEP = 4
DIFF_HEADS_PER_STEP = 2
ROW_GROUP = 256
INPROJ_ROWS = 512
ROW_TILE = 1024
DIFF_Q_BLOCK = 512
MLP_FF_CHUNK = 1024
WEIGHT_CAST_ROWS = 128
VMEM_LIMIT = 56 * 1024 * 1024


def _dot(a, b):
    return jnp.dot(a, b, preferred_element_type=F32)


def _dot_nt(a, b):
    return lax.dot_general(a, b, (((1,), (1,)), ((), ())), preferred_element_type=F32)


def _rms(x, g):
    ms = jnp.mean(x * x, axis=-1, keepdims=True)
    return x * lax.rsqrt(ms + NORM_EPS) * g


def _silu_of_half(half):
    return half + half * jnp.tanh(half)


def _const_spec(shape):
    nd = len(shape)
    return pl.BlockSpec(shape, lambda *_: (0,) * nd, pipeline_mode=pl.Buffered(1))


def _params(*sem):
    return pltpu.CompilerParams(dimension_semantics=sem, vmem_limit_bytes=VMEM_LIMIT)


def _inproj_kernel(x_ref, g_ref, cos_ref, sin_ref, wz_ref, wxbc_ref, wq_ref, wk_ref, wvdt_ref,
                   cw_ref, cb_ref, dtb_ref,
                   zg_ref, xc_ref, dtt_ref, q_ref, k_ref, vt_ref, halo_ref, *, tm, tiles_per_seq):
    @pl.when(pl.program_id(0) % tiles_per_seq == 0)
    def _():
        halo_ref[...] = jnp.zeros_like(halo_ref)

    vregs = ROW_GROUP // SUBLANES
    sub = lax.broadcasted_iota(jnp.int32, (1, SUBLANES, 1), 1)
    reps = LANES // (DIFF_HEAD_DIM // 2)
    lane = lax.broadcasted_iota(jnp.int32, (ROW_GROUP, LANES), 1)
    history = {}

    def row_group(rows):
        h = _rms(x_ref[rows, :], g_ref[...]).astype(BF16)

        def gate(z):
            zg_ref[rows, :] = _silu_of_half(0.5 * z).astype(BF16)

        def conv(c, xbc):
            cols = slice(c * CONV_COLS, (c + 1) * CONV_COLS)
            xb = xbc.reshape(vregs, SUBLANES, CONV_COLS)
            prev = history[c] if c in history else halo_ref[:, cols]
            w0, w1, w2, w3 = [0.5 * cw_ref[j:j + 1, cols] for j in range(CONV_WIDTH)]

            def shift(a, a_prev, k):
                r = pltpu.roll(a, k, 1)
                r_first = pltpu.roll(a_prev, k, 0)[None]
                return jnp.where(sub >= k, r, jnp.concatenate([r_first, r[:vregs - 1]], axis=0))

            x1 = shift(xb, prev, 1)
            pair = w1 * xb + w0 * x1
            pair_prev = w1 * prev + w0 * pltpu.roll(prev, 1, 0)
            acc = 0.5 * cb_ref[:, cols] + w3 * xb + w2 * x1 + shift(pair, pair_prev, 2)
            history[c] = xb[vregs - 1]
            xc_ref[rows, cols] = _silu_of_half(acc.reshape(ROW_GROUP, CONV_COLS)).astype(BF16)

        def values_and_steps(r):
            vt_ref[:, rows] = r[:D_DIFF].astype(BF16)
            dtt_ref[:, rows] = jax.nn.softplus(r[D_DIFF:] + dtb_ref[...])

        tables = []

        def rope(a, c):
            if not tables:
                cos = jnp.concatenate([cos_ref[:, rows]] * reps, axis=0).T
                sin = jnp.concatenate([sin_ref[:, rows]] * reps, axis=0).T
                tables.extend([cos, jnp.where(lane < LANES // 2, -sin, sin)])
            ac = a[:, c * LANES:(c + 1) * LANES]
            return ac * tables[0] + pltpu.roll(ac, LANES // 2, 1) * tables[1]

        def keys(a):
            for c in range(DIFF_HEADS):
                k_ref[rows, c * LANES:(c + 1) * LANES] = rope(a, c).astype(BF16)

        def queries(a):
            for c in range(DIFF_HEADS):
                q_ref[rows, c * LANES:(c + 1) * LANES] = (rope(a, c) * (DIFF_HEAD_DIM ** -0.5 * LOG2E)).astype(BF16)

        light = [(lambda: _dot_nt(wvdt_ref[...], h), values_and_steps), (lambda: _dot(h, wz_ref[...]), gate),
                 (lambda: _dot(h, wk_ref[...]), keys), (lambda: _dot(h, wq_ref[...]), queries)]
        for c in range(D_XBC // CONV_COLS):
            for matmul, epilogue in ((functools.partial(lambda c: _dot(h, wxbc_ref[:, c * CONV_COLS:(c + 1) * CONV_COLS]), c),
                                      functools.partial(conv, c)), light[c]):
                result = matmul()
                yield
                epilogue(result)

    for _ in itertools.zip_longest(*[row_group(slice(r, r + ROW_GROUP)) for r in range(0, tm, ROW_GROUP)]):
        pass
    for c, last_rows in history.items():
        halo_ref[:, c * CONV_COLS:(c + 1) * CONV_COLS] = last_rows


def _inproj(x2, g, cos, sin, wz, wxbc, wq, wk, wvdt, cw, cb, dtb_col, S, tm):
    T = x2.shape[0]
    row = lambda n: pl.BlockSpec((tm, n), lambda i: (i, 0))
    col = lambda n: pl.BlockSpec((n, tm), lambda i: (0, i))
    return pl.pallas_call(
        functools.partial(_inproj_kernel, tm=tm, tiles_per_seq=S // tm),
        grid=(T // tm,),
        in_specs=[row(D_MODEL), _const_spec((1, D_MODEL)), col(cos.shape[0]), col(sin.shape[0]),
                  _const_spec(wz.shape), _const_spec(wxbc.shape),
                  _const_spec(wq.shape), _const_spec(wk.shape), _const_spec(wvdt.shape),
                  _const_spec(cw.shape), _const_spec(cb.shape), _const_spec(dtb_col.shape)],
        out_specs=[row(D_SSD), row(D_XBC), col(DT_ROWS), row(D_DIFF), row(D_DIFF), col(D_DIFF)],
        out_shape=[jax.ShapeDtypeStruct((T, D_SSD), BF16),
                   jax.ShapeDtypeStruct((T, D_XBC), BF16),
                   jax.ShapeDtypeStruct((DT_ROWS, T), F32),
                   jax.ShapeDtypeStruct((T, D_DIFF), BF16),
                   jax.ShapeDtypeStruct((T, D_DIFF), BF16),
                   jax.ShapeDtypeStruct((D_DIFF, T), BF16)],
        scratch_shapes=[pltpu.VMEM((SUBLANES, D_XBC), F32)],
        compiler_params=_params("arbitrary"),
        name="inproj",
    )(x2, g, cos, sin, wz, wxbc, wq, wk, wvdt, cw, cb, dtb_col)


def _ssd_chunk(xc_ref, zg_ref, dt_t, alog_ref, dskip_ref, nw_ref, y_ref, state_ref, lane, causal, tri_t):
    da_t = dt_t * (-LOG2E * jnp.exp(alog_ref[...]))
    hi = da_t.astype(BF16)
    rem = da_t - hi.astype(F32)
    mid = rem.astype(BF16)
    lo = (rem - mid.astype(F32)).astype(BF16)
    cum_t = _dot(hi, tri_t) + _dot(mid, tri_t) + _dot(lo, tri_t)
    yield
    last = cum_t[:, CHUNK - 1:CHUNK]
    wrow_t = dt_t * jnp.exp2(last - cum_t)
    cdec = jnp.exp2(last)
    src_t = cum_t - jnp.log2(dt_t)
    cum = jnp.concatenate([cum_t, jnp.zeros((CHUNK - DT_ROWS, CHUNK), F32)], axis=0).T
    yield

    gn = SSD_GROUPS * SSD_STATE
    pairs_per_group = SSD_HEADS // SSD_GROUPS // 2
    y_pairs = []
    for g in range(SSD_GROUPS):
        bg = xc_ref[:, D_SSD + g * SSD_STATE:D_SSD + (g + 1) * SSD_STATE]
        cg = xc_ref[:, D_SSD + gn + g * SSD_STATE:D_SSD + gn + (g + 1) * SSD_STATE]
        cbm = _dot_nt(cg, bg)
        bgt = bg.astype(F32).T
        cg32 = cg.astype(F32)
        yield
        for jp in range(pairs_per_group):
            j = g * pairs_per_group + jp
            xs_pair = xc_ref[:, j * LANES:(j + 1) * LANES]
            lhs, rhs = [], []
            for half in range(2):
                hd = 2 * j + half
                colb = jnp.broadcast_to(cum[:, hd:hd + 1], (CHUNK, LANES))
                dec = jnp.where(causal, jnp.exp2(colb - src_t[hd:hd + 1, :]), 0.0)
                wp = (cbm * dec).astype(BF16)
                gg = (cg32 * jnp.exp2(colb)).astype(BF16)
                in_half = (lane >= SSD_HEAD_DIM) if half else (lane < SSD_HEAD_DIM)
                xs_m = jnp.where(in_half, xs_pair, jnp.zeros_like(xs_pair))
                st = state_ref[hd]
                lhs += [wp, gg]
                rhs += [xs_m, st.astype(BF16)]
                l2 = (bgt * wrow_t[hd:hd + 1, :]).astype(BF16)
                state_ref[hd] = st * cdec[hd:hd + 1, :] + _dot(l2, xs_m)
                yield
            y_pairs.append(_dot(jnp.concatenate(lhs, axis=1), jnp.concatenate(rhs, axis=0)))

    y = jnp.concatenate(y_pairs, axis=1) + dskip_ref[...] * xc_ref[:, :D_SSD].astype(F32)
    y = y * zg_ref[...].astype(F32)
    y_ref[...] = _rms(y, nw_ref[...]).astype(BF16)


def _ssd_kernel(xc_ref, zg_ref, *rest):
    dtt_refs = rest[:SSD_SEQS_PER_STEP]
    alog_ref, dskip_ref, nw_ref, y_ref, state_ref = rest[SSD_SEQS_PER_STEP:]

    @pl.when(pl.program_id(1) == 0)
    def _():
        state_ref[...] = jnp.zeros_like(state_ref)

    row = lax.broadcasted_iota(jnp.int32, (CHUNK, LANES), 0)
    lane = lax.broadcasted_iota(jnp.int32, (CHUNK, LANES), 1)
    causal = row >= lane
    tri_t = jnp.where(row <= lane, 1.0, 0.0).astype(BF16)
    for sc in range(SSD_CHUNKS_PER_STEP):
        t = pl.ds(sc * CHUNK, CHUNK)
        chunks = [_ssd_chunk(xc_ref.at[0, p, t], zg_ref.at[0, p, t], dtt_refs[p][:, sc * CHUNK:(sc + 1) * CHUNK],
                             alog_ref, dskip_ref, nw_ref, y_ref.at[0, p, t], state_ref.at[p], lane, causal, tri_t)
                  for p in range(SSD_SEQS_PER_STEP)]
        for _ in itertools.zip_longest(*chunks):
            pass


def _ssd(xc, zg, dtt, alog_col, dskip, nw, B, S):
    span = SSD_CHUNKS_PER_STEP * CHUNK
    nc = S // span
    nb = SSD_SEQS_PER_STEP
    seqs = lambda a: a.reshape(B // nb, nb, S, a.shape[-1])
    row = lambda n: pl.BlockSpec((1, nb, span, n), lambda b, c: (b, 0, c, 0))
    y = pl.pallas_call(
        _ssd_kernel,
        grid=(B // nb, nc),
        in_specs=[row(D_XBC), row(D_SSD)]
        + [pl.BlockSpec((DT_ROWS, span), functools.partial(lambda p, b, c: (0, (b * nb + p) * nc + c), p))
           for p in range(nb)]
        + [_const_spec(alog_col.shape), _const_spec(dskip.shape), _const_spec(nw.shape)],
        out_specs=row(D_SSD),
        out_shape=jax.ShapeDtypeStruct((B // nb, nb, S, D_SSD), BF16),
        scratch_shapes=[pltpu.VMEM((nb, SSD_HEADS, SSD_STATE, LANES), F32)],
        compiler_params=_params("parallel", "arbitrary"),
        name="ssd",
    )(seqs(xc), seqs(zg), *([dtt] * nb), alog_col, dskip, nw)
    return y.reshape(B * S, D_SSD)


def _diffattn_kernel(lam_ref, subw_ref, q_ref, k_ref, vt_ref, o_ref, sa_scr, sb_scr, acc_scr, m_scr, qc_scr, *,
                     tq, tk, nq, heads):
    hd = 2 * DIFF_HEAD_DIM
    ones = jnp.ones((ONES_ROWS, tk), BF16)
    krow = lax.broadcasted_iota(jnp.int32, (tk, tk), 0)
    qcol = lax.broadcasted_iota(jnp.int32, (tk, tk), 1)
    tri = krow <= qcol
    lane = lax.broadcasted_iota(jnp.int32, (tk, hd), 1)
    comp0 = (lane % DIFF_HEAD_DIM) < DIFF_HEAD_DIM // 2
    lv = lam_ref[...]
    lam = (jnp.exp(jnp.sum(lv[0:1] * lv[1:2], axis=-1, keepdims=True))
           - jnp.exp(jnp.sum(lv[2:3] * lv[3:4], axis=-1, keepdims=True)) + LAMBDA_INIT)

    steps = []
    for i in range(nq):
        steps += [(i, kb, "full") for kb in range(2 * i)] + [(i, 2 * i, "diag0"), (i, 2 * i + 1, "diag1")]

    def query_cols(kind):
        return (tk, tq) if kind == "diag1" else (0, tq)

    def head(hh):
        acc = acc_scr.at[hh]
        mx = m_scr.at[hh]
        bufs = (sa_scr.at[hh], sb_scr.at[hh])
        feat = slice(hh * hd, (hh + 1) * hd)
        qc = qc_scr.at[hh]
        for r in range(0, nq * tq, tk):
            qblk = q_ref[r:r + tk, feat]
            qc[0, r:r + tk, :] = jnp.where(comp0, qblk, jnp.zeros_like(qblk))
            qc[1, r:r + tk, :] = jnp.where(comp0, jnp.zeros_like(qblk), qblk)

        def scores(step, s_ref):
            i, kb, kind = step
            lo, hi = query_cols(kind)
            kblk = k_ref[kb * tk:(kb + 1) * tk, feat]
            for c in range(2):
                s = _dot_nt(kblk, qc[c, i * tq + lo:i * tq + hi, :])
                if kind == "diag0":
                    s = jnp.concatenate([jnp.where(tri, s[:, :tk], -jnp.inf), s[:, tk:]], axis=1)
                elif kind == "diag1":
                    s = jnp.where(tri, s, -jnp.inf)
                s_ref[c, :, lo:hi] = s

        def softmax_pv(step, s_ref):
            i, kb, kind = step
            lo, hi = query_cols(kind)
            par = i % 2
            lhs = jnp.concatenate([vt_ref[feat, kb * tk:(kb + 1) * tk], ones], axis=0)
            for c in range(2):
                m_blk = jnp.max(s_ref[c, :, lo:hi], axis=0, keepdims=True)
                if kb == 0:
                    m_new = m_blk
                    acc[par, c, :, lo:hi] = _dot(lhs, jnp.exp2(s_ref[c, :, lo:hi] - m_new).astype(BF16))
                else:
                    m_old = mx[par, c, :, lo:hi]
                    m_new = jnp.maximum(m_old, m_blk)
                    alpha = jnp.exp2(m_old - m_new)
                    p = jnp.exp2(s_ref[c, :, lo:hi] - m_new).astype(BF16)
                    acc[par, c, :, lo:hi] = alpha * acc[par, c, :, lo:hi] + _dot(lhs, p)
                mx[par, c, :, lo:hi] = m_new

        def finish(i):
            par = i % 2
            a0 = acc[par, 0]
            a1 = acc[par, 1]
            ot = a0[:hd] * (1.0 / a0[hd:hd + 1]) - a1[:hd] * (lam / a1[hd:hd + 1])
            ms = jnp.mean(ot * ot, axis=0, keepdims=True)
            ot = ot * (lax.rsqrt(ms + NORM_EPS) * (1.0 - LAMBDA_INIT)) * subw_ref[...]
            o_ref[i * tq:(i + 1) * tq, feat] = ot.T.astype(BF16)

        scores(steps[0], bufs[0])
        yield
        for n, step in enumerate(steps):
            if n + 1 < len(steps):
                scores(steps[n + 1], bufs[(n + 1) % 2])
                yield
            softmax_pv(step, bufs[n % 2])
            yield
            if step[2] == "diag1":
                finish(step[0])

    for _ in itertools.zip_longest(*[head(hh) for hh in range(heads)]):
        pass


def _diffattn(lamv, subw_col, q, k, vt, B, S, tq):
    nq = S // tq
    tk = tq // 2
    hd = 2 * DIFF_HEAD_DIM
    hp = DIFF_HEADS_PER_STEP
    return pl.pallas_call(
        functools.partial(_diffattn_kernel, tq=tq, tk=tk, nq=nq, heads=hp),
        grid=(B, DIFF_HEADS // hp),
        in_specs=[_const_spec(lamv.shape), _const_spec(subw_col.shape),
                  pl.BlockSpec((S, hd * hp), lambda b, h: (b, h)),
                  pl.BlockSpec((S, hd * hp), lambda b, h: (b, h)),
                  pl.BlockSpec((hd * hp, S), lambda b, h: (h, b))],
        out_specs=pl.BlockSpec((S, hd * hp), lambda b, h: (b, h)),
        out_shape=jax.ShapeDtypeStruct((B * S, D_DIFF), BF16),
        scratch_shapes=[pltpu.VMEM((hp, 2, tk, tq), F32),
                        pltpu.VMEM((hp, 2, tk, tq), F32),
                        pltpu.VMEM((hp, 2, 2, hd + ONES_ROWS, tq), F32),
                        pltpu.VMEM((hp, 2, 2, 1, tq), F32),
                        pltpu.VMEM((hp, 2, S, hd), BF16)],
        compiler_params=_params("parallel", "parallel"),
        name="diffattn",
    )(lamv, subw_col, q, k, vt)


def _memkv_rows(rows, mem_ref, g_ref, wk_ref, wv_ref, k_ref, v_ref):
    h = _rms(mem_ref[rows, :], g_ref[...]).astype(BF16)
    k = _dot(h, wk_ref[...])
    yield
    k_ref[rows, :] = k.astype(BF16)
    v = _dot(h, wv_ref[...])
    yield
    v_ref[rows, :] = v.astype(BF16)


def _round_weights_once(pairs):
    @pl.when(pl.program_id(0) == 0)
    def _():
        for src, dst in pairs:
            for r in range(0, src.shape[0], WEIGHT_CAST_ROWS):
                dst[r:r + WEIGHT_CAST_ROWS, :] = src[r:r + WEIGHT_CAST_ROWS, :].astype(BF16)


def _memkv_kernel(mem_ref, g_ref, wk32_ref, wv32_ref, k_ref, v_ref, wk_ref, wv_ref):
    _round_weights_once([(wk32_ref, wk_ref), (wv32_ref, wv_ref)])
    tm = mem_ref.shape[0]
    groups = [_memkv_rows(slice(r, r + ROW_GROUP), mem_ref, g_ref, wk_ref, wv_ref, k_ref, v_ref)
              for r in range(0, tm, ROW_GROUP)]
    for _ in itertools.zip_longest(*groups):
        pass


def _memkv(mem2, g, wk, wv, tm):
    R = mem2.shape[0]
    row = pl.BlockSpec((tm, D_MODEL), lambda i: (i, 0))
    return pl.pallas_call(
        _memkv_kernel,
        grid=(R // tm,),
        in_specs=[row, _const_spec(g.shape), _const_spec(wk.shape), _const_spec(wv.shape)],
        out_specs=[row, row],
        out_shape=[jax.ShapeDtypeStruct((R, D_MODEL), BF16)] * 2,
        scratch_shapes=[pltpu.VMEM(wk.shape, BF16), pltpu.VMEM(wv.shape, BF16)],
        compiler_params=_params("arbitrary"),
        name="memkv",
    )(mem2, g, wk, wv)


def _mixmem_rows(rows, ys_ref, yd_ref, x_ref, km_ref, vm_ref, wo_ref, gmix_ref, gq_ref, wmq_ref,
                 wmo_ref, gmem_ref, x2_ref):
    mixed = _dot(ys_ref[rows, :], wo_ref[:D_SSD, :]) + _dot(yd_ref[rows, :], wo_ref[D_SSD:, :])
    yield
    x1 = x_ref[rows, :] + _rms(mixed, gmix_ref[...])
    qm = _dot(_rms(x1, gq_ref[...]).astype(BF16), wmq_ref[...])
    yield
    qm = (qm * (MEM_HEAD_DIM ** -0.5)).astype(BF16)
    outs = []
    for hd in range(MEM_HEADS):
        sl = slice(hd * MEM_HEAD_DIM, (hd + 1) * MEM_HEAD_DIM)
        s = _dot_nt(qm[:, sl], km_ref[:, sl])
        yield
        e = jnp.exp(s - jnp.max(s, axis=-1, keepdims=True))
        p = e / jnp.sum(e, axis=-1, keepdims=True)
        outs.append(_dot(p.astype(BF16), vm_ref[:, sl]).astype(BF16))
        yield
    c = _dot(jnp.concatenate(outs, axis=1), wmo_ref[...])
    yield
    x2_ref[rows, :] = x1 + _rms(c, gmem_ref[...])


def _mixmem_kernel(ys_ref, yd_ref, x_ref, km_ref, vm_ref, wo32_ref, gmix_ref, gq_ref, wmq32_ref, wmo32_ref,
                   gmem_ref, wup32_ref, wdn32_ref, x2_ref, wup_ref, wdn_ref, wo_ref, wmq_ref, wmo_ref):
    _round_weights_once([(wo32_ref, wo_ref), (wmq32_ref, wmq_ref), (wmo32_ref, wmo_ref)])
    for src, dst in ((wup32_ref, wup_ref), (wdn32_ref, wdn_ref)):
        step = min(WEIGHT_CAST_ROWS, src.shape[0])
        for r in range(0, src.shape[0], step):
            dst[r:r + step, :] = src[r:r + step, :].astype(BF16)
    tm = x_ref.shape[0]
    groups = [_mixmem_rows(slice(r, r + ROW_GROUP), ys_ref, yd_ref, x_ref, km_ref, vm_ref, wo_ref, gmix_ref,
                           gq_ref, wmq_ref, wmo_ref, gmem_ref, x2_ref) for r in range(0, tm, ROW_GROUP)]
    for _ in itertools.zip_longest(*groups):
        pass


def _mixmem(ys, yd, x2, km, vm, wo, gmix, gq, wmq, wmo, gmem, wup, wdn, S, M, tm):
    T = x2.shape[0]
    steps = T // tm
    per_b = S // tm
    row = lambda n: pl.BlockSpec((tm, n), lambda i: (i, 0))
    mem = pl.BlockSpec((M, D_MODEL), lambda i: (i // per_b, 0))
    consts = (wo, gmix, gq, wmq, wmo, gmem)
    assert wup.shape[0] % steps == 0 and wdn.shape[0] % steps == 0
    piece = lambda w: pl.BlockSpec((w.shape[0] // steps, w.shape[1]), lambda i: (i, 0))
    return pl.pallas_call(
        _mixmem_kernel,
        grid=(steps,),
        in_specs=[row(D_SSD), row(D_DIFF), row(D_MODEL), mem, mem] + [_const_spec(c.shape) for c in consts]
        + [piece(wup), piece(wdn)],
        out_specs=[row(D_MODEL), piece(wup), piece(wdn)],
        out_shape=[jax.ShapeDtypeStruct((T, D_MODEL), F32), jax.ShapeDtypeStruct(wup.shape, BF16),
                   jax.ShapeDtypeStruct(wdn.shape, BF16)],
        scratch_shapes=[pltpu.VMEM(w.shape, BF16) for w in (wo, wmq, wmo)],
        compiler_params=_params("arbitrary"),
        name="mixmem",
    )(ys, yd, x2, km, vm, *consts, wup, wdn)


def _mlp_rows(rows, tf, x_ref, gpre_ref, wup_ref, wdn_ref, gpost_ref, o_ref):
    x = x_ref[rows, :]
    h = _rms(x, gpre_ref[...]).astype(BF16)
    acc = jnp.zeros(x.shape, F32)
    for c in range(D_FF // tf):
        u = _dot(h, wup_ref[:, c * tf:(c + 1) * tf])
        yield
        u = jnp.maximum(u, 0.0)
        acc = acc + _dot((u * u).astype(BF16), wdn_ref[c * tf:(c + 1) * tf, :])
        yield
    o_ref[rows, :] = x + _rms(acc, gpost_ref[...])


def _mlp_kernel(*refs, tf):
    tm = refs[0].shape[0]
    groups = [_mlp_rows(slice(r, r + ROW_GROUP), tf, *refs) for r in range(0, tm, ROW_GROUP)]
    for _ in itertools.zip_longest(*groups):
        pass


def _mlp(x2, gpre, wup, wdn, gpost, tm, tf):
    T = x2.shape[0]
    row = pl.BlockSpec((tm, D_MODEL), lambda i: (i, 0))
    return pl.pallas_call(
        functools.partial(_mlp_kernel, tf=tf),
        grid=(T // tm,),
        in_specs=[row, _const_spec(gpre.shape), _const_spec(wup.shape), _const_spec(wdn.shape),
                  _const_spec(gpost.shape)],
        out_specs=row,
        out_shape=jax.ShapeDtypeStruct((T, D_MODEL), F32),
        compiler_params=_params("parallel"),
        name="mlp",
    )(x2, gpre, wup, wdn, gpost)


def _rope_tables(positions):
    inv = ROPE_THETA ** (-jnp.arange(0, DIFF_HEAD_DIM, 2, dtype=F32) / DIFF_HEAD_DIM)
    ang = inv.reshape(-1, 1) * positions.astype(F32).reshape(1, -1)
    return jnp.cos(ang), jnp.sin(ang)


def _head_lane_order(wcols):
    half = DIFF_HEAD_DIM // 2
    rows = wcols.shape[0]
    return wcols.reshape(rows, DIFF_HEADS, 2, 2, half).transpose(0, 1, 3, 2, 4).reshape(rows, D_DIFF)


def kernel(x, mem, positions, norm_mix_pre, norm_mix_post, norm_mem_q, norm_mem_kv, norm_mem_post,
           norm_mlp_pre, norm_mlp_post, w_in, conv_w, conv_b, dt_bias, a_log, d_skip, ssd_norm_w,
           lambda_q1, lambda_k1, lambda_q2, lambda_k2, subln_w, w_out, w_mq, w_mk, w_mv, w_mo, w_up, w_down):
    B, S, _ = x.shape
    M = mem.shape[1]
    T = B * S
    assert norm_mix_pre.shape[0] == 1, "single-layer trunk"
    x2 = x.reshape(T, D_MODEL)
    cos, sin = _rope_tables(positions)

    o0 = D_SSD
    o1 = o0 + D_XBC
    o2 = o1 + SSD_HEADS
    o3 = o2 + D_DIFF
    o4 = o3 + D_DIFF
    w = w_in[0]
    wz = w[:, :o0].astype(BF16)
    wxbc = w[:, o0:o1].astype(BF16)
    wq = _head_lane_order(w[:, o2:o3]).astype(BF16)
    wk = _head_lane_order(w[:, o3:o4]).astype(BF16)
    head_pad = ((0, DT_ROWS - SSD_HEADS), (0, 0))
    wvdt = jnp.concatenate([w[:, o4:].T, jnp.pad(w[:, o1:o2].T, head_pad)], axis=0).astype(BF16)
    head_col = lambda p: jnp.pad(p.reshape(SSD_HEADS, 1), head_pad)

    zg, xc, dtt, q, k, vt = _inproj(x2, norm_mix_pre, cos, sin, wz, wxbc, wq, wk, wvdt,
                                    conv_w[0, :, 0, :], conv_b, head_col(dt_bias), S, tm=INPROJ_ROWS)

    y_ssd = _ssd(xc, zg, dtt, head_col(a_log), jnp.repeat(d_skip, SSD_HEAD_DIM, axis=1), ssd_norm_w, B, S)

    lamv = jnp.concatenate([lambda_q1, lambda_k1, lambda_q2, lambda_k2], axis=0)
    y_diff = _diffattn(lamv, subln_w.reshape(2 * DIFF_HEAD_DIM, 1), q, k, vt, B, S, tq=DIFF_Q_BLOCK)

    km, vm = _memkv(mem.reshape(B * M, D_MODEL), norm_mem_kv, w_mk[0], w_mv[0], tm=min(ROW_TILE, B * M))
    x2b, wup, wdn = _mixmem(y_ssd, y_diff, x2, km, vm, w_out[0], norm_mix_post, norm_mem_q, w_mq[0], w_mo[0],
                            norm_mem_post, w_up[0], w_down[0], S, M, tm=ROW_TILE)
    out = _mlp(x2b, norm_mlp_pre, wup, wdn, norm_mlp_post, tm=ROW_TILE, tf=MLP_FF_CHUNK)
    return out.reshape(B, S, D_MODEL)
```

```python
import functools
import itertools
import math

import jax
import jax.numpy as jnp
from jax import lax
from jax.experimental import pallas as pl
from jax.experimental.pallas import tpu as pltpu

F32 = jnp.float32
BF16 = jnp.bfloat16

D_MODEL = 1024
D_SSD = 512
SSD_HEAD_DIM = 64
SSD_HEADS = D_SSD // SSD_HEAD_DIM
SSD_GROUPS = 2
SSD_STATE = 128
CONV_WIDTH = 4
CHUNK = 128
D_XBC = D_SSD + 2 * SSD_GROUPS * SSD_STATE
D_DIFF = D_MODEL - D_SSD
DIFF_HEAD_DIM = 64
DIFF_HEADS = D_DIFF // (2 * DIFF_HEAD_DIM)
ROPE_THETA = 10000.0
MEM_HEADS = 4
MEM_HEAD_DIM = D_MODEL // MEM_HEADS
D_FF = 4 * D_MODEL
NORM_EPS = 1e-6
LAMBDA_INIT = 0.8 - 0.6 * math.exp(-0.3 * 0)

LOG2E = math.log2(math.e)
LANES = 128
SUBLANES = 8
ONES_ROWS = 16
DT_ROWS = 16
CONV_COLS = 256
SSD_SEQS_PER_STEP = 4
SSD_CHUNKS_PER_STEP = 4
DIFF_HEADS_PER_STEP = 2
ROW_GROUP = 256
INPROJ_ROWS = 512
ROW_TILE = 1024
DIFF_Q_BLOCK = 512
MLP_FF_CHUNK = 1024
WEIGHT_CAST_ROWS = 128
VMEM_LIMIT = 56 * 1024 * 1024


def _dot(a, b):
    return jnp.dot(a, b, preferred_element_type=F32)


def _dot_nt(a, b):
    return lax.dot_general(a, b, (((1,), (1,)), ((), ())), preferred_element_type=F32)


def _rms(x, g):
    ms = jnp.mean(x * x, axis=-1, keepdims=True)
    return x * lax.rsqrt(ms + NORM_EPS) * g


def _silu_of_half(half):
    return half + half * jnp.tanh(half)


def _const_spec(shape):
    nd = len(shape)
    return pl.BlockSpec(shape, lambda *_: (0,) * nd, pipeline_mode=pl.Buffered(1))


def _params(*sem):
    return pltpu.CompilerParams(dimension_semantics=sem, vmem_limit_bytes=VMEM_LIMIT)


def _inproj_kernel(x_ref, g_ref, cos_ref, sin_ref, wz_ref, wxbc_ref, wq_ref, wk_ref, wvdt_ref,
                   cw_ref, cb_ref, dtb_ref,
                   zg_ref, xc_ref, dtt_ref, q_ref, k_ref, vt_ref, halo_ref, *, tm, tiles_per_seq):
    @pl.when(pl.program_id(0) % tiles_per_seq == 0)
    def _():
        halo_ref[...] = jnp.zeros_like(halo_ref)

    vregs = ROW_GROUP // SUBLANES
    sub = lax.broadcasted_iota(jnp.int32, (1, SUBLANES, 1), 1)
    reps = LANES // (DIFF_HEAD_DIM // 2)
    lane = lax.broadcasted_iota(jnp.int32, (ROW_GROUP, LANES), 1)
    history = {}

    def row_group(rows):
        h = _rms(x_ref[rows, :], g_ref[...]).astype(BF16)

        def gate(z):
            zg_ref[rows, :] = _silu_of_half(0.5 * z).astype(BF16)

        def conv(c, xbc):
            cols = slice(c * CONV_COLS, (c + 1) * CONV_COLS)
            xb = xbc.reshape(vregs, SUBLANES, CONV_COLS)
            prev = history[c] if c in history else halo_ref[:, cols]
            w0, w1, w2, w3 = [0.5 * cw_ref[j:j + 1, cols] for j in range(CONV_WIDTH)]

            def shift(a, a_prev, k):
                r = pltpu.roll(a, k, 1)
                r_first = pltpu.roll(a_prev, k, 0)[None]
                return jnp.where(sub >= k, r, jnp.concatenate([r_first, r[:vregs - 1]], axis=0))

            x1 = shift(xb, prev, 1)
            pair = w1 * xb + w0 * x1
            pair_prev = w1 * prev + w0 * pltpu.roll(prev, 1, 0)
            acc = 0.5 * cb_ref[:, cols] + w3 * xb + w2 * x1 + shift(pair, pair_prev, 2)
            history[c] = xb[vregs - 1]
            xc_ref[rows, cols] = _silu_of_half(acc.reshape(ROW_GROUP, CONV_COLS)).astype(BF16)

        def values_and_steps(r):
            vt_ref[:, rows] = r[:D_DIFF].astype(BF16)
            dtt_ref[:, rows] = jax.nn.softplus(r[D_DIFF:] + dtb_ref[...])

        tables = []

        def rope(a, c):
            if not tables:
                cos = jnp.concatenate([cos_ref[:, rows]] * reps, axis=0).T
                sin = jnp.concatenate([sin_ref[:, rows]] * reps, axis=0).T
                tables.extend([cos, jnp.where(lane < LANES // 2, -sin, sin)])
            ac = a[:, c * LANES:(c + 1) * LANES]
            return ac * tables[0] + pltpu.roll(ac, LANES // 2, 1) * tables[1]

        def keys(a):
            for c in range(DIFF_HEADS):
                k_ref[rows, c * LANES:(c + 1) * LANES] = rope(a, c).astype(BF16)

        def queries(a):
            for c in range(DIFF_HEADS):
                q_ref[rows, c * LANES:(c + 1) * LANES] = (rope(a, c) * (DIFF_HEAD_DIM ** -0.5 * LOG2E)).astype(BF16)

        light = [(lambda: _dot_nt(wvdt_ref[...], h), values_and_steps), (lambda: _dot(h, wz_ref[...]), gate),
                 (lambda: _dot(h, wk_ref[...]), keys), (lambda: _dot(h, wq_ref[...]), queries)]
        for c in range(D_XBC // CONV_COLS):
            for matmul, epilogue in ((functools.partial(lambda c: _dot(h, wxbc_ref[:, c * CONV_COLS:(c + 1) * CONV_COLS]), c),
                                      functools.partial(conv, c)), light[c]):
                result = matmul()
                yield
                epilogue(result)

    for _ in itertools.zip_longest(*[row_group(slice(r, r + ROW_GROUP)) for r in range(0, tm, ROW_GROUP)]):
        pass
    for c, last_rows in history.items():
        halo_ref[:, c * CONV_COLS:(c + 1) * CONV_COLS] = last_rows


def _inproj(x2, g, cos, sin, wz, wxbc, wq, wk, wvdt, cw, cb, dtb_col, S, tm):
    T = x2.shape[0]
    row = lambda n: pl.BlockSpec((tm, n), lambda i: (i, 0))
    col = lambda n: pl.BlockSpec((n, tm), lambda i: (0, i))
    return pl.pallas_call(
        functools.partial(_inproj_kernel, tm=tm, tiles_per_seq=S // tm),
        grid=(T // tm,),
        in_specs=[row(D_MODEL), _const_spec((1, D_MODEL)), col(cos.shape[0]), col(sin.shape[0]),
                  _const_spec(wz.shape), _const_spec(wxbc.shape),
                  _const_spec(wq.shape), _const_spec(wk.shape), _const_spec(wvdt.shape),
                  _const_spec(cw.shape), _const_spec(cb.shape), _const_spec(dtb_col.shape)],
        out_specs=[row(D_SSD), row(D_XBC), col(DT_ROWS), row(D_DIFF), row(D_DIFF), col(D_DIFF)],
        out_shape=[jax.ShapeDtypeStruct((T, D_SSD), BF16),
                   jax.ShapeDtypeStruct((T, D_XBC), BF16),
                   jax.ShapeDtypeStruct((DT_ROWS, T), F32),
                   jax.ShapeDtypeStruct((T, D_DIFF), BF16),
                   jax.ShapeDtypeStruct((T, D_DIFF), BF16),
                   jax.ShapeDtypeStruct((D_DIFF, T), BF16)],
        scratch_shapes=[pltpu.VMEM((SUBLANES, D_XBC), F32)],
        compiler_params=_params("arbitrary"),
        name="inproj",
    )(x2, g, cos, sin, wz, wxbc, wq, wk, wvdt, cw, cb, dtb_col)


def _ssd_chunk(xc_ref, zg_ref, dt_t, alog_ref, dskip_ref, nw_ref, y_ref, state_ref, lane, causal, tri_t):
    da_t = dt_t * (-LOG2E * jnp.exp(alog_ref[...]))
    hi = da_t.astype(BF16)
    rem = da_t - hi.astype(F32)
    mid = rem.astype(BF16)
    lo = (rem - mid.astype(F32)).astype(BF16)
    cum_t = _dot(hi, tri_t) + _dot(mid, tri_t) + _dot(lo, tri_t)
    yield
    last = cum_t[:, CHUNK - 1:CHUNK]
    wrow_t = dt_t * jnp.exp2(last - cum_t)
    cdec = jnp.exp2(last)
    src_t = cum_t - jnp.log2(dt_t)
    cum = jnp.concatenate([cum_t, jnp.zeros((CHUNK - DT_ROWS, CHUNK), F32)], axis=0).T
    yield

    gn = SSD_GROUPS * SSD_STATE
    pairs_per_group = SSD_HEADS // SSD_GROUPS // 2
    y_pairs = []
    for g in range(SSD_GROUPS):
        bg = xc_ref[:, D_SSD + g * SSD_STATE:D_SSD + (g + 1) * SSD_STATE]
        cg = xc_ref[:, D_SSD + gn + g * SSD_STATE:D_SSD + gn + (g + 1) * SSD_STATE]
        cbm = _dot_nt(cg, bg)
        bgt = bg.astype(F32).T
        cg32 = cg.astype(F32)
        yield
        for jp in range(pairs_per_group):
            j = g * pairs_per_group + jp
            xs_pair = xc_ref[:, j * LANES:(j + 1) * LANES]
            lhs, rhs = [], []
            for half in range(2):
                hd = 2 * j + half
                colb = jnp.broadcast_to(cum[:, hd:hd + 1], (CHUNK, LANES))
                dec = jnp.where(causal, jnp.exp2(colb - src_t[hd:hd + 1, :]), 0.0)
                wp = (cbm * dec).astype(BF16)
                gg = (cg32 * jnp.exp2(colb)).astype(BF16)
                in_half = (lane >= SSD_HEAD_DIM) if half else (lane < SSD_HEAD_DIM)
                xs_m = jnp.where(in_half, xs_pair, jnp.zeros_like(xs_pair))
                st = state_ref[hd]
                lhs += [wp, gg]
                rhs += [xs_m, st.astype(BF16)]
                l2 = (bgt * wrow_t[hd:hd + 1, :]).astype(BF16)
                state_ref[hd] = st * cdec[hd:hd + 1, :] + _dot(l2, xs_m)
                yield
            y_pairs.append(_dot(jnp.concatenate(lhs, axis=1), jnp.concatenate(rhs, axis=0)))

    y = jnp.concatenate(y_pairs, axis=1) + dskip_ref[...] * xc_ref[:, :D_SSD].astype(F32)
    y = y * zg_ref[...].astype(F32)
    y_ref[...] = _rms(y, nw_ref[...]).astype(BF16)


def _ssd_kernel(xc_ref, zg_ref, *rest):
    dtt_refs = rest[:SSD_SEQS_PER_STEP]
    alog_ref, dskip_ref, nw_ref, y_ref, state_ref = rest[SSD_SEQS_PER_STEP:]

    @pl.when(pl.program_id(1) == 0)
    def _():
        state_ref[...] = jnp.zeros_like(state_ref)

    row = lax.broadcasted_iota(jnp.int32, (CHUNK, LANES), 0)
    lane = lax.broadcasted_iota(jnp.int32, (CHUNK, LANES), 1)
    causal = row >= lane
    tri_t = jnp.where(row <= lane, 1.0, 0.0).astype(BF16)
    for sc in range(SSD_CHUNKS_PER_STEP):
        t = pl.ds(sc * CHUNK, CHUNK)
        chunks = [_ssd_chunk(xc_ref.at[0, p, t], zg_ref.at[0, p, t], dtt_refs[p][:, sc * CHUNK:(sc + 1) * CHUNK],
                             alog_ref, dskip_ref, nw_ref, y_ref.at[0, p, t], state_ref.at[p], lane, causal, tri_t)
                  for p in range(SSD_SEQS_PER_STEP)]
        for _ in itertools.zip_longest(*chunks):
            pass


def _ssd(xc, zg, dtt, alog_col, dskip, nw, B, S):
    span = SSD_CHUNKS_PER_STEP * CHUNK
    nc = S // span
    nb = SSD_SEQS_PER_STEP
    seqs = lambda a: a.reshape(B // nb, nb, S, a.shape[-1])
    row = lambda n: pl.BlockSpec((1, nb, span, n), lambda b, c: (b, 0, c, 0))
    y = pl.pallas_call(
        _ssd_kernel,
        grid=(B // nb, nc),
        in_specs=[row(D_XBC), row(D_SSD)]
        + [pl.BlockSpec((DT_ROWS, span), functools.partial(lambda p, b, c: (0, (b * nb + p) * nc + c), p))
           for p in range(nb)]
        + [_const_spec(alog_col.shape), _const_spec(dskip.shape), _const_spec(nw.shape)],
        out_specs=row(D_SSD),
        out_shape=jax.ShapeDtypeStruct((B // nb, nb, S, D_SSD), BF16),
        scratch_shapes=[pltpu.VMEM((nb, SSD_HEADS, SSD_STATE, LANES), F32)],
        compiler_params=_params("parallel", "arbitrary"),
        name="ssd",
    )(seqs(xc), seqs(zg), *([dtt] * nb), alog_col, dskip, nw)
    return y.reshape(B * S, D_SSD)


def _diffattn_kernel(lam_ref, subw_ref, q_ref, k_ref, vt_ref, o_ref, sa_scr, sb_scr, acc_scr, m_scr, qc_scr, *,
                     tq, tk, nq, heads):
    hd = 2 * DIFF_HEAD_DIM
    ones = jnp.ones((ONES_ROWS, tk), BF16)
    krow = lax.broadcasted_iota(jnp.int32, (tk, tk), 0)
    qcol = lax.broadcasted_iota(jnp.int32, (tk, tk), 1)
    tri = krow <= qcol
    lane = lax.broadcasted_iota(jnp.int32, (tk, hd), 1)
    comp0 = (lane % DIFF_HEAD_DIM) < DIFF_HEAD_DIM // 2
    lv = lam_ref[...]
    lam = (jnp.exp(jnp.sum(lv[0:1] * lv[1:2], axis=-1, keepdims=True))
           - jnp.exp(jnp.sum(lv[2:3] * lv[3:4], axis=-1, keepdims=True)) + LAMBDA_INIT)

    steps = []
    for i in range(nq):
        steps += [(i, kb, "full") for kb in range(2 * i)] + [(i, 2 * i, "diag0"), (i, 2 * i + 1, "diag1")]

    def query_cols(kind):
        return (tk, tq) if kind == "diag1" else (0, tq)

    def head(hh):
        acc = acc_scr.at[hh]
        mx = m_scr.at[hh]
        bufs = (sa_scr.at[hh], sb_scr.at[hh])
        feat = slice(hh * hd, (hh + 1) * hd)
        qc = qc_scr.at[hh]
        for r in range(0, nq * tq, tk):
            qblk = q_ref[r:r + tk, feat]
            qc[0, r:r + tk, :] = jnp.where(comp0, qblk, jnp.zeros_like(qblk))
            qc[1, r:r + tk, :] = jnp.where(comp0, jnp.zeros_like(qblk), qblk)

        def scores(step, s_ref):
            i, kb, kind = step
            lo, hi = query_cols(kind)
            kblk = k_ref[kb * tk:(kb + 1) * tk, feat]
            for c in range(2):
                s = _dot_nt(kblk, qc[c, i * tq + lo:i * tq + hi, :])
                if kind == "diag0":
                    s = jnp.concatenate([jnp.where(tri, s[:, :tk], -jnp.inf), s[:, tk:]], axis=1)
                elif kind == "diag1":
                    s = jnp.where(tri, s, -jnp.inf)
                s_ref[c, :, lo:hi] = s

        def softmax_pv(step, s_ref):
            i, kb, kind = step
            lo, hi = query_cols(kind)
            par = i % 2
            lhs = jnp.concatenate([vt_ref[feat, kb * tk:(kb + 1) * tk], ones], axis=0)
            for c in range(2):
                m_blk = jnp.max(s_ref[c, :, lo:hi], axis=0, keepdims=True)
                if kb == 0:
                    m_new = m_blk
                    acc[par, c, :, lo:hi] = _dot(lhs, jnp.exp2(s_ref[c, :, lo:hi] - m_new).astype(BF16))
                else:
                    m_old = mx[par, c, :, lo:hi]
                    m_new = jnp.maximum(m_old, m_blk)
                    alpha = jnp.exp2(m_old - m_new)
                    p = jnp.exp2(s_ref[c, :, lo:hi] - m_new).astype(BF16)
                    acc[par, c, :, lo:hi] = alpha * acc[par, c, :, lo:hi] + _dot(lhs, p)
                mx[par, c, :, lo:hi] = m_new

        def finish(i):
            par = i % 2
            a0 = acc[par, 0]
            a1 = acc[par, 1]
            ot = a0[:hd] * (1.0 / a0[hd:hd + 1]) - a1[:hd] * (lam / a1[hd:hd + 1])
            ms = jnp.mean(ot * ot, axis=0, keepdims=True)
            ot = ot * (lax.rsqrt(ms + NORM_EPS) * (1.0 - LAMBDA_INIT)) * subw_ref[...]
            o_ref[i * tq:(i + 1) * tq, feat] = ot.T.astype(BF16)

        scores(steps[0], bufs[0])
        yield
        for n, step in enumerate(steps):
            if n + 1 < len(steps):
                scores(steps[n + 1], bufs[(n + 1) % 2])
                yield
            softmax_pv(step, bufs[n % 2])
            yield
            if step[2] == "diag1":
                finish(step[0])

    for _ in itertools.zip_longest(*[head(hh) for hh in range(heads)]):
        pass


def _diffattn(lamv, subw_col, q, k, vt, B, S, tq):
    nq = S // tq
    tk = tq // 2
    hd = 2 * DIFF_HEAD_DIM
    hp = DIFF_HEADS_PER_STEP
    return pl.pallas_call(
        functools.partial(_diffattn_kernel, tq=tq, tk=tk, nq=nq, heads=hp),
        grid=(B, DIFF_HEADS // hp),
        in_specs=[_const_spec(lamv.shape), _const_spec(subw_col.shape),
                  pl.BlockSpec((S, hd * hp), lambda b, h: (b, h)),
                  pl.BlockSpec((S, hd * hp), lambda b, h: (b, h)),
                  pl.BlockSpec((hd * hp, S), lambda b, h: (h, b))],
        out_specs=pl.BlockSpec((S, hd * hp), lambda b, h: (b, h)),
        out_shape=jax.ShapeDtypeStruct((B * S, D_DIFF), BF16),
        scratch_shapes=[pltpu.VMEM((hp, 2, tk, tq), F32),
                        pltpu.VMEM((hp, 2, tk, tq), F32),
                        pltpu.VMEM((hp, 2, 2, hd + ONES_ROWS, tq), F32),
                        pltpu.VMEM((hp, 2, 2, 1, tq), F32),
                        pltpu.VMEM((hp, 2, S, hd), BF16)],
        compiler_params=_params("parallel", "parallel"),
        name="diffattn",
    )(lamv, subw_col, q, k, vt)


def _memkv_rows(rows, mem_ref, g_ref, wk_ref, wv_ref, k_ref, v_ref):
    h = _rms(mem_ref[rows, :], g_ref[...]).astype(BF16)
    k = _dot(h, wk_ref[...])
    yield
    k_ref[rows, :] = k.astype(BF16)
    v = _dot(h, wv_ref[...])
    yield
    v_ref[rows, :] = v.astype(BF16)


def _round_weights_once(pairs):
    @pl.when(pl.program_id(0) == 0)
    def _():
        for src, dst in pairs:
            for r in range(0, src.shape[0], WEIGHT_CAST_ROWS):
                dst[r:r + WEIGHT_CAST_ROWS, :] = src[r:r + WEIGHT_CAST_ROWS, :].astype(BF16)


def _split_inproj_weights(w_ref, wz_ref, wxbc_ref, wq_ref, wk_ref, wvdt_ref):
    rows = w_ref.shape[0]
    o1 = D_SSD + D_XBC
    wz_ref[...] = w_ref[:, :D_SSD].astype(BF16)
    wxbc_ref[...] = w_ref[:, D_SSD:o1].astype(BF16)
    tail = w_ref[:, o1:]
    q0 = SSD_HEADS
    half = DIFF_HEAD_DIM // 2

    def head_order(cols):
        pieces = []
        for h in range(DIFF_HEADS):
            b = h * LANES
            pieces += [cols[:, b + s:b + s + half] for s in (0, 2 * half, half, 3 * half)]
        return jnp.concatenate(pieces, axis=1)

    wq_ref[...] = head_order(tail[:, q0:q0 + D_DIFF]).astype(BF16)
    wk_ref[...] = head_order(tail[:, q0 + D_DIFF:q0 + 2 * D_DIFF]).astype(BF16)
    wvdt_ref[:D_DIFF, :] = tail[:, q0 + 2 * D_DIFF:].T.astype(BF16)
    dt = jnp.concatenate([tail[:, :q0], jnp.zeros((rows, LANES - q0), F32)], axis=1)
    wvdt_ref[D_DIFF:, :] = dt.T[:DT_ROWS].astype(BF16)


def _memkv_kernel(mem_ref, g_ref, wk32_ref, wv32_ref, win_ref, k_ref, v_ref, wz_ref, wxbc_ref, wq_ref, wkd_ref,
                  wvdt_ref, wk_ref, wv_ref):
    _round_weights_once([(wk32_ref, wk_ref), (wv32_ref, wv_ref)])
    _split_inproj_weights(win_ref, wz_ref, wxbc_ref, wq_ref, wkd_ref, wvdt_ref)
    tm = mem_ref.shape[0]
    groups = [_memkv_rows(slice(r, r + ROW_GROUP), mem_ref, g_ref, wk_ref, wv_ref, k_ref, v_ref)
              for r in range(0, tm, ROW_GROUP)]
    for _ in itertools.zip_longest(*groups):
        pass


def _memkv(mem2, g, wk, wv, w_in, tm):
    R = mem2.shape[0]
    steps = R // tm
    kdim = w_in.shape[0]
    assert kdim % steps == 0 and (kdim // steps) % LANES == 0
    kp = kdim // steps
    row = pl.BlockSpec((tm, D_MODEL), lambda i: (i, 0))
    piece = lambda n: pl.BlockSpec((kp, n), lambda i: (i, 0))
    outs = pl.pallas_call(
        _memkv_kernel,
        grid=(steps,),
        in_specs=[row, _const_spec(g.shape), _const_spec(wk.shape), _const_spec(wv.shape), piece(w_in.shape[1])],
        out_specs=[row, row, piece(D_SSD), piece(D_XBC), piece(D_DIFF), piece(D_DIFF),
                   pl.BlockSpec((D_DIFF + DT_ROWS, kp), lambda i: (0, i))],
        out_shape=[jax.ShapeDtypeStruct((R, D_MODEL), BF16)] * 2
        + [jax.ShapeDtypeStruct((kdim, n), BF16) for n in (D_SSD, D_XBC, D_DIFF, D_DIFF)]
        + [jax.ShapeDtypeStruct((D_DIFF + DT_ROWS, kdim), BF16)],
        scratch_shapes=[pltpu.VMEM(wk.shape, BF16), pltpu.VMEM(wv.shape, BF16)],
        compiler_params=_params("arbitrary"),
        name="memkv",
    )(mem2, g, wk, wv, w_in)
    return outs[:2], outs[2:]


def _mixmem_rows(rows, ys_ref, yd_ref, x_ref, km_ref, vm_ref, wo_ref, gmix_ref, gq_ref, wmq_ref,
                 wmo_ref, gmem_ref, x2_ref):
    mixed = _dot(ys_ref[rows, :], wo_ref[:D_SSD, :]) + _dot(yd_ref[rows, :], wo_ref[D_SSD:, :])
    yield
    x1 = x_ref[rows, :] + _rms(mixed, gmix_ref[...])
    qm = _dot(_rms(x1, gq_ref[...]).astype(BF16), wmq_ref[...])
    yield
    qm = (qm * (MEM_HEAD_DIM ** -0.5)).astype(BF16)
    outs = []
    for hd in range(MEM_HEADS):
        sl = slice(hd * MEM_HEAD_DIM, (hd + 1) * MEM_HEAD_DIM)
        s = _dot_nt(qm[:, sl], km_ref[:, sl])
        yield
        e = jnp.exp(s - jnp.max(s, axis=-1, keepdims=True))
        p = e / jnp.sum(e, axis=-1, keepdims=True)
        outs.append(_dot(p.astype(BF16), vm_ref[:, sl]).astype(BF16))
        yield
    c = _dot(jnp.concatenate(outs, axis=1), wmo_ref[...])
    yield
    x2_ref[rows, :] = x1 + _rms(c, gmem_ref[...])


def _mixmem_kernel(ys_ref, yd_ref, x_ref, km_ref, vm_ref, wo32_ref, gmix_ref, gq_ref, wmq32_ref, wmo32_ref,
                   gmem_ref, wup32_ref, wdn32_ref, x2_ref, wup_ref, wdn_ref, wo_ref, wmq_ref, wmo_ref):
    _round_weights_once([(wo32_ref, wo_ref), (wmq32_ref, wmq_ref), (wmo32_ref, wmo_ref)])
    for src, dst in ((wup32_ref, wup_ref), (wdn32_ref, wdn_ref)):
        step = min(WEIGHT_CAST_ROWS, src.shape[0])
        for r in range(0, src.shape[0], step):
            dst[r:r + step, :] = src[r:r + step, :].astype(BF16)
    tm = x_ref.shape[0]
    groups = [_mixmem_rows(slice(r, r + ROW_GROUP), ys_ref, yd_ref, x_ref, km_ref, vm_ref, wo_ref, gmix_ref,
                           gq_ref, wmq_ref, wmo_ref, gmem_ref, x2_ref) for r in range(0, tm, ROW_GROUP)]
    for _ in itertools.zip_longest(*groups):
        pass


def _mixmem(ys, yd, x2, km, vm, wo, gmix, gq, wmq, wmo, gmem, wup, wdn, S, M, tm):
    T = x2.shape[0]
    steps = T // tm
    per_b = S // tm
    row = lambda n: pl.BlockSpec((tm, n), lambda i: (i, 0))
    mem = pl.BlockSpec((M, D_MODEL), lambda i: (i // per_b, 0))
    consts = (wo, gmix, gq, wmq, wmo, gmem)
    assert wup.shape[0] % steps == 0 and wdn.shape[0] % steps == 0
    piece = lambda w: pl.BlockSpec((w.shape[0] // steps, w.shape[1]), lambda i: (i, 0))
    return pl.pallas_call(
        _mixmem_kernel,
        grid=(steps,),
        in_specs=[row(D_SSD), row(D_DIFF), row(D_MODEL), mem, mem] + [_const_spec(c.shape) for c in consts]
        + [piece(wup), piece(wdn)],
        out_specs=[row(D_MODEL), piece(wup), piece(wdn)],
        out_shape=[jax.ShapeDtypeStruct((T, D_MODEL), F32), jax.ShapeDtypeStruct(wup.shape, BF16),
                   jax.ShapeDtypeStruct(wdn.shape, BF16)],
        scratch_shapes=[pltpu.VMEM(w.shape, BF16) for w in (wo, wmq, wmo)],
        compiler_params=_params("arbitrary"),
        name="mixmem",
    )(ys, yd, x2, km, vm, *consts, wup, wdn)


def _mlp_rows(rows, tf, x_ref, gpre_ref, wup_ref, wdn_ref, gpost_ref, o_ref):
    x = x_ref[rows, :]
    h = _rms(x, gpre_ref[...]).astype(BF16)
    acc = jnp.zeros(x.shape, F32)
    for c in range(D_FF // tf):
        u = _dot(h, wup_ref[:, c * tf:(c + 1) * tf])
        yield
        u = jnp.maximum(u, 0.0)
        acc = acc + _dot((u * u).astype(BF16), wdn_ref[c * tf:(c + 1) * tf, :])
        yield
    o_ref[rows, :] = x + _rms(acc, gpost_ref[...])


def _mlp_kernel(*refs, tf):
    tm = refs[0].shape[0]
    groups = [_mlp_rows(slice(r, r + ROW_GROUP), tf, *refs) for r in range(0, tm, ROW_GROUP)]
    for _ in itertools.zip_longest(*groups):
        pass


def _mlp(x2, gpre, wup, wdn, gpost, tm, tf):
    T = x2.shape[0]
    row = pl.BlockSpec((tm, D_MODEL), lambda i: (i, 0))
    return pl.pallas_call(
        functools.partial(_mlp_kernel, tf=tf),
        grid=(T // tm,),
        in_specs=[row, _const_spec(gpre.shape), _const_spec(wup.shape), _const_spec(wdn.shape),
                  _const_spec(gpost.shape)],
        out_specs=row,
        out_shape=jax.ShapeDtypeStruct((T, D_MODEL), F32),
        compiler_params=_params("parallel"),
        name="mlp",
    )(x2, gpre, wup, wdn, gpost)


def _rope_tables(positions):
    inv = ROPE_THETA ** (-jnp.arange(0, DIFF_HEAD_DIM, 2, dtype=F32) / DIFF_HEAD_DIM)
    ang = inv.reshape(-1, 1) * positions.astype(F32).reshape(1, -1)
    return jnp.cos(ang), jnp.sin(ang)


def kernel(x, mem, positions, norm_mix_pre, norm_mix_post, norm_mem_q, norm_mem_kv, norm_mem_post,
           norm_mlp_pre, norm_mlp_post, w_in, conv_w, conv_b, dt_bias, a_log, d_skip, ssd_norm_w,
           lambda_q1, lambda_k1, lambda_q2, lambda_k2, subln_w, w_out, w_mq, w_mk, w_mv, w_mo, w_up, w_down):
    B, S, _ = x.shape
    M = mem.shape[1]
    T = B * S
    assert norm_mix_pre.shape[0] == 1, "single-layer trunk"
    x2 = x.reshape(T, D_MODEL)
    cos, sin = _rope_tables(positions)

    (km, vm), (wz, wxbc, wq, wk, wvdt) = _memkv(mem.reshape(B * M, D_MODEL), norm_mem_kv, w_mk[0], w_mv[0],
                                                w_in[0], tm=min(ROW_TILE, B * M))
    head_col = lambda p: jnp.pad(p.reshape(SSD_HEADS, 1), ((0, DT_ROWS - SSD_HEADS), (0, 0)))

    zg, xc, dtt, q, k, vt = _inproj(x2, norm_mix_pre, cos, sin, wz, wxbc, wq, wk, wvdt,
                                    conv_w[0, :, 0, :], conv_b, head_col(dt_bias), S, tm=INPROJ_ROWS)

    y_ssd = _ssd(xc, zg, dtt, head_col(a_log), jnp.repeat(d_skip, SSD_HEAD_DIM, axis=1), ssd_norm_w, B, S)

    lamv = jnp.concatenate([lambda_q1, lambda_k1, lambda_q2, lambda_k2], axis=0)
    y_diff = _diffattn(lamv, subln_w.reshape(2 * DIFF_HEAD_DIM, 1), q, k, vt, B, S, tq=DIFF_Q_BLOCK)

    x2b, wup, wdn = _mixmem(y_ssd, y_diff, x2, km, vm, w_out[0], norm_mix_post, norm_mem_q, w_mq[0], w_mo[0],
                            norm_mem_post, w_up[0], w_down[0], S, M, tm=ROW_TILE)
    out = _mlp(x2b, norm_mlp_pre, wup, wdn, norm_mlp_post, tm=ROW_TILE, tf=MLP_FF_CHUNK)
    return out.reshape(B, S, D_MODEL)
```

```python
import functools
import itertools
import math

import jax
import jax.numpy as jnp
from jax import lax
from jax.experimental import pallas as pl
from jax.experimental.pallas import tpu as pltpu

F32 = jnp.float32
BF16 = jnp.bfloat16

D_MODEL = 1024
D_SSD = 512
SSD_HEAD_DIM = 64
SSD_HEADS = D_SSD // SSD_HEAD_DIM
SSD_GROUPS = 2
SSD_STATE = 128
CONV_WIDTH = 4
CHUNK = 128
D_XBC = D_SSD + 2 * SSD_GROUPS * SSD_STATE
D_DIFF = D_MODEL - D_SSD
DIFF_HEAD_DIM = 64
DIFF_HEADS = D_DIFF // (2 * DIFF_HEAD_DIM)
ROPE_THETA = 10000.0
MEM_HEADS = 4
MEM_HEAD_DIM = D_MODEL // MEM_HEADS
D_FF = 4 * D_MODEL
NORM_EPS = 1e-6
LAMBDA_INIT = 0.8 - 0.6 * math.exp(-0.3 * 0)

LOG2E = math.log2(math.e)
LANES = 128
SUBLANES = 8
ONES_ROWS = 16
DT_ROWS = 16
CONV_COLS = 256
SSD_SEQS_PER_STEP = 4
SSD_CHUNKS_PER_STEP = 4
DIFF_HEADS_PER_STEP = 2
ROW_GROUP = 256
INPROJ_ROWS = 512
ROW_TILE = 1024
DIFF_Q_BLOCK = 512
MLP_FF_CHUNK = 1024
WEIGHT_CAST_ROWS = 128
VMEM_LIMIT = 56 * 1024 * 1024


def _dot(a, b):
    return jnp.dot(a, b, preferred_element_type=F32)


def _dot_nt(a, b):
    return lax.dot_general(a, b, (((1,), (1,)), ((), ())), preferred_element_type=F32)


def _rms(x, g):
    ms = jnp.mean(x * x, axis=-1, keepdims=True)
    return x * lax.rsqrt(ms + NORM_EPS) * g


def _silu_of_half(half):
    return half + half * jnp.tanh(half)


def _const_spec(shape):
    nd = len(shape)
    return pl.BlockSpec(shape, lambda *_: (0,) * nd, pipeline_mode=pl.Buffered(1))


def _params(*sem):
    return pltpu.CompilerParams(dimension_semantics=sem, vmem_limit_bytes=VMEM_LIMIT)


def _inproj_kernel(x_ref, g_ref, cos_ref, sin_ref, wz_ref, wxbc_ref, wq_ref, wk_ref, wvdt_ref,
                   cw_ref, cb_ref, dtb_ref,
                   zg_ref, xc_ref, dtt_ref, q_ref, k_ref, vt_ref, halo_ref, *, tm, tiles_per_seq):
    @pl.when(pl.program_id(0) % tiles_per_seq == 0)
    def _():
        halo_ref[...] = jnp.zeros_like(halo_ref)

    vregs = ROW_GROUP // SUBLANES
    sub = lax.broadcasted_iota(jnp.int32, (1, SUBLANES, 1), 1)
    reps = LANES // (DIFF_HEAD_DIM // 2)
    lane = lax.broadcasted_iota(jnp.int32, (ROW_GROUP, LANES), 1)
    history = {}

    def row_group(rows):
        h = _rms(x_ref[rows, :], g_ref[...]).astype(BF16)

        def gate(z):
            zg_ref[rows, :] = _silu_of_half(0.5 * z).astype(BF16)

        def conv(c, xbc):
            cols = slice(c * CONV_COLS, (c + 1) * CONV_COLS)
            xb = xbc.reshape(vregs, SUBLANES, CONV_COLS)
            prev = history[c] if c in history else halo_ref[:, cols]
            w0, w1, w2, w3 = [0.5 * cw_ref[j:j + 1, cols] for j in range(CONV_WIDTH)]

            def shift(a, a_prev, k):
                r = pltpu.roll(a, k, 1)
                r_first = pltpu.roll(a_prev, k, 0)[None]
                return jnp.where(sub >= k, r, jnp.concatenate([r_first, r[:vregs - 1]], axis=0))

            x1 = shift(xb, prev, 1)
            pair = w1 * xb + w0 * x1
            pair_prev = w1 * prev + w0 * pltpu.roll(prev, 1, 0)
            acc = 0.5 * cb_ref[:, cols] + w3 * xb + w2 * x1 + shift(pair, pair_prev, 2)
            history[c] = xb[vregs - 1]
            xc_ref[rows, cols] = _silu_of_half(acc.reshape(ROW_GROUP, CONV_COLS)).astype(BF16)

        def values_and_steps(r):
            vt_ref[:, rows] = r[:D_DIFF].astype(BF16)
            dtt_ref[:, rows] = jax.nn.softplus(r[D_DIFF:] + dtb_ref[...])

        tables = []

        def rope(a, c):
            if not tables:
                cos = jnp.concatenate([cos_ref[:, rows]] * reps, axis=0).T
                sin = jnp.concatenate([sin_ref[:, rows]] * reps, axis=0).T
                tables.extend([cos, jnp.where(lane < LANES // 2, -sin, sin)])
            ac = a[:, c * LANES:(c + 1) * LANES]
            return ac * tables[0] + pltpu.roll(ac, LANES // 2, 1) * tables[1]

        def keys(a):
            for c in range(DIFF_HEADS):
                k_ref[rows, c * LANES:(c + 1) * LANES] = rope(a, c).astype(BF16)

        def queries(a):
            for c in range(DIFF_HEADS):
                q_ref[rows, c * LANES:(c + 1) * LANES] = (rope(a, c) * (DIFF_HEAD_DIM ** -0.5 * LOG2E)).astype(BF16)

        light = [(lambda: _dot_nt(wvdt_ref[...], h), values_and_steps), (lambda: _dot_nt(h, wz_ref[...]), gate),
                 (lambda: _dot_nt(h, wk_ref[...]), keys), (lambda: _dot_nt(h, wq_ref[...]), queries)]
        for c in range(D_XBC // CONV_COLS):
            for matmul, epilogue in ((functools.partial(lambda c: _dot_nt(h, wxbc_ref[c * CONV_COLS:(c + 1) * CONV_COLS, :]), c),
                                      functools.partial(conv, c)), light[c]):
                result = matmul()
                yield
                epilogue(result)

    for _ in itertools.zip_longest(*[row_group(slice(r, r + ROW_GROUP)) for r in range(0, tm, ROW_GROUP)]):
        pass
    for c, last_rows in history.items():
        halo_ref[:, c * CONV_COLS:(c + 1) * CONV_COLS] = last_rows


def _inproj(x2, g, cos, sin, wz, wxbc, wq, wk, wvdt, cw, cb, dtb_col, S, tm):
    T = x2.shape[0]
    row = lambda n: pl.BlockSpec((tm, n), lambda i: (i, 0))
    col = lambda n: pl.BlockSpec((n, tm), lambda i: (0, i))
    return pl.pallas_call(
        functools.partial(_inproj_kernel, tm=tm, tiles_per_seq=S // tm),
        grid=(T // tm,),
        in_specs=[row(D_MODEL), _const_spec((1, D_MODEL)), col(cos.shape[0]), col(sin.shape[0]),
                  _const_spec(wz.shape), _const_spec(wxbc.shape),
                  _const_spec(wq.shape), _const_spec(wk.shape), _const_spec(wvdt.shape),
                  _const_spec(cw.shape), _const_spec(cb.shape), _const_spec(dtb_col.shape)],
        out_specs=[row(D_SSD), row(D_XBC), col(DT_ROWS), row(D_DIFF), row(D_DIFF), col(D_DIFF)],
        out_shape=[jax.ShapeDtypeStruct((T, D_SSD), BF16),
                   jax.ShapeDtypeStruct((T, D_XBC), BF16),
                   jax.ShapeDtypeStruct((DT_ROWS, T), F32),
                   jax.ShapeDtypeStruct((T, D_DIFF), BF16),
                   jax.ShapeDtypeStruct((T, D_DIFF), BF16),
                   jax.ShapeDtypeStruct((D_DIFF, T), BF16)],
        scratch_shapes=[pltpu.VMEM((SUBLANES, D_XBC), F32)],
        compiler_params=_params("arbitrary"),
        name="inproj",
    )(x2, g, cos, sin, wz, wxbc, wq, wk, wvdt, cw, cb, dtb_col)


def _ssd_chunk(xc_ref, zg_ref, dt_t, alog_ref, dskip_ref, nw_ref, y_ref, state_ref, lane, causal, tri_t):
    da_t = dt_t * (-LOG2E * jnp.exp(alog_ref[...]))
    hi = da_t.astype(BF16)
    rem = da_t - hi.astype(F32)
    mid = rem.astype(BF16)
    lo = (rem - mid.astype(F32)).astype(BF16)
    cum_t = _dot(hi, tri_t) + _dot(mid, tri_t) + _dot(lo, tri_t)
    yield
    last = cum_t[:, CHUNK - 1:CHUNK]
    wrow_t = dt_t * jnp.exp2(last - cum_t)
    cdec = jnp.exp2(last)
    src_t = cum_t - jnp.log2(dt_t)
    cum = jnp.concatenate([cum_t, jnp.zeros((CHUNK - DT_ROWS, CHUNK), F32)], axis=0).T
    yield

    gn = SSD_GROUPS * SSD_STATE
    pairs_per_group = SSD_HEADS // SSD_GROUPS // 2
    y_pairs = []
    for g in range(SSD_GROUPS):
        bg = xc_ref[:, D_SSD + g * SSD_STATE:D_SSD + (g + 1) * SSD_STATE]
        cg = xc_ref[:, D_SSD + gn + g * SSD_STATE:D_SSD + gn + (g + 1) * SSD_STATE]
        cbm = _dot_nt(cg, bg)
        bgt = bg.astype(F32).T
        cg32 = cg.astype(F32)
        yield
        for jp in range(pairs_per_group):
            j = g * pairs_per_group + jp
            xs_pair = xc_ref[:, j * LANES:(j + 1) * LANES]
            lhs, rhs = [], []
            for half in range(2):
                hd = 2 * j + half
                colb = jnp.broadcast_to(cum[:, hd:hd + 1], (CHUNK, LANES))
                dec = jnp.where(causal, jnp.exp2(colb - src_t[hd:hd + 1, :]), 0.0)
                wp = (cbm * dec).astype(BF16)
                gg = (cg32 * jnp.exp2(colb)).astype(BF16)
                in_half = (lane >= SSD_HEAD_DIM) if half else (lane < SSD_HEAD_DIM)
                xs_m = jnp.where(in_half, xs_pair, jnp.zeros_like(xs_pair))
                st = state_ref[hd]
                lhs += [wp, gg]
                rhs += [xs_m, st.astype(BF16)]
                l2 = (bgt * wrow_t[hd:hd + 1, :]).astype(BF16)
                state_ref[hd] = st * cdec[hd:hd + 1, :] + _dot(l2, xs_m)
                yield
            y_pairs.append(_dot(jnp.concatenate(lhs, axis=1), jnp.concatenate(rhs, axis=0)))

    y = jnp.concatenate(y_pairs, axis=1) + dskip_ref[...] * xc_ref[:, :D_SSD].astype(F32)
    y = y * zg_ref[...].astype(F32)
    y_ref[...] = _rms(y, nw_ref[...]).astype(BF16)


def _ssd_kernel(xc_ref, zg_ref, *rest):
    dtt_refs = rest[:SSD_SEQS_PER_STEP]
    alog_ref, dskip_ref, nw_ref, y_ref, state_ref = rest[SSD_SEQS_PER_STEP:]

    @pl.when(pl.program_id(1) == 0)
    def _():
        state_ref[...] = jnp.zeros_like(state_ref)

    row = lax.broadcasted_iota(jnp.int32, (CHUNK, LANES), 0)
    lane = lax.broadcasted_iota(jnp.int32, (CHUNK, LANES), 1)
    causal = row >= lane
    tri_t = jnp.where(row <= lane, 1.0, 0.0).astype(BF16)
    for sc in range(SSD_CHUNKS_PER_STEP):
        t = pl.ds(sc * CHUNK, CHUNK)
        chunks = [_ssd_chunk(xc_ref.at[0, p, t], zg_ref.at[0, p, t], dtt_refs[p][:, sc * CHUNK:(sc + 1) * CHUNK],
                             alog_ref, dskip_ref, nw_ref, y_ref.at[0, p, t], state_ref.at[p], lane, causal, tri_t)
                  for p in range(SSD_SEQS_PER_STEP)]
        for _ in itertools.zip_longest(*chunks):
            pass


def _ssd(xc, zg, dtt, alog_col, dskip, nw, B, S):
    span = SSD_CHUNKS_PER_STEP * CHUNK
    nc = S // span
    nb = SSD_SEQS_PER_STEP
    seqs = lambda a: a.reshape(B // nb, nb, S, a.shape[-1])
    row = lambda n: pl.BlockSpec((1, nb, span, n), lambda b, c: (b, 0, c, 0))
    y = pl.pallas_call(
        _ssd_kernel,
        grid=(B // nb, nc),
        in_specs=[row(D_XBC), row(D_SSD)]
        + [pl.BlockSpec((DT_ROWS, span), functools.partial(lambda p, b, c: (0, (b * nb + p) * nc + c), p))
           for p in range(nb)]
        + [_const_spec(alog_col.shape), _const_spec(dskip.shape), _const_spec(nw.shape)],
        out_specs=row(D_SSD),
        out_shape=jax.ShapeDtypeStruct((B // nb, nb, S, D_SSD), BF16),
        scratch_shapes=[pltpu.VMEM((nb, SSD_HEADS, SSD_STATE, LANES), F32)],
        compiler_params=_params("parallel", "arbitrary"),
        name="ssd",
    )(seqs(xc), seqs(zg), *([dtt] * nb), alog_col, dskip, nw)
    return y.reshape(B * S, D_SSD)


def _diffattn_kernel(lam_ref, subw_ref, q_ref, k_ref, vt_ref, o_ref, sa_scr, sb_scr, acc_scr, m_scr, qc_scr, *,
                     tq, tk, nq, heads):
    hd = 2 * DIFF_HEAD_DIM
    ones = jnp.ones((ONES_ROWS, tk), BF16)
    krow = lax.broadcasted_iota(jnp.int32, (tk, tk), 0)
    qcol = lax.broadcasted_iota(jnp.int32, (tk, tk), 1)
    tri = krow <= qcol
    lane = lax.broadcasted_iota(jnp.int32, (tk, hd), 1)
    comp0 = (lane % DIFF_HEAD_DIM) < DIFF_HEAD_DIM // 2
    lv = lam_ref[...]
    lam = (jnp.exp(jnp.sum(lv[0:1] * lv[1:2], axis=-1, keepdims=True))
           - jnp.exp(jnp.sum(lv[2:3] * lv[3:4], axis=-1, keepdims=True)) + LAMBDA_INIT)

    steps = []
    for i in range(nq):
        steps += [(i, kb, "full") for kb in range(2 * i)] + [(i, 2 * i, "diag0"), (i, 2 * i + 1, "diag1")]

    def query_cols(kind):
        return (tk, tq) if kind == "diag1" else (0, tq)

    def head(hh):
        acc = acc_scr.at[hh]
        mx = m_scr.at[hh]
        bufs = (sa_scr.at[hh], sb_scr.at[hh])
        feat = slice(hh * hd, (hh + 1) * hd)
        qc = qc_scr.at[hh]
        for r in range(0, nq * tq, tk):
            qblk = q_ref[r:r + tk, feat]
            qc[0, r:r + tk, :] = jnp.where(comp0, qblk, jnp.zeros_like(qblk))
            qc[1, r:r + tk, :] = jnp.where(comp0, jnp.zeros_like(qblk), qblk)

        def scores(step, s_ref):
            i, kb, kind = step
            lo, hi = query_cols(kind)
            kblk = k_ref[kb * tk:(kb + 1) * tk, feat]
            for c in range(2):
                s = _dot_nt(kblk, qc[c, i * tq + lo:i * tq + hi, :])
                if kind == "diag0":
                    s = jnp.concatenate([jnp.where(tri, s[:, :tk], -jnp.inf), s[:, tk:]], axis=1)
                elif kind == "diag1":
                    s = jnp.where(tri, s, -jnp.inf)
                s_ref[c, :, lo:hi] = s

        def softmax_pv(step, s_ref):
            i, kb, kind = step
            lo, hi = query_cols(kind)
            par = i % 2
            lhs = jnp.concatenate([vt_ref[feat, kb * tk:(kb + 1) * tk], ones], axis=0)
            for c in range(2):
                m_blk = jnp.max(s_ref[c, :, lo:hi], axis=0, keepdims=True)
                if kb == 0:
                    m_new = m_blk
                    acc[par, c, :, lo:hi] = _dot(lhs, jnp.exp2(s_ref[c, :, lo:hi] - m_new).astype(BF16))
                else:
                    m_old = mx[par, c, :, lo:hi]
                    m_new = jnp.maximum(m_old, m_blk)
                    alpha = jnp.exp2(m_old - m_new)
                    p = jnp.exp2(s_ref[c, :, lo:hi] - m_new).astype(BF16)
                    acc[par, c, :, lo:hi] = alpha * acc[par, c, :, lo:hi] + _dot(lhs, p)
                mx[par, c, :, lo:hi] = m_new

        def finish(i):
            par = i % 2
            a0 = acc[par, 0]
            a1 = acc[par, 1]
            ot = a0[:hd] * (1.0 / a0[hd:hd + 1]) - a1[:hd] * (lam / a1[hd:hd + 1])
            ms = jnp.mean(ot * ot, axis=0, keepdims=True)
            ot = ot * (lax.rsqrt(ms + NORM_EPS) * (1.0 - LAMBDA_INIT)) * subw_ref[...]
            o_ref[i * tq:(i + 1) * tq, feat] = ot.T.astype(BF16)

        scores(steps[0], bufs[0])
        yield
        for n, step in enumerate(steps):
            if n + 1 < len(steps):
                scores(steps[n + 1], bufs[(n + 1) % 2])
                yield
            softmax_pv(step, bufs[n % 2])
            yield
            if step[2] == "diag1":
                finish(step[0])

    for _ in itertools.zip_longest(*[head(hh) for hh in range(heads)]):
        pass


def _diffattn(lamv, subw_col, q, k, vt, B, S, tq):
    nq = S // tq
    tk = tq // 2
    hd = 2 * DIFF_HEAD_DIM
    hp = DIFF_HEADS_PER_STEP
    return pl.pallas_call(
        functools.partial(_diffattn_kernel, tq=tq, tk=tk, nq=nq, heads=hp),
        grid=(B, DIFF_HEADS // hp),
        in_specs=[_const_spec(lamv.shape), _const_spec(subw_col.shape),
                  pl.BlockSpec((S, hd * hp), lambda b, h: (b, h)),
                  pl.BlockSpec((S, hd * hp), lambda b, h: (b, h)),
                  pl.BlockSpec((hd * hp, S), lambda b, h: (h, b))],
        out_specs=pl.BlockSpec((S, hd * hp), lambda b, h: (b, h)),
        out_shape=jax.ShapeDtypeStruct((B * S, D_DIFF), BF16),
        scratch_shapes=[pltpu.VMEM((hp, 2, tk, tq), F32),
                        pltpu.VMEM((hp, 2, tk, tq), F32),
                        pltpu.VMEM((hp, 2, 2, hd + ONES_ROWS, tq), F32),
                        pltpu.VMEM((hp, 2, 2, 1, tq), F32),
                        pltpu.VMEM((hp, 2, S, hd), BF16)],
        compiler_params=_params("parallel", "parallel"),
        name="diffattn",
    )(lamv, subw_col, q, k, vt)


def _memkv_rows(rows, mem_ref, g_ref, wk_ref, wv_ref, k_ref, v_ref):
    h = _rms(mem_ref[rows, :], g_ref[...]).astype(BF16)
    k = _dot(h, wk_ref[...])
    yield
    k_ref[rows, :] = k.astype(BF16)
    v = _dot(h, wv_ref[...])
    yield
    v_ref[rows, :] = v.astype(BF16)


def _round_weights_once(pairs):
    @pl.when(pl.program_id(0) == 0)
    def _():
        for src, dst in pairs:
            for r in range(0, src.shape[0], WEIGHT_CAST_ROWS):
                dst[r:r + WEIGHT_CAST_ROWS, :] = src[r:r + WEIGHT_CAST_ROWS, :].astype(BF16)


def _split_inproj_weights(w_ref, wz_ref, wxbc_ref, wq_ref, wk_ref, wvdt_ref):
    o1 = D_SSD + D_XBC
    o2 = o1 + SSD_HEADS
    half = DIFF_HEAD_DIM // 2
    wz_ref[...] = w_ref[:D_SSD, :].astype(BF16)
    wxbc_ref[...] = w_ref[D_SSD:o1, :].astype(BF16)
    for dst, base in ((wq_ref, o2), (wk_ref, o2 + D_DIFF)):
        for h in range(DIFF_HEADS):
            for n, s in enumerate((0, 2 * half, half, 3 * half)):
                src = base + h * LANES + s
                dst[h * LANES + n * half:h * LANES + (n + 1) * half, :] = w_ref[src:src + half, :].astype(BF16)
    wvdt_ref[:D_DIFF, :] = w_ref[o2 + 2 * D_DIFF:, :].astype(BF16)
    pad = jnp.zeros((DT_ROWS - SSD_HEADS, w_ref.shape[1]), F32)
    wvdt_ref[D_DIFF:, :] = jnp.concatenate([w_ref[o1:o2, :], pad], axis=0).astype(BF16)


def _memkv_kernel(mem_ref, g_ref, wk32_ref, wv32_ref, win_ref, k_ref, v_ref, wz_ref, wxbc_ref, wq_ref, wkd_ref,
                  wvdt_ref, wk_ref, wv_ref):
    _round_weights_once([(wk32_ref, wk_ref), (wv32_ref, wv_ref)])
    _split_inproj_weights(win_ref, wz_ref, wxbc_ref, wq_ref, wkd_ref, wvdt_ref)
    tm = mem_ref.shape[0]
    groups = [_memkv_rows(slice(r, r + ROW_GROUP), mem_ref, g_ref, wk_ref, wv_ref, k_ref, v_ref)
              for r in range(0, tm, ROW_GROUP)]
    for _ in itertools.zip_longest(*groups):
        pass


def _memkv(mem2, g, wk, wv, w_in_t, tm):
    R = mem2.shape[0]
    steps = R // tm
    kdim = w_in_t.shape[1]
    assert kdim % steps == 0 and (kdim // steps) % LANES == 0
    kp = kdim // steps
    row = pl.BlockSpec((tm, D_MODEL), lambda i: (i, 0))
    piece = lambda n: pl.BlockSpec((n, kp), lambda i: (0, i))
    pieces = (D_SSD, D_XBC, D_DIFF, D_DIFF, D_DIFF + DT_ROWS)
    outs = pl.pallas_call(
        _memkv_kernel,
        grid=(steps,),
        in_specs=[row, _const_spec(g.shape), _const_spec(wk.shape), _const_spec(wv.shape), piece(w_in_t.shape[0])],
        out_specs=[row, row] + [piece(n) for n in pieces],
        out_shape=[jax.ShapeDtypeStruct((R, D_MODEL), BF16)] * 2
        + [jax.ShapeDtypeStruct((n, kdim), BF16) for n in pieces],
        scratch_shapes=[pltpu.VMEM(wk.shape, BF16), pltpu.VMEM(wv.shape, BF16)],
        compiler_params=_params("arbitrary"),
        name="memkv",
    )(mem2, g, wk, wv, w_in_t)
    return outs[:2], outs[2:]


def _mixmem_rows(rows, ys_ref, yd_ref, x_ref, km_ref, vm_ref, wo_ref, gmix_ref, gq_ref, wmq_ref,
                 wmo_ref, gmem_ref, x2_ref):
    mixed = _dot(ys_ref[rows, :], wo_ref[:D_SSD, :]) + _dot(yd_ref[rows, :], wo_ref[D_SSD:, :])
    yield
    x1 = x_ref[rows, :] + _rms(mixed, gmix_ref[...])
    qm = _dot(_rms(x1, gq_ref[...]).astype(BF16), wmq_ref[...])
    yield
    qm = (qm * (MEM_HEAD_DIM ** -0.5)).astype(BF16)
    outs = []
    for hd in range(MEM_HEADS):
        sl = slice(hd * MEM_HEAD_DIM, (hd + 1) * MEM_HEAD_DIM)
        s = _dot_nt(qm[:, sl], km_ref[:, sl])
        yield
        e = jnp.exp(s - jnp.max(s, axis=-1, keepdims=True))
        p = e / jnp.sum(e, axis=-1, keepdims=True)
        outs.append(_dot(p.astype(BF16), vm_ref[:, sl]).astype(BF16))
        yield
    c = _dot(jnp.concatenate(outs, axis=1), wmo_ref[...])
    yield
    x2_ref[rows, :] = x1 + _rms(c, gmem_ref[...])


def _mixmem_kernel(ys_ref, yd_ref, x_ref, km_ref, vm_ref, wo32_ref, gmix_ref, gq_ref, wmq32_ref, wmo32_ref,
                   gmem_ref, wup32_ref, wdn32_ref, x2_ref, wup_ref, wdn_ref, wo_ref, wmq_ref, wmo_ref):
    _round_weights_once([(wo32_ref, wo_ref), (wmq32_ref, wmq_ref), (wmo32_ref, wmo_ref)])
    for src, dst in ((wup32_ref, wup_ref), (wdn32_ref, wdn_ref)):
        step = min(WEIGHT_CAST_ROWS, src.shape[0])
        for r in range(0, src.shape[0], step):
            dst[r:r + step, :] = src[r:r + step, :].astype(BF16)
    tm = x_ref.shape[0]
    groups = [_mixmem_rows(slice(r, r + ROW_GROUP), ys_ref, yd_ref, x_ref, km_ref, vm_ref, wo_ref, gmix_ref,
                           gq_ref, wmq_ref, wmo_ref, gmem_ref, x2_ref) for r in range(0, tm, ROW_GROUP)]
    for _ in itertools.zip_longest(*groups):
        pass


def _mixmem(ys, yd, x2, km, vm, wo, gmix, gq, wmq, wmo, gmem, wup, wdn, S, M, tm):
    T = x2.shape[0]
    steps = T // tm
    per_b = S // tm
    row = lambda n: pl.BlockSpec((tm, n), lambda i: (i, 0))
    mem = pl.BlockSpec((M, D_MODEL), lambda i: (i // per_b, 0))
    consts = (wo, gmix, gq, wmq, wmo, gmem)
    assert wup.shape[0] % steps == 0 and wdn.shape[0] % steps == 0
    piece = lambda w: pl.BlockSpec((w.shape[0] // steps, w.shape[1]), lambda i: (i, 0))
    return pl.pallas_call(
        _mixmem_kernel,
        grid=(steps,),
        in_specs=[row(D_SSD), row(D_DIFF), row(D_MODEL), mem, mem] + [_const_spec(c.shape) for c in consts]
        + [piece(wup), piece(wdn)],
        out_specs=[row(D_MODEL), piece(wup), piece(wdn)],
        out_shape=[jax.ShapeDtypeStruct((T, D_MODEL), F32), jax.ShapeDtypeStruct(wup.shape, BF16),
                   jax.ShapeDtypeStruct(wdn.shape, BF16)],
        scratch_shapes=[pltpu.VMEM(w.shape, BF16) for w in (wo, wmq, wmo)],
        compiler_params=_params("arbitrary"),
        name="mixmem",
    )(ys, yd, x2, km, vm, *consts, wup, wdn)


def _mlp_rows(rows, tf, x_ref, gpre_ref, wup_ref, wdn_ref, gpost_ref, o_ref):
    x = x_ref[rows, :]
    h = _rms(x, gpre_ref[...]).astype(BF16)
    acc = jnp.zeros(x.shape, F32)
    for c in range(D_FF // tf):
        u = _dot(h, wup_ref[:, c * tf:(c + 1) * tf])
        yield
        u = jnp.maximum(u, 0.0)
        acc = acc + _dot((u * u).astype(BF16), wdn_ref[c * tf:(c + 1) * tf, :])
        yield
    o_ref[rows, :] = x + _rms(acc, gpost_ref[...])


def _mlp_kernel(*refs, tf):
    tm = refs[0].shape[0]
    groups = [_mlp_rows(slice(r, r + ROW_GROUP), tf, *refs) for r in range(0, tm, ROW_GROUP)]
    for _ in itertools.zip_longest(*groups):
        pass


def _mlp(x2, gpre, wup, wdn, gpost, tm, tf):
    T = x2.shape[0]
    row = pl.BlockSpec((tm, D_MODEL), lambda i: (i, 0))
    return pl.pallas_call(
        functools.partial(_mlp_kernel, tf=tf),
        grid=(T // tm,),
        in_specs=[row, _const_spec(gpre.shape), _const_spec(wup.shape), _const_spec(wdn.shape),
                  _const_spec(gpost.shape)],
        out_specs=row,
        out_shape=jax.ShapeDtypeStruct((T, D_MODEL), F32),
        compiler_params=_params("parallel"),
        name="mlp",
    )(x2, gpre, wup, wdn, gpost)


def _rope_tables(positions):
    inv = ROPE_THETA ** (-jnp.arange(0, DIFF_HEAD_DIM, 2, dtype=F32) / DIFF_HEAD_DIM)
    ang = inv.reshape(-1, 1) * positions.astype(F32).reshape(1, -1)
    return jnp.cos(ang), jnp.sin(ang)


def kernel(x, mem, positions, norm_mix_pre, norm_mix_post, norm_mem_q, norm_mem_kv, norm_mem_post,
           norm_mlp_pre, norm_mlp_post, w_in, conv_w, conv_b, dt_bias, a_log, d_skip, ssd_norm_w,
           lambda_q1, lambda_k1, lambda_q2, lambda_k2, subln_w, w_out, w_mq, w_mk, w_mv, w_mo, w_up, w_down):
    B, S, _ = x.shape
    M = mem.shape[1]
    T = B * S
    assert norm_mix_pre.shape[0] == 1, "single-layer trunk"
    x2 = x.reshape(T, D_MODEL)
    cos, sin = _rope_tables(positions)

    (km, vm), (wz, wxbc, wq, wk, wvdt) = _memkv(mem.reshape(B * M, D_MODEL), norm_mem_kv, w_mk[0], w_mv[0],
                                                w_in[0].T, tm=min(ROW_TILE, B * M))
    head_col = lambda p: jnp.pad(p.reshape(SSD_HEADS, 1), ((0, DT_ROWS - SSD_HEADS), (0, 0)))

    zg, xc, dtt, q, k, vt = _inproj(x2, norm_mix_pre, cos, sin, wz, wxbc, wq, wk, wvdt,
                                    conv_w[0, :, 0, :], conv_b, head_col(dt_bias), S, tm=INPROJ_ROWS)

    y_ssd = _ssd(xc, zg, dtt, head_col(a_log), jnp.repeat(d_skip, SSD_HEAD_DIM, axis=1), ssd_norm_w, B, S)

    lamv = jnp.concatenate([lambda_q1, lambda_k1, lambda_q2, lambda_k2], axis=0)
    y_diff = _diffattn(lamv, subln_w.reshape(2 * DIFF_HEAD_DIM, 1), q, k, vt, B, S, tq=DIFF_Q_BLOCK)

    x2b, wup, wdn = _mixmem(y_ssd, y_diff, x2, km, vm, w_out[0], norm_mix_post, norm_mem_q, w_mq[0], w_mo[0],
                            norm_mem_post, w_up[0], w_down[0], S, M, tm=ROW_TILE)
    out = _mlp(x2b, norm_mlp_pre, wup, wdn, norm_mlp_post, tm=ROW_TILE, tf=MLP_FF_CHUNK)
    return out.reshape(B, S, D_MODEL)
```

```python
import functools
import itertools
import math

import jax
import jax.numpy as jnp
from jax import lax
from jax.experimental import pallas as pl
from jax.experimental.pallas import tpu as pltpu

F32 = jnp.float32
BF16 = jnp.bfloat16

D_MODEL = 1024
D_SSD = 512
SSD_HEAD_DIM = 64
SSD_HEADS = D_SSD // SSD_HEAD_DIM
SSD_GROUPS = 2
SSD_STATE = 128
CONV_WIDTH = 4
CHUNK = 128
D_XBC = D_SSD + 2 * SSD_GROUPS * SSD_STATE
D_DIFF = D_MODEL - D_SSD
DIFF_HEAD_DIM = 64
DIFF_HEADS = D_DIFF // (2 * DIFF_HEAD_DIM)
ROPE_THETA = 10000.0
MEM_HEADS = 4
MEM_HEAD_DIM = D_MODEL // MEM_HEADS
D_FF = 4 * D_MODEL
NORM_EPS = 1e-6
LAMBDA_INIT = 0.8 - 0.6 * math.exp(-0.3 * 0)

LOG2E = math.log2(math.e)
LANES = 128
SUBLANES = 8
ONES_ROWS = 16
DT_ROWS = 16
CONV_COLS = 256
SSD_SEQS_PER_STEP = 4
SSD_CHUNKS_PER_STEP = 4
DIFF_HEADS_PER_STEP = 2
ROW_GROUP = 256
INPROJ_ROWS = 512
ROW_TILE = 1024
DIFF_Q_BLOCK = 512
MLP_FF_CHUNK = 1024
WEIGHT_CAST_ROWS = 128
VMEM_LIMIT = 56 * 1024 * 1024


def _dot(a, b):
    return jnp.dot(a, b, preferred_element_type=F32)


def _dot_nt(a, b):
    return lax.dot_general(a, b, (((1,), (1,)), ((), ())), preferred_element_type=F32)


def _rms(x, g):
    ms = jnp.mean(x * x, axis=-1, keepdims=True)
    return x * lax.rsqrt(ms + NORM_EPS) * g


def _silu_of_half(half):
    return half + half * jnp.tanh(half)


def _const_spec(shape):
    nd = len(shape)
    return pl.BlockSpec(shape, lambda *_: (0,) * nd, pipeline_mode=pl.Buffered(1))


def _params(*sem):
    return pltpu.CompilerParams(dimension_semantics=sem, vmem_limit_bytes=VMEM_LIMIT)


def _inproj_kernel(x_ref, g_ref, cos_ref, sin_ref, wz_ref, wxbc_ref, wq_ref, wk_ref, wvdt_ref,
                   cw_ref, cb_ref, dtb_ref,
                   zg_ref, xc_ref, dtt_ref, q_ref, k_ref, vt_ref, halo_ref, *, tm, tiles_per_seq):
    @pl.when(pl.program_id(0) % tiles_per_seq == 0)
    def _():
        halo_ref[...] = jnp.zeros_like(halo_ref)

    vregs = ROW_GROUP // SUBLANES
    sub = lax.broadcasted_iota(jnp.int32, (1, SUBLANES, 1), 1)
    reps = LANES // (DIFF_HEAD_DIM // 2)
    lane = lax.broadcasted_iota(jnp.int32, (ROW_GROUP, LANES), 1)
    history = {}

    def row_group(rows):
        h = _rms(x_ref[rows, :], g_ref[...]).astype(BF16)

        def gate(z):
            zg_ref[rows, :] = _silu_of_half(0.5 * z).astype(BF16)

        def conv(c, xbc):
            cols = slice(c * CONV_COLS, (c + 1) * CONV_COLS)
            xb = xbc.reshape(vregs, SUBLANES, CONV_COLS)
            prev = history[c] if c in history else halo_ref[:, cols]
            w0, w1, w2, w3 = [0.5 * cw_ref[j:j + 1, cols] for j in range(CONV_WIDTH)]

            def shift(a, a_prev, k):
                r = pltpu.roll(a, k, 1)
                r_first = pltpu.roll(a_prev, k, 0)[None]
                return jnp.where(sub >= k, r, jnp.concatenate([r_first, r[:vregs - 1]], axis=0))

            x1 = shift(xb, prev, 1)
            pair = w1 * xb + w0 * x1
            pair_prev = w1 * prev + w0 * pltpu.roll(prev, 1, 0)
            acc = 0.5 * cb_ref[:, cols] + w3 * xb + w2 * x1 + shift(pair, pair_prev, 2)
            history[c] = xb[vregs - 1]
            xc_ref[rows, cols] = _silu_of_half(acc.reshape(ROW_GROUP, CONV_COLS)).astype(BF16)

        def values_and_steps(r):
            vt_ref[:, rows] = r[:D_DIFF].astype(BF16)
            dtt_ref[:, rows] = jax.nn.softplus(r[D_DIFF:] + dtb_ref[...])

        tables = []

        def rope(a, c):
            if not tables:
                cos = jnp.concatenate([cos_ref[:, rows]] * reps, axis=0).T
                sin = jnp.concatenate([sin_ref[:, rows]] * reps, axis=0).T
                tables.extend([cos, jnp.where(lane < LANES // 2, -sin, sin)])
            ac = a[:, c * LANES:(c + 1) * LANES]
            return ac * tables[0] + pltpu.roll(ac, LANES // 2, 1) * tables[1]

        def keys(a):
            for c in range(DIFF_HEADS):
                k_ref[rows, c * LANES:(c + 1) * LANES] = rope(a, c).astype(BF16)

        def queries(a):
            for c in range(DIFF_HEADS):
                q_ref[rows, c * LANES:(c + 1) * LANES] = (rope(a, c) * (DIFF_HEAD_DIM ** -0.5 * LOG2E)).astype(BF16)

        light = [(lambda: _dot_nt(wvdt_ref[...], h), values_and_steps), (lambda: _dot(h, wz_ref[...]), gate),
                 (lambda: _dot(h, wk_ref[...]), keys), (lambda: _dot(h, wq_ref[...]), queries)]
        for c in range(D_XBC // CONV_COLS):
            for matmul, epilogue in ((functools.partial(lambda c: _dot(h, wxbc_ref[:, c * CONV_COLS:(c + 1) * CONV_COLS]), c),
                                      functools.partial(conv, c)), light[c]):
                result = matmul()
                yield
                epilogue(result)

    for _ in itertools.zip_longest(*[row_group(slice(r, r + ROW_GROUP)) for r in range(0, tm, ROW_GROUP)]):
        pass
    for c, last_rows in history.items():
        halo_ref[:, c * CONV_COLS:(c + 1) * CONV_COLS] = last_rows


def _inproj(x2, g, cos, sin, wz, wxbc, wq, wk, wvdt, cw, cb, dtb_col, S, tm):
    T = x2.shape[0]
    row = lambda n: pl.BlockSpec((tm, n), lambda i: (i, 0))
    col = lambda n: pl.BlockSpec((n, tm), lambda i: (0, i))
    return pl.pallas_call(
        functools.partial(_inproj_kernel, tm=tm, tiles_per_seq=S // tm),
        grid=(T // tm,),
        in_specs=[row(D_MODEL), _const_spec((1, D_MODEL)), col(cos.shape[0]), col(sin.shape[0]),
                  _const_spec(wz.shape), _const_spec(wxbc.shape),
                  _const_spec(wq.shape), _const_spec(wk.shape), _const_spec(wvdt.shape),
                  _const_spec(cw.shape), _const_spec(cb.shape), _const_spec(dtb_col.shape)],
        out_specs=[row(D_SSD), row(D_XBC), col(DT_ROWS), row(D_DIFF), row(D_DIFF), col(D_DIFF)],
        out_shape=[jax.ShapeDtypeStruct((T, D_SSD), BF16),
                   jax.ShapeDtypeStruct((T, D_XBC), BF16),
                   jax.ShapeDtypeStruct((DT_ROWS, T), F32),
                   jax.ShapeDtypeStruct((T, D_DIFF), BF16),
                   jax.ShapeDtypeStruct((T, D_DIFF), BF16),
                   jax.ShapeDtypeStruct((D_DIFF, T), BF16)],
        scratch_shapes=[pltpu.VMEM((SUBLANES, D_XBC), F32)],
        compiler_params=_params("arbitrary"),
        name="inproj",
    )(x2, g, cos, sin, wz, wxbc, wq, wk, wvdt, cw, cb, dtb_col)


def _ssd_chunk(xc_ref, zg_ref, dt_t, alog_ref, dskip_ref, nw_ref, y_ref, state_ref, lane, causal, tri_t):
    da_t = dt_t * (-LOG2E * jnp.exp(alog_ref[...]))
    hi = da_t.astype(BF16)
    rem = da_t - hi.astype(F32)
    mid = rem.astype(BF16)
    lo = (rem - mid.astype(F32)).astype(BF16)
    cum_t = _dot(hi, tri_t) + _dot(mid, tri_t) + _dot(lo, tri_t)
    yield
    last = cum_t[:, CHUNK - 1:CHUNK]
    wrow_t = dt_t * jnp.exp2(last - cum_t)
    cdec = jnp.exp2(last)
    src_t = cum_t - jnp.log2(dt_t)
    cum = jnp.concatenate([cum_t, jnp.zeros((CHUNK - DT_ROWS, CHUNK), F32)], axis=0).T
    yield

    gn = SSD_GROUPS * SSD_STATE
    pairs_per_group = SSD_HEADS // SSD_GROUPS // 2
    y_pairs = []
    for g in range(SSD_GROUPS):
        bg = xc_ref[:, D_SSD + g * SSD_STATE:D_SSD + (g + 1) * SSD_STATE]
        cg = xc_ref[:, D_SSD + gn + g * SSD_STATE:D_SSD + gn + (g + 1) * SSD_STATE]
        cbm = _dot_nt(cg, bg)
        bgt = bg.astype(F32).T
        cg32 = cg.astype(F32)
        yield
        for jp in range(pairs_per_group):
            j = g * pairs_per_group + jp
            xs_pair = xc_ref[:, j * LANES:(j + 1) * LANES]
            lhs, rhs = [], []
            for half in range(2):
                hd = 2 * j + half
                colb = jnp.broadcast_to(cum[:, hd:hd + 1], (CHUNK, LANES))
                dec = jnp.where(causal, jnp.exp2(colb - src_t[hd:hd + 1, :]), 0.0)
                wp = (cbm * dec).astype(BF16)
                gg = (cg32 * jnp.exp2(colb)).astype(BF16)
                in_half = (lane >= SSD_HEAD_DIM) if half else (lane < SSD_HEAD_DIM)
                xs_m = jnp.where(in_half, xs_pair, jnp.zeros_like(xs_pair))
                st = state_ref[hd]
                lhs += [wp, gg]
                rhs += [xs_m, st.astype(BF16)]
                l2 = (bgt * wrow_t[hd:hd + 1, :]).astype(BF16)
                state_ref[hd] = st * cdec[hd:hd + 1, :] + _dot(l2, xs_m)
                yield
            y_pairs.append(_dot(jnp.concatenate(lhs, axis=1), jnp.concatenate(rhs, axis=0)))

    y = jnp.concatenate(y_pairs, axis=1) + dskip_ref[...] * xc_ref[:, :D_SSD].astype(F32)
    y = y * zg_ref[...].astype(F32)
    y_ref[...] = _rms(y, nw_ref[...]).astype(BF16)


def _ssd_kernel(xc_ref, zg_ref, *rest):
    dtt_refs = rest[:SSD_SEQS_PER_STEP]
    alog_ref, dskip_ref, nw_ref, y_ref, state_ref = rest[SSD_SEQS_PER_STEP:]

    @pl.when(pl.program_id(1) == 0)
    def _():
        state_ref[...] = jnp.zeros_like(state_ref)

    row = lax.broadcasted_iota(jnp.int32, (CHUNK, LANES), 0)
    lane = lax.broadcasted_iota(jnp.int32, (CHUNK, LANES), 1)
    causal = row >= lane
    tri_t = jnp.where(row <= lane, 1.0, 0.0).astype(BF16)
    for sc in range(SSD_CHUNKS_PER_STEP):
        t = pl.ds(sc * CHUNK, CHUNK)
        chunks = [_ssd_chunk(xc_ref.at[0, p, t], zg_ref.at[0, p, t], dtt_refs[p][:, sc * CHUNK:(sc + 1) * CHUNK],
                             alog_ref, dskip_ref, nw_ref, y_ref.at[0, p, t], state_ref.at[p], lane, causal, tri_t)
                  for p in range(SSD_SEQS_PER_STEP)]
        for _ in itertools.zip_longest(*chunks):
            pass


def _ssd(xc, zg, dtt, alog_col, dskip, nw, B, S):
    span = SSD_CHUNKS_PER_STEP * CHUNK
    nc = S // span
    nb = SSD_SEQS_PER_STEP
    seqs = lambda a: a.reshape(B // nb, nb, S, a.shape[-1])
    row = lambda n: pl.BlockSpec((1, nb, span, n), lambda b, c: (b, 0, c, 0))
    y = pl.pallas_call(
        _ssd_kernel,
        grid=(B // nb, nc),
        in_specs=[row(D_XBC), row(D_SSD)]
        + [pl.BlockSpec((DT_ROWS, span), functools.partial(lambda p, b, c: (0, (b * nb + p) * nc + c), p))
           for p in range(nb)]
        + [_const_spec(alog_col.shape), _const_spec(dskip.shape), _const_spec(nw.shape)],
        out_specs=row(D_SSD),
        out_shape=jax.ShapeDtypeStruct((B // nb, nb, S, D_SSD), BF16),
        scratch_shapes=[pltpu.VMEM((nb, SSD_HEADS, SSD_STATE, LANES), F32)],
        compiler_params=_params("parallel", "arbitrary"),
        name="ssd",
    )(seqs(xc), seqs(zg), *([dtt] * nb), alog_col, dskip, nw)
    return y.reshape(B * S, D_SSD)


def _diffattn_kernel(lam_ref, subw_ref, q_ref, k_ref, vt_ref, o_ref, sa_scr, sb_scr, acc_scr, m_scr, qc_scr, *,
                     tq, tk, nq, heads):
    hd = 2 * DIFF_HEAD_DIM
    ones = jnp.ones((ONES_ROWS, tk), BF16)
    krow = lax.broadcasted_iota(jnp.int32, (tk, tk), 0)
    qcol = lax.broadcasted_iota(jnp.int32, (tk, tk), 1)
    tri = krow <= qcol
    lane = lax.broadcasted_iota(jnp.int32, (tk, hd), 1)
    comp0 = (lane % DIFF_HEAD_DIM) < DIFF_HEAD_DIM // 2
    lv = lam_ref[...]
    lam = (jnp.exp(jnp.sum(lv[0:1] * lv[1:2], axis=-1, keepdims=True))
           - jnp.exp(jnp.sum(lv[2:3] * lv[3:4], axis=-1, keepdims=True)) + LAMBDA_INIT)

    steps = []
    for i in range(nq):
        steps += [(i, kb, "full") for kb in range(2 * i)] + [(i, 2 * i, "diag0"), (i, 2 * i + 1, "diag1")]

    def query_cols(kind):
        return (tk, tq) if kind == "diag1" else (0, tq)

    def head(hh):
        acc = acc_scr.at[hh]
        mx = m_scr.at[hh]
        bufs = (sa_scr.at[hh], sb_scr.at[hh])
        feat = slice(hh * hd, (hh + 1) * hd)
        qc = qc_scr.at[hh]
        for r in range(0, nq * tq, tk):
            qblk = q_ref[r:r + tk, feat]
            qc[0, r:r + tk, :] = jnp.where(comp0, qblk, jnp.zeros_like(qblk))
            qc[1, r:r + tk, :] = jnp.where(comp0, jnp.zeros_like(qblk), qblk)

        def scores(step, s_ref):
            i, kb, kind = step
            lo, hi = query_cols(kind)
            kblk = k_ref[kb * tk:(kb + 1) * tk, feat]
            for c in range(2):
                s = _dot_nt(kblk, qc[c, i * tq + lo:i * tq + hi, :])
                if kind == "diag0":
                    s = jnp.concatenate([jnp.where(tri, s[:, :tk], -jnp.inf), s[:, tk:]], axis=1)
                elif kind == "diag1":
                    s = jnp.where(tri, s, -jnp.inf)
                s_ref[c, :, lo:hi] = s

        def softmax_pv(step, s_ref):
            i, kb, kind = step
            lo, hi = query_cols(kind)
            par = i % 2
            lhs = jnp.concatenate([vt_ref[feat, kb * tk:(kb + 1) * tk], ones], axis=0)
            for c in range(2):
                m_blk = jnp.max(s_ref[c, :, lo:hi], axis=0, keepdims=True)
                if kb == 0:
                    m_new = m_blk
                    acc[par, c, :, lo:hi] = _dot(lhs, jnp.exp2(s_ref[c, :, lo:hi] - m_new).astype(BF16))
                else:
                    m_old = mx[par, c, :, lo:hi]
                    m_new = jnp.maximum(m_old, m_blk)
                    alpha = jnp.exp2(m_old - m_new)
                    p = jnp.exp2(s_ref[c, :, lo:hi] - m_new).astype(BF16)
                    acc[par, c, :, lo:hi] = alpha * acc[par, c, :, lo:hi] + _dot(lhs, p)
                mx[par, c, :, lo:hi] = m_new

        def finish(i):
            par = i % 2
            a0 = acc[par, 0]
            a1 = acc[par, 1]
            ot = a0[:hd] * (1.0 / a0[hd:hd + 1]) - a1[:hd] * (lam / a1[hd:hd + 1])
            ms = jnp.mean(ot * ot, axis=0, keepdims=True)
            ot = ot * (lax.rsqrt(ms + NORM_EPS) * (1.0 - LAMBDA_INIT)) * subw_ref[...]
            o_ref[i * tq:(i + 1) * tq, feat] = ot.T.astype(BF16)

        scores(steps[0], bufs[0])
        yield
        for n, step in enumerate(steps):
            if n + 1 < len(steps):
                scores(steps[n + 1], bufs[(n + 1) % 2])
                yield
            softmax_pv(step, bufs[n % 2])
            yield
            if step[2] == "diag1":
                finish(step[0])

    for _ in itertools.zip_longest(*[head(hh) for hh in range(heads)]):
        pass


def _diffattn(lamv, subw_col, q, k, vt, B, S, tq):
    nq = S // tq
    tk = tq // 2
    hd = 2 * DIFF_HEAD_DIM
    hp = DIFF_HEADS_PER_STEP
    return pl.pallas_call(
        functools.partial(_diffattn_kernel, tq=tq, tk=tk, nq=nq, heads=hp),
        grid=(B, DIFF_HEADS // hp),
        in_specs=[_const_spec(lamv.shape), _const_spec(subw_col.shape),
                  pl.BlockSpec((S, hd * hp), lambda b, h: (b, h)),
                  pl.BlockSpec((S, hd * hp), lambda b, h: (b, h)),
                  pl.BlockSpec((hd * hp, S), lambda b, h: (h, b))],
        out_specs=pl.BlockSpec((S, hd * hp), lambda b, h: (b, h)),
        out_shape=jax.ShapeDtypeStruct((B * S, D_DIFF), BF16),
        scratch_shapes=[pltpu.VMEM((hp, 2, tk, tq), F32),
                        pltpu.VMEM((hp, 2, tk, tq), F32),
                        pltpu.VMEM((hp, 2, 2, hd + ONES_ROWS, tq), F32),
                        pltpu.VMEM((hp, 2, 2, 1, tq), F32),
                        pltpu.VMEM((hp, 2, S, hd), BF16)],
        compiler_params=_params("parallel", "parallel"),
        name="diffattn",
    )(lamv, subw_col, q, k, vt)


def _memkv_rows(rows, mem_ref, g_ref, wk_ref, wv_ref, k_ref, v_ref):
    h = _rms(mem_ref[rows, :], g_ref[...]).astype(BF16)
    k = _dot(h, wk_ref[...])
    yield
    k_ref[rows, :] = k.astype(BF16)
    v = _dot(h, wv_ref[...])
    yield
    v_ref[rows, :] = v.astype(BF16)


def _round_weights_once(pairs):
    @pl.when(pl.program_id(0) == 0)
    def _():
        for src, dst in pairs:
            for r in range(0, src.shape[0], WEIGHT_CAST_ROWS):
                dst[r:r + WEIGHT_CAST_ROWS, :] = src[r:r + WEIGHT_CAST_ROWS, :].astype(BF16)


def _split_inproj_weights(w_ref, wz_ref, wxbc_ref, wq_ref, wk_ref, wvdt_ref):
    o1 = D_SSD + D_XBC
    o2 = o1 + SSD_HEADS
    half = DIFF_HEAD_DIM // 2
    for r in range(0, D_SSD, LANES):
        wz_ref[:, r:r + LANES] = w_ref[r:r + LANES, :].T.astype(BF16)
    for r in range(0, D_XBC, LANES):
        wxbc_ref[:, r:r + LANES] = w_ref[D_SSD + r:D_SSD + r + LANES, :].T.astype(BF16)
    for dst, base in ((wq_ref, o2), (wk_ref, o2 + D_DIFF)):
        for h in range(DIFF_HEADS):
            src = base + h * LANES
            head = jnp.concatenate([w_ref[src + s:src + s + half, :] for s in (0, 2 * half, half, 3 * half)], axis=0)
            dst[:, h * LANES:(h + 1) * LANES] = head.T.astype(BF16)
    wvdt_ref[:D_DIFF, :] = w_ref[o2 + 2 * D_DIFF:, :].astype(BF16)
    pad = jnp.zeros((DT_ROWS - SSD_HEADS, w_ref.shape[1]), F32)
    wvdt_ref[D_DIFF:, :] = jnp.concatenate([w_ref[o1:o2, :], pad], axis=0).astype(BF16)


def _memkv_kernel(mem_ref, g_ref, wk32_ref, wv32_ref, win_ref, k_ref, v_ref, wz_ref, wxbc_ref, wq_ref, wkd_ref,
                  wvdt_ref, wk_ref, wv_ref):
    _round_weights_once([(wk32_ref, wk_ref), (wv32_ref, wv_ref)])
    _split_inproj_weights(win_ref, wz_ref, wxbc_ref, wq_ref, wkd_ref, wvdt_ref)
    tm = mem_ref.shape[0]
    groups = [_memkv_rows(slice(r, r + ROW_GROUP), mem_ref, g_ref, wk_ref, wv_ref, k_ref, v_ref)
              for r in range(0, tm, ROW_GROUP)]
    for _ in itertools.zip_longest(*groups):
        pass


def _memkv(mem2, g, wk, wv, w_in_t, tm):
    R = mem2.shape[0]
    steps = R // tm
    kdim = w_in_t.shape[1]
    assert kdim % steps == 0 and (kdim // steps) % LANES == 0
    kp = kdim // steps
    row = pl.BlockSpec((tm, D_MODEL), lambda i: (i, 0))
    rows_of = lambda n: pl.BlockSpec((kp, n), lambda i: (i, 0))
    cols_of = lambda n: pl.BlockSpec((n, kp), lambda i: (0, i))
    major = (D_SSD, D_XBC, D_DIFF, D_DIFF)
    outs = pl.pallas_call(
        _memkv_kernel,
        grid=(steps,),
        in_specs=[row, _const_spec(g.shape), _const_spec(wk.shape), _const_spec(wv.shape),
                  cols_of(w_in_t.shape[0])],
        out_specs=[row, row] + [rows_of(n) for n in major] + [cols_of(D_DIFF + DT_ROWS)],
        out_shape=[jax.ShapeDtypeStruct((R, D_MODEL), BF16)] * 2
        + [jax.ShapeDtypeStruct((kdim, n), BF16) for n in major]
        + [jax.ShapeDtypeStruct((D_DIFF + DT_ROWS, kdim), BF16)],
        scratch_shapes=[pltpu.VMEM(wk.shape, BF16), pltpu.VMEM(wv.shape, BF16)],
        compiler_params=_params("arbitrary"),
        name="memkv",
    )(mem2, g, wk, wv, w_in_t)
    return outs[:2], outs[2:]


def _mixmem_rows(rows, ys_ref, yd_ref, x_ref, km_ref, vm_ref, wo_ref, gmix_ref, gq_ref, wmq_ref,
                 wmo_ref, gmem_ref, x2_ref):
    mixed = _dot(ys_ref[rows, :], wo_ref[:D_SSD, :]) + _dot(yd_ref[rows, :], wo_ref[D_SSD:, :])
    yield
    x1 = x_ref[rows, :] + _rms(mixed, gmix_ref[...])
    qm = _dot(_rms(x1, gq_ref[...]).astype(BF16), wmq_ref[...])
    yield
    qm = (qm * (MEM_HEAD_DIM ** -0.5)).astype(BF16)
    outs = []
    for hd in range(MEM_HEADS):
        sl = slice(hd * MEM_HEAD_DIM, (hd + 1) * MEM_HEAD_DIM)
        s = _dot_nt(qm[:, sl], km_ref[:, sl])
        yield
        e = jnp.exp(s - jnp.max(s, axis=-1, keepdims=True))
        p = e / jnp.sum(e, axis=-1, keepdims=True)
        outs.append(_dot(p.astype(BF16), vm_ref[:, sl]).astype(BF16))
        yield
    c = _dot(jnp.concatenate(outs, axis=1), wmo_ref[...])
    yield
    x2_ref[rows, :] = x1 + _rms(c, gmem_ref[...])


def _mixmem_kernel(ys_ref, yd_ref, x_ref, km_ref, vm_ref, wo32_ref, gmix_ref, gq_ref, wmq32_ref, wmo32_ref,
                   gmem_ref, wup32_ref, wdn32_ref, x2_ref, wup_ref, wdn_ref, wo_ref, wmq_ref, wmo_ref):
    _round_weights_once([(wo32_ref, wo_ref), (wmq32_ref, wmq_ref), (wmo32_ref, wmo_ref)])
    for src, dst in ((wup32_ref, wup_ref), (wdn32_ref, wdn_ref)):
        step = min(WEIGHT_CAST_ROWS, src.shape[0])
        for r in range(0, src.shape[0], step):
            dst[r:r + step, :] = src[r:r + step, :].astype(BF16)
    tm = x_ref.shape[0]
    groups = [_mixmem_rows(slice(r, r + ROW_GROUP), ys_ref, yd_ref, x_ref, km_ref, vm_ref, wo_ref, gmix_ref,
                           gq_ref, wmq_ref, wmo_ref, gmem_ref, x2_ref) for r in range(0, tm, ROW_GROUP)]
    for _ in itertools.zip_longest(*groups):
        pass


def _mixmem(ys, yd, x2, km, vm, wo, gmix, gq, wmq, wmo, gmem, wup, wdn, S, M, tm):
    T = x2.shape[0]
    steps = T // tm
    per_b = S // tm
    row = lambda n: pl.BlockSpec((tm, n), lambda i: (i, 0))
    mem = pl.BlockSpec((M, D_MODEL), lambda i: (i // per_b, 0))
    consts = (wo, gmix, gq, wmq, wmo, gmem)
    assert wup.shape[0] % steps == 0 and wdn.shape[0] % steps == 0
    piece = lambda w: pl.BlockSpec((w.shape[0] // steps, w.shape[1]), lambda i: (i, 0))
    return pl.pallas_call(
        _mixmem_kernel,
        grid=(steps,),
        in_specs=[row(D_SSD), row(D_DIFF), row(D_MODEL), mem, mem] + [_const_spec(c.shape) for c in consts]
        + [piece(wup), piece(wdn)],
        out_specs=[row(D_MODEL), piece(wup), piece(wdn)],
        out_shape=[jax.ShapeDtypeStruct((T, D_MODEL), F32), jax.ShapeDtypeStruct(wup.shape, BF16),
                   jax.ShapeDtypeStruct(wdn.shape, BF16)],
        scratch_shapes=[pltpu.VMEM(w.shape, BF16) for w in (wo, wmq, wmo)],
        compiler_params=_params("arbitrary"),
        name="mixmem",
    )(ys, yd, x2, km, vm, *consts, wup, wdn)


def _mlp_rows(rows, tf, x_ref, gpre_ref, wup_ref, wdn_ref, gpost_ref, o_ref):
    x = x_ref[rows, :]
    h = _rms(x, gpre_ref[...]).astype(BF16)
    acc = jnp.zeros(x.shape, F32)
    for c in range(D_FF // tf):
        u = _dot(h, wup_ref[:, c * tf:(c + 1) * tf])
        yield
        u = jnp.maximum(u, 0.0)
        acc = acc + _dot((u * u).astype(BF16), wdn_ref[c * tf:(c + 1) * tf, :])
        yield
    o_ref[rows, :] = x + _rms(acc, gpost_ref[...])


def _mlp_kernel(*refs, tf):
    tm = refs[0].shape[0]
    groups = [_mlp_rows(slice(r, r + ROW_GROUP), tf, *refs) for r in range(0, tm, ROW_GROUP)]
    for _ in itertools.zip_longest(*groups):
        pass


def _mlp(x2, gpre, wup, wdn, gpost, tm, tf):
    T = x2.shape[0]
    row = pl.BlockSpec((tm, D_MODEL), lambda i: (i, 0))
    return pl.pallas_call(
        functools.partial(_mlp_kernel, tf=tf),
        grid=(T // tm,),
        in_specs=[row, _const_spec(gpre.shape), _const_spec(wup.shape), _const_spec(wdn.shape),
                  _const_spec(gpost.shape)],
        out_specs=row,
        out_shape=jax.ShapeDtypeStruct((T, D_MODEL), F32),
        compiler_params=_params("parallel"),
        name="mlp",
    )(x2, gpre, wup, wdn, gpost)


def _rope_tables(positions):
    inv = ROPE_THETA ** (-jnp.arange(0, DIFF_HEAD_DIM, 2, dtype=F32) / DIFF_HEAD_DIM)
    ang = inv.reshape(-1, 1) * positions.astype(F32).reshape(1, -1)
    return jnp.cos(ang), jnp.sin(ang)


def kernel(x, mem, positions, norm_mix_pre, norm_mix_post, norm_mem_q, norm_mem_kv, norm_mem_post,
           norm_mlp_pre, norm_mlp_post, w_in, conv_w, conv_b, dt_bias, a_log, d_skip, ssd_norm_w,
           lambda_q1, lambda_k1, lambda_q2, lambda_k2, subln_w, w_out, w_mq, w_mk, w_mv, w_mo, w_up, w_down):
    B, S, _ = x.shape
    M = mem.shape[1]
    T = B * S
    assert norm_mix_pre.shape[0] == 1, "single-layer trunk"
    x2 = x.reshape(T, D_MODEL)
    cos, sin = _rope_tables(positions)

    (km, vm), (wz, wxbc, wq, wk, wvdt) = _memkv(mem.reshape(B * M, D_MODEL), norm_mem_kv, w_mk[0], w_mv[0],
                                                w_in[0].T, tm=min(ROW_TILE, B * M))
    head_col = lambda p: jnp.pad(p.reshape(SSD_HEADS, 1), ((0, DT_ROWS - SSD_HEADS), (0, 0)))

    zg, xc, dtt, q, k, vt = _inproj(x2, norm_mix_pre, cos, sin, wz, wxbc, wq, wk, wvdt,
                                    conv_w[0, :, 0, :], conv_b, head_col(dt_bias), S, tm=INPROJ_ROWS)

    y_ssd = _ssd(xc, zg, dtt, head_col(a_log), jnp.repeat(d_skip, SSD_HEAD_DIM, axis=1), ssd_norm_w, B, S)

    lamv = jnp.concatenate([lambda_q1, lambda_k1, lambda_q2, lambda_k2], axis=0)
    y_diff = _diffattn(lamv, subln_w.reshape(2 * DIFF_HEAD_DIM, 1), q, k, vt, B, S, tq=DIFF_Q_BLOCK)

    x2b, wup, wdn = _mixmem(y_ssd, y_diff, x2, km, vm, w_out[0], norm_mix_post, norm_mem_q, w_mq[0], w_mo[0],
                            norm_mem_post, w_up[0], w_down[0], S, M, tm=ROW_TILE)
    out = _mlp(x2b, norm_mlp_pre, wup, wdn, norm_mlp_post, tm=ROW_TILE, tf=MLP_FF_CHUNK)
    return out.reshape(B, S, D_MODEL)
```

```python
import functools
import itertools
import math

import jax
import jax.numpy as jnp
from jax import lax
from jax.experimental import pallas as pl
from jax.experimental.pallas import tpu as pltpu

F32 = jnp.float32
BF16 = jnp.bfloat16

D_MODEL = 1024
D_SSD = 512
SSD_HEAD_DIM = 64
SSD_HEADS = D_SSD // SSD_HEAD_DIM
SSD_GROUPS = 2
SSD_STATE = 128
CONV_WIDTH = 4
CHUNK = 128
D_XBC = D_SSD + 2 * SSD_GROUPS * SSD_STATE
D_DIFF = D_MODEL - D_SSD
DIFF_HEAD_DIM = 64
DIFF_HEADS = D_DIFF // (2 * DIFF_HEAD_DIM)
ROPE_THETA = 10000.0
MEM_HEADS = 4
MEM_HEAD_DIM = D_MODEL // MEM_HEADS
D_FF = 4 * D_MODEL
NORM_EPS = 1e-6
LAMBDA_INIT = 0.8 - 0.6 * math.exp(-0.3 * 0)

LOG2E = math.log2(math.e)
LANES = 128
SUBLANES = 8
ONES_ROWS = 16
DT_ROWS = 16
CONV_COLS = 256
SSD_SEQS_PER_STEP = 4
SSD_CHUNKS_PER_STEP = 4
DIFF_HEADS_PER_STEP = 2
ROW_GROUP = 256
INPROJ_ROWS = 512
ROW_TILE = 1024
DIFF_Q_BLOCK = 512
MLP_FF_CHUNK = 1024
WEIGHT_CAST_ROWS = 128
VMEM_LIMIT = 56 * 1024 * 1024


def _dot(a, b):
    return jnp.dot(a, b, preferred_element_type=F32)


def _dot_nt(a, b):
    return lax.dot_general(a, b, (((1,), (1,)), ((), ())), preferred_element_type=F32)


def _rms(x, g):
    ms = jnp.mean(x * x, axis=-1, keepdims=True)
    return x * lax.rsqrt(ms + NORM_EPS) * g


def _silu_of_half(half):
    return half + half * jnp.tanh(half)


def _const_spec(shape):
    nd = len(shape)
    return pl.BlockSpec(shape, lambda *_: (0,) * nd, pipeline_mode=pl.Buffered(1))


def _params(*sem):
    return pltpu.CompilerParams(dimension_semantics=sem, vmem_limit_bytes=VMEM_LIMIT)


def _inproj_kernel(x_ref, g_ref, cos_ref, sin_ref, wz_ref, wxbc_ref, wq_ref, wk_ref, wvdt_ref,
                   cw_ref, cb_ref, dtb_ref,
                   zg_ref, xc_ref, dtt_ref, q_ref, k_ref, vt_ref, halo_ref, *, tm, tiles_per_seq):
    @pl.when(pl.program_id(0) % tiles_per_seq == 0)
    def _():
        halo_ref[...] = jnp.zeros_like(halo_ref)

    vregs = ROW_GROUP // SUBLANES
    sub = lax.broadcasted_iota(jnp.int32, (1, SUBLANES, 1), 1)
    reps = LANES // (DIFF_HEAD_DIM // 2)
    lane = lax.broadcasted_iota(jnp.int32, (ROW_GROUP, LANES), 1)
    history = {}

    def row_group(rows):
        h = _rms(x_ref[rows, :], g_ref[...]).astype(BF16)

        def gate(z):
            zg_ref[rows, :] = _silu_of_half(0.5 * z).astype(BF16)

        def conv(c, xbc):
            cols = slice(c * CONV_COLS, (c + 1) * CONV_COLS)
            xb = xbc.reshape(vregs, SUBLANES, CONV_COLS)
            prev = history[c] if c in history else halo_ref[:, cols]
            w0, w1, w2, w3 = [0.5 * cw_ref[j:j + 1, cols] for j in range(CONV_WIDTH)]

            def shift(a, a_prev, k):
                r = pltpu.roll(a, k, 1)
                r_first = pltpu.roll(a_prev, k, 0)[None]
                return jnp.where(sub >= k, r, jnp.concatenate([r_first, r[:vregs - 1]], axis=0))

            x1 = shift(xb, prev, 1)
            pair = w1 * xb + w0 * x1
            pair_prev = w1 * prev + w0 * pltpu.roll(prev, 1, 0)
            acc = 0.5 * cb_ref[:, cols] + w3 * xb + w2 * x1 + shift(pair, pair_prev, 2)
            history[c] = xb[vregs - 1]
            xc_ref[rows, cols] = _silu_of_half(acc.reshape(ROW_GROUP, CONV_COLS)).astype(BF16)

        def values_and_steps(r):
            vt_ref[:, rows] = r[:D_DIFF].astype(BF16)
            dtt_ref[:, rows] = jax.nn.softplus(r[D_DIFF:] + dtb_ref[...])

        tables = []

        def rope(a, c):
            if not tables:
                cos = jnp.concatenate([cos_ref[:, rows]] * reps, axis=0).T
                sin = jnp.concatenate([sin_ref[:, rows]] * reps, axis=0).T
                tables.extend([cos, jnp.where(lane < LANES // 2, -sin, sin)])
            ac = a[:, c * LANES:(c + 1) * LANES]
            return ac * tables[0] + pltpu.roll(ac, LANES // 2, 1) * tables[1]

        def keys(a):
            for c in range(DIFF_HEADS):
                k_ref[rows, c * LANES:(c + 1) * LANES] = rope(a, c).astype(BF16)

        def queries(a):
            for c in range(DIFF_HEADS):
                q_ref[rows, c * LANES:(c + 1) * LANES] = (rope(a, c) * (DIFF_HEAD_DIM ** -0.5 * LOG2E)).astype(BF16)

        light = [(lambda: _dot_nt(wvdt_ref[...], h), values_and_steps), (lambda: _dot(h, wz_ref[...]), gate),
                 (lambda: _dot(h, wk_ref[...]), keys), (lambda: _dot(h, wq_ref[...]), queries)]
        for c in range(D_XBC // CONV_COLS):
            for matmul, epilogue in ((functools.partial(lambda c: _dot(h, wxbc_ref[:, c * CONV_COLS:(c + 1) * CONV_COLS]), c),
                                      functools.partial(conv, c)), light[c]):
                result = matmul()
                yield
                epilogue(result)

    for _ in itertools.zip_longest(*[row_group(slice(r, r + ROW_GROUP)) for r in range(0, tm, ROW_GROUP)]):
        pass
    for c, last_rows in history.items():
        halo_ref[:, c * CONV_COLS:(c + 1) * CONV_COLS] = last_rows


def _inproj(x2, g, cos, sin, wz, wxbc, wq, wk, wvdt, cw, cb, dtb_col, S, tm):
    T = x2.shape[0]
    row = lambda n: pl.BlockSpec((tm, n), lambda i: (i, 0))
    col = lambda n: pl.BlockSpec((n, tm), lambda i: (0, i))
    return pl.pallas_call(
        functools.partial(_inproj_kernel, tm=tm, tiles_per_seq=S // tm),
        grid=(T // tm,),
        in_specs=[row(D_MODEL), _const_spec((1, D_MODEL)), col(cos.shape[0]), col(sin.shape[0]),
                  _const_spec(wz.shape), _const_spec(wxbc.shape),
                  _const_spec(wq.shape), _const_spec(wk.shape), _const_spec(wvdt.shape),
                  _const_spec(cw.shape), _const_spec(cb.shape), _const_spec(dtb_col.shape)],
        out_specs=[row(D_SSD), row(D_XBC), col(DT_ROWS), row(D_DIFF), row(D_DIFF), col(D_DIFF)],
        out_shape=[jax.ShapeDtypeStruct((T, D_SSD), BF16),
                   jax.ShapeDtypeStruct((T, D_XBC), BF16),
                   jax.ShapeDtypeStruct((DT_ROWS, T), F32),
                   jax.ShapeDtypeStruct((T, D_DIFF), BF16),
                   jax.ShapeDtypeStruct((T, D_DIFF), BF16),
                   jax.ShapeDtypeStruct((D_DIFF, T), BF16)],
        scratch_shapes=[pltpu.VMEM((SUBLANES, D_XBC), F32)],
        compiler_params=_params("arbitrary"),
        name="inproj",
    )(x2, g, cos, sin, wz, wxbc, wq, wk, wvdt, cw, cb, dtb_col)


def _ssd_chunk(xc_ref, zg_ref, dt_t, alog_ref, dskip_ref, nw_ref, y_ref, state_ref, lane, causal, tri_t):
    da_t = dt_t * (-LOG2E * jnp.exp(alog_ref[...]))
    hi = da_t.astype(BF16)
    rem = da_t - hi.astype(F32)
    mid = rem.astype(BF16)
    lo = (rem - mid.astype(F32)).astype(BF16)
    cum_t = _dot(hi, tri_t) + _dot(mid, tri_t) + _dot(lo, tri_t)
    yield
    last = cum_t[:, CHUNK - 1:CHUNK]
    wrow_t = dt_t * jnp.exp2(last - cum_t)
    cdec = jnp.exp2(last)
    src_t = cum_t - jnp.log2(dt_t)
    cum = jnp.concatenate([cum_t, jnp.zeros((CHUNK - DT_ROWS, CHUNK), F32)], axis=0).T
    yield

    gn = SSD_GROUPS * SSD_STATE
    pairs_per_group = SSD_HEADS // SSD_GROUPS // 2
    y_pairs = []
    for g in range(SSD_GROUPS):
        bg = xc_ref[:, D_SSD + g * SSD_STATE:D_SSD + (g + 1) * SSD_STATE]
        cg = xc_ref[:, D_SSD + gn + g * SSD_STATE:D_SSD + gn + (g + 1) * SSD_STATE]
        cbm = _dot_nt(cg, bg)
        bgt = bg.astype(F32).T
        cg32 = cg.astype(F32)
        yield
        for jp in range(pairs_per_group):
            j = g * pairs_per_group + jp
            xs_pair = xc_ref[:, j * LANES:(j + 1) * LANES]
            lhs, rhs = [], []
            for half in range(2):
                hd = 2 * j + half
                colb = jnp.broadcast_to(cum[:, hd:hd + 1], (CHUNK, LANES))
                dec = jnp.where(causal, jnp.exp2(colb - src_t[hd:hd + 1, :]), 0.0)
                wp = (cbm * dec).astype(BF16)
                gg = (cg32 * jnp.exp2(colb)).astype(BF16)
                in_half = (lane >= SSD_HEAD_DIM) if half else (lane < SSD_HEAD_DIM)
                xs_m = jnp.where(in_half, xs_pair, jnp.zeros_like(xs_pair))
                st = state_ref[hd]
                lhs += [wp, gg]
                rhs += [xs_m, st.astype(BF16)]
                l2 = (bgt * wrow_t[hd:hd + 1, :]).astype(BF16)
                state_ref[hd] = st * cdec[hd:hd + 1, :] + _dot(l2, xs_m)
                yield
            y_pairs.append(_dot(jnp.concatenate(lhs, axis=1), jnp.concatenate(rhs, axis=0)))

    y = jnp.concatenate(y_pairs, axis=1) + dskip_ref[...] * xc_ref[:, :D_SSD].astype(F32)
    y = y * zg_ref[...].astype(F32)
    y_ref[...] = _rms(y, nw_ref[...]).astype(BF16)


def _ssd_kernel(xc_ref, zg_ref, *rest):
    dtt_refs = rest[:SSD_SEQS_PER_STEP]
    alog_ref, dskip_ref, nw_ref, y_ref, state_ref = rest[SSD_SEQS_PER_STEP:]

    @pl.when(pl.program_id(1) == 0)
    def _():
        state_ref[...] = jnp.zeros_like(state_ref)

    row = lax.broadcasted_iota(jnp.int32, (CHUNK, LANES), 0)
    lane = lax.broadcasted_iota(jnp.int32, (CHUNK, LANES), 1)
    causal = row >= lane
    tri_t = jnp.where(row <= lane, 1.0, 0.0).astype(BF16)
    for sc in range(SSD_CHUNKS_PER_STEP):
        t = pl.ds(sc * CHUNK, CHUNK)
        chunks = [_ssd_chunk(xc_ref.at[0, p, t], zg_ref.at[0, p, t], dtt_refs[p][:, sc * CHUNK:(sc + 1) * CHUNK],
                             alog_ref, dskip_ref, nw_ref, y_ref.at[0, p, t], state_ref.at[p], lane, causal, tri_t)
                  for p in range(SSD_SEQS_PER_STEP)]
        for _ in itertools.zip_longest(*chunks):
            pass


def _ssd(xc, zg, dtt, alog_col, dskip, nw, B, S):
    span = SSD_CHUNKS_PER_STEP * CHUNK
    nc = S // span
    nb = SSD_SEQS_PER_STEP
    seqs = lambda a: a.reshape(B // nb, nb, S, a.shape[-1])
    row = lambda n: pl.BlockSpec((1, nb, span, n), lambda b, c: (b, 0, c, 0))
    y = pl.pallas_call(
        _ssd_kernel,
        grid=(B // nb, nc),
        in_specs=[row(D_XBC), row(D_SSD)]
        + [pl.BlockSpec((DT_ROWS, span), functools.partial(lambda p, b, c: (0, (b * nb + p) * nc + c), p))
           for p in range(nb)]
        + [_const_spec(alog_col.shape), _const_spec(dskip.shape), _const_spec(nw.shape)],
        out_specs=row(D_SSD),
        out_shape=jax.ShapeDtypeStruct((B // nb, nb, S, D_SSD), BF16),
        scratch_shapes=[pltpu.VMEM((nb, SSD_HEADS, SSD_STATE, LANES), F32)],
        compiler_params=_params("parallel", "arbitrary"),
        name="ssd",
    )(seqs(xc), seqs(zg), *([dtt] * nb), alog_col, dskip, nw)
    return y.reshape(B * S, D_SSD)


def _diffattn_kernel(lam_ref, subw_ref, q_ref, k_ref, vt_ref, wa32_ref, wb32_ref, wc32_ref,
                     o_ref, wa_ref, wb_ref, wc_ref, sa_scr, sb_scr, acc_scr, m_scr, qc_scr, *,
                     tq, tk, nq, heads):
    hd = 2 * DIFF_HEAD_DIM
    ones = jnp.ones((ONES_ROWS, tk), BF16)
    krow = lax.broadcasted_iota(jnp.int32, (tk, tk), 0)
    qcol = lax.broadcasted_iota(jnp.int32, (tk, tk), 1)
    tri = krow <= qcol
    lane = lax.broadcasted_iota(jnp.int32, (tk, hd), 1)
    comp0 = (lane % DIFF_HEAD_DIM) < DIFF_HEAD_DIM // 2
    lv = lam_ref[...]
    lam = (jnp.exp(jnp.sum(lv[0:1] * lv[1:2], axis=-1, keepdims=True))
           - jnp.exp(jnp.sum(lv[2:3] * lv[3:4], axis=-1, keepdims=True)) + LAMBDA_INIT)

    steps = []
    for i in range(nq):
        steps += [(i, kb, "full") for kb in range(2 * i)] + [(i, 2 * i, "diag0"), (i, 2 * i + 1, "diag1")]

    def query_cols(kind):
        return (tk, tq) if kind == "diag1" else (0, tq)

    def head(hh):
        acc = acc_scr.at[hh]
        mx = m_scr.at[hh]
        bufs = (sa_scr.at[hh], sb_scr.at[hh])
        feat = slice(hh * hd, (hh + 1) * hd)
        qc = qc_scr.at[hh]
        for r in range(0, nq * tq, tk):
            qblk = q_ref[r:r + tk, feat]
            qc[0, r:r + tk, :] = jnp.where(comp0, qblk, jnp.zeros_like(qblk))
            qc[1, r:r + tk, :] = jnp.where(comp0, jnp.zeros_like(qblk), qblk)

        def scores(step, s_ref):
            i, kb, kind = step
            lo, hi = query_cols(kind)
            kblk = k_ref[kb * tk:(kb + 1) * tk, feat]
            for c in range(2):
                s = _dot_nt(kblk, qc[c, i * tq + lo:i * tq + hi, :])
                if kind == "diag0":
                    s = jnp.concatenate([jnp.where(tri, s[:, :tk], -jnp.inf), s[:, tk:]], axis=1)
                elif kind == "diag1":
                    s = jnp.where(tri, s, -jnp.inf)
                s_ref[c, :, lo:hi] = s

        def softmax_pv(step, s_ref):
            i, kb, kind = step
            lo, hi = query_cols(kind)
            par = i % 2
            lhs = jnp.concatenate([vt_ref[feat, kb * tk:(kb + 1) * tk], ones], axis=0)
            for c in range(2):
                m_blk = jnp.max(s_ref[c, :, lo:hi], axis=0, keepdims=True)
                if kb == 0:
                    m_new = m_blk
                    acc[par, c, :, lo:hi] = _dot(lhs, jnp.exp2(s_ref[c, :, lo:hi] - m_new).astype(BF16))
                else:
                    m_old = mx[par, c, :, lo:hi]
                    m_new = jnp.maximum(m_old, m_blk)
                    alpha = jnp.exp2(m_old - m_new)
                    p = jnp.exp2(s_ref[c, :, lo:hi] - m_new).astype(BF16)
                    acc[par, c, :, lo:hi] = alpha * acc[par, c, :, lo:hi] + _dot(lhs, p)
                mx[par, c, :, lo:hi] = m_new

        def finish(i):
            par = i % 2
            a0 = acc[par, 0]
            a1 = acc[par, 1]
            ot = a0[:hd] * (1.0 / a0[hd:hd + 1]) - a1[:hd] * (lam / a1[hd:hd + 1])
            ms = jnp.mean(ot * ot, axis=0, keepdims=True)
            ot = ot * (lax.rsqrt(ms + NORM_EPS) * (1.0 - LAMBDA_INIT)) * subw_ref[...]
            o_ref[i * tq:(i + 1) * tq, feat] = ot.T.astype(BF16)

        scores(steps[0], bufs[0])
        yield
        for n, step in enumerate(steps):
            if n + 1 < len(steps):
                scores(steps[n + 1], bufs[(n + 1) % 2])
                yield
            softmax_pv(step, bufs[n % 2])
            yield
            if step[2] == "diag1":
                finish(step[0])

    for _ in itertools.zip_longest(*[head(hh) for hh in range(heads)]):
        pass
    for src, dst in ((wa32_ref, wa_ref), (wb32_ref, wb_ref), (wc32_ref, wc_ref)):
        dst[...] = src[...].astype(BF16)


def _diffattn(lamv, subw_col, q, k, vt, next_weights, B, S, tq):
    nq = S // tq
    tk = tq // 2
    hd = 2 * DIFF_HEAD_DIM
    hp = DIFF_HEADS_PER_STEP
    nh = DIFF_HEADS // hp
    steps = B * nh
    assert all(w.shape[0] % steps == 0 and (w.shape[0] // steps) % ONES_ROWS == 0 for w in next_weights)
    piece = lambda w: pl.BlockSpec((w.shape[0] // steps, w.shape[1]), lambda b, h: (b * nh + h, 0))
    return pl.pallas_call(
        functools.partial(_diffattn_kernel, tq=tq, tk=tk, nq=nq, heads=hp),
        grid=(B, nh),
        in_specs=[_const_spec(lamv.shape), _const_spec(subw_col.shape),
                  pl.BlockSpec((S, hd * hp), lambda b, h: (b, h)),
                  pl.BlockSpec((S, hd * hp), lambda b, h: (b, h)),
                  pl.BlockSpec((hd * hp, S), lambda b, h: (h, b))] + [piece(w) for w in next_weights],
        out_specs=[pl.BlockSpec((S, hd * hp), lambda b, h: (b, h))] + [piece(w) for w in next_weights],
        out_shape=[jax.ShapeDtypeStruct((B * S, D_DIFF), BF16)]
        + [jax.ShapeDtypeStruct(w.shape, BF16) for w in next_weights],
        scratch_shapes=[pltpu.VMEM((hp, 2, tk, tq), F32),
                        pltpu.VMEM((hp, 2, tk, tq), F32),
                        pltpu.VMEM((hp, 2, 2, hd + ONES_ROWS, tq), F32),
                        pltpu.VMEM((hp, 2, 2, 1, tq), F32),
                        pltpu.VMEM((hp, 2, S, hd), BF16)],
        compiler_params=_params("parallel", "parallel"),
        name="diffattn",
    )(lamv, subw_col, q, k, vt, *next_weights)


def _memkv_rows(rows, mem_ref, g_ref, wk_ref, wv_ref, k_ref, v_ref):
    h = _rms(mem_ref[rows, :], g_ref[...]).astype(BF16)
    k = _dot(h, wk_ref[...])
    yield
    k_ref[rows, :] = k.astype(BF16)
    v = _dot(h, wv_ref[...])
    yield
    v_ref[rows, :] = v.astype(BF16)


def _round_weights_once(pairs):
    @pl.when(pl.program_id(0) == 0)
    def _():
        for src, dst in pairs:
            for r in range(0, src.shape[0], WEIGHT_CAST_ROWS):
                dst[r:r + WEIGHT_CAST_ROWS, :] = src[r:r + WEIGHT_CAST_ROWS, :].astype(BF16)


def _split_inproj_weights(w_ref, wz_ref, wxbc_ref, wq_ref, wk_ref, wvdt_ref):
    o1 = D_SSD + D_XBC
    o2 = o1 + SSD_HEADS
    half = DIFF_HEAD_DIM // 2
    for r in range(0, D_SSD, LANES):
        wz_ref[:, r:r + LANES] = w_ref[r:r + LANES, :].T.astype(BF16)
    for r in range(0, D_XBC, LANES):
        wxbc_ref[:, r:r + LANES] = w_ref[D_SSD + r:D_SSD + r + LANES, :].T.astype(BF16)
    for dst, base in ((wq_ref, o2), (wk_ref, o2 + D_DIFF)):
        for h in range(DIFF_HEADS):
            src = base + h * LANES
            head = jnp.concatenate([w_ref[src + s:src + s + half, :] for s in (0, 2 * half, half, 3 * half)], axis=0)
            dst[:, h * LANES:(h + 1) * LANES] = head.T.astype(BF16)
    wvdt_ref[:D_DIFF, :] = w_ref[o2 + 2 * D_DIFF:, :].astype(BF16)
    pad = jnp.zeros((DT_ROWS - SSD_HEADS, w_ref.shape[1]), F32)
    wvdt_ref[D_DIFF:, :] = jnp.concatenate([w_ref[o1:o2, :], pad], axis=0).astype(BF16)


def _memkv_kernel(mem_ref, g_ref, wk32_ref, wv32_ref, win_ref, k_ref, v_ref, wz_ref, wxbc_ref, wq_ref, wkd_ref,
                  wvdt_ref, wk_ref, wv_ref):
    _round_weights_once([(wk32_ref, wk_ref), (wv32_ref, wv_ref)])
    _split_inproj_weights(win_ref, wz_ref, wxbc_ref, wq_ref, wkd_ref, wvdt_ref)
    tm = mem_ref.shape[0]
    groups = [_memkv_rows(slice(r, r + ROW_GROUP), mem_ref, g_ref, wk_ref, wv_ref, k_ref, v_ref)
              for r in range(0, tm, ROW_GROUP)]
    for _ in itertools.zip_longest(*groups):
        pass


def _memkv(mem2, g, wk, wv, w_in_t, tm):
    R = mem2.shape[0]
    steps = R // tm
    kdim = w_in_t.shape[1]
    assert kdim % steps == 0 and (kdim // steps) % LANES == 0
    kp = kdim // steps
    row = pl.BlockSpec((tm, D_MODEL), lambda i: (i, 0))
    rows_of = lambda n: pl.BlockSpec((kp, n), lambda i: (i, 0))
    cols_of = lambda n: pl.BlockSpec((n, kp), lambda i: (0, i))
    major = (D_SSD, D_XBC, D_DIFF, D_DIFF)
    outs = pl.pallas_call(
        _memkv_kernel,
        grid=(steps,),
        in_specs=[row, _const_spec(g.shape), _const_spec(wk.shape), _const_spec(wv.shape),
                  cols_of(w_in_t.shape[0])],
        out_specs=[row, row] + [rows_of(n) for n in major] + [cols_of(D_DIFF + DT_ROWS)],
        out_shape=[jax.ShapeDtypeStruct((R, D_MODEL), BF16)] * 2
        + [jax.ShapeDtypeStruct((kdim, n), BF16) for n in major]
        + [jax.ShapeDtypeStruct((D_DIFF + DT_ROWS, kdim), BF16)],
        scratch_shapes=[pltpu.VMEM(wk.shape, BF16), pltpu.VMEM(wv.shape, BF16)],
        compiler_params=_params("arbitrary"),
        name="memkv",
    )(mem2, g, wk, wv, w_in_t)
    return outs[:2], outs[2:]


def _mixmem_rows(rows, ys_ref, yd_ref, x_ref, km_ref, vm_ref, wo_ref, gmix_ref, gq_ref, wmq_ref,
                 wmo_ref, gmem_ref, x2_ref):
    mixed = _dot(ys_ref[rows, :], wo_ref[:D_SSD, :]) + _dot(yd_ref[rows, :], wo_ref[D_SSD:, :])
    yield
    x1 = x_ref[rows, :] + _rms(mixed, gmix_ref[...])
    qm = _dot(_rms(x1, gq_ref[...]).astype(BF16), wmq_ref[...])
    yield
    qm = (qm * (MEM_HEAD_DIM ** -0.5)).astype(BF16)
    outs = []
    for hd in range(MEM_HEADS):
        sl = slice(hd * MEM_HEAD_DIM, (hd + 1) * MEM_HEAD_DIM)
        s = _dot_nt(qm[:, sl], km_ref[:, sl])
        yield
        e = jnp.exp(s - jnp.max(s, axis=-1, keepdims=True))
        p = e / jnp.sum(e, axis=-1, keepdims=True)
        outs.append(_dot(p.astype(BF16), vm_ref[:, sl]).astype(BF16))
        yield
    c = _dot(jnp.concatenate(outs, axis=1), wmo_ref[...])
    yield
    x2_ref[rows, :] = x1 + _rms(c, gmem_ref[...])


def _mixmem_kernel(ys_ref, yd_ref, x_ref, km_ref, vm_ref, wo_ref, gmix_ref, gq_ref, wmq_ref, wmo_ref,
                   gmem_ref, wup32_ref, wdn32_ref, x2_ref, wup_ref, wdn_ref):
    for src, dst in ((wup32_ref, wup_ref), (wdn32_ref, wdn_ref)):
        step = min(WEIGHT_CAST_ROWS, src.shape[0])
        for r in range(0, src.shape[0], step):
            dst[r:r + step, :] = src[r:r + step, :].astype(BF16)
    tm = x_ref.shape[0]
    groups = [_mixmem_rows(slice(r, r + ROW_GROUP), ys_ref, yd_ref, x_ref, km_ref, vm_ref, wo_ref, gmix_ref,
                           gq_ref, wmq_ref, wmo_ref, gmem_ref, x2_ref) for r in range(0, tm, ROW_GROUP)]
    for _ in itertools.zip_longest(*groups):
        pass


def _mixmem(ys, yd, x2, km, vm, wo, gmix, gq, wmq, wmo, gmem, wup, wdn, S, M, tm):
    T = x2.shape[0]
    steps = T // tm
    per_b = S // tm
    row = lambda n: pl.BlockSpec((tm, n), lambda i: (i, 0))
    mem = pl.BlockSpec((M, D_MODEL), lambda i: (i // per_b, 0))
    consts = (wo, gmix, gq, wmq, wmo, gmem)
    assert wup.shape[0] % steps == 0 and wdn.shape[0] % steps == 0
    piece = lambda w: pl.BlockSpec((w.shape[0] // steps, w.shape[1]), lambda i: (i, 0))
    return pl.pallas_call(
        _mixmem_kernel,
        grid=(steps,),
        in_specs=[row(D_SSD), row(D_DIFF), row(D_MODEL), mem, mem] + [_const_spec(c.shape) for c in consts]
        + [piece(wup), piece(wdn)],
        out_specs=[row(D_MODEL), piece(wup), piece(wdn)],
        out_shape=[jax.ShapeDtypeStruct((T, D_MODEL), F32), jax.ShapeDtypeStruct(wup.shape, BF16),
                   jax.ShapeDtypeStruct(wdn.shape, BF16)],
        compiler_params=_params("parallel"),
        name="mixmem",
    )(ys, yd, x2, km, vm, *consts, wup, wdn)


def _mlp_rows(rows, tf, x_ref, gpre_ref, wup_ref, wdn_ref, gpost_ref, o_ref):
    x = x_ref[rows, :]
    h = _rms(x, gpre_ref[...]).astype(BF16)
    acc = jnp.zeros(x.shape, F32)
    for c in range(D_FF // tf):
        u = _dot(h, wup_ref[:, c * tf:(c + 1) * tf])
        yield
        u = jnp.maximum(u, 0.0)
        acc = acc + _dot((u * u).astype(BF16), wdn_ref[c * tf:(c + 1) * tf, :])
        yield
    o_ref[rows, :] = x + _rms(acc, gpost_ref[...])


def _mlp_kernel(*refs, tf):
    tm = refs[0].shape[0]
    groups = [_mlp_rows(slice(r, r + ROW_GROUP), tf, *refs) for r in range(0, tm, ROW_GROUP)]
    for _ in itertools.zip_longest(*groups):
        pass


def _mlp(x2, gpre, wup, wdn, gpost, tm, tf):
    T = x2.shape[0]
    row = pl.BlockSpec((tm, D_MODEL), lambda i: (i, 0))
    return pl.pallas_call(
        functools.partial(_mlp_kernel, tf=tf),
        grid=(T // tm,),
        in_specs=[row, _const_spec(gpre.shape), _const_spec(wup.shape), _const_spec(wdn.shape),
                  _const_spec(gpost.shape)],
        out_specs=row,
        out_shape=jax.ShapeDtypeStruct((T, D_MODEL), F32),
        compiler_params=_params("parallel"),
        name="mlp",
    )(x2, gpre, wup, wdn, gpost)


def _rope_tables(positions):
    inv = ROPE_THETA ** (-jnp.arange(0, DIFF_HEAD_DIM, 2, dtype=F32) / DIFF_HEAD_DIM)
    ang = inv.reshape(-1, 1) * positions.astype(F32).reshape(1, -1)
    return jnp.cos(ang), jnp.sin(ang)


def kernel(x, mem, positions, norm_mix_pre, norm_mix_post, norm_mem_q, norm_mem_kv, norm_mem_post,
           norm_mlp_pre, norm_mlp_post, w_in, conv_w, conv_b, dt_bias, a_log, d_skip, ssd_norm_w,
           lambda_q1, lambda_k1, lambda_q2, lambda_k2, subln_w, w_out, w_mq, w_mk, w_mv, w_mo, w_up, w_down):
    B, S, _ = x.shape
    M = mem.shape[1]
    T = B * S
    assert norm_mix_pre.shape[0] == 1, "single-layer trunk"
    x2 = x.reshape(T, D_MODEL)
    cos, sin = _rope_tables(positions)

    (km, vm), (wz, wxbc, wq, wk, wvdt) = _memkv(mem.reshape(B * M, D_MODEL), norm_mem_kv, w_mk[0], w_mv[0],
                                                w_in[0].T, tm=min(ROW_TILE, B * M))
    head_col = lambda p: jnp.pad(p.reshape(SSD_HEADS, 1), ((0, DT_ROWS - SSD_HEADS), (0, 0)))

    zg, xc, dtt, q, k, vt = _inproj(x2, norm_mix_pre, cos, sin, wz, wxbc, wq, wk, wvdt,
                                    conv_w[0, :, 0, :], conv_b, head_col(dt_bias), S, tm=INPROJ_ROWS)

    y_ssd = _ssd(xc, zg, dtt, head_col(a_log), jnp.repeat(d_skip, SSD_HEAD_DIM, axis=1), ssd_norm_w, B, S)

    lamv = jnp.concatenate([lambda_q1, lambda_k1, lambda_q2, lambda_k2], axis=0)
    y_diff, wo, wmq, wmo = _diffattn(lamv, subln_w.reshape(2 * DIFF_HEAD_DIM, 1), q, k, vt,
                                     (w_out[0], w_mq[0], w_mo[0]), B, S, tq=DIFF_Q_BLOCK)

    x2b, wup, wdn = _mixmem(y_ssd, y_diff, x2, km, vm, wo, norm_mix_post, norm_mem_q, wmq, wmo,
                            norm_mem_post, w_up[0], w_down[0], S, M, tm=ROW_TILE)
    out = _mlp(x2b, norm_mlp_pre, wup, wdn, norm_mlp_post, tm=ROW_TILE, tf=MLP_FF_CHUNK)
    return out.reshape(B, S, D_MODEL)
```
